```python
import math
import jax, jax.numpy as jnp
from jax import lax
import numpy as np

D_MODEL = 2048
BATCH = 8
SEQ = 2048
DEPTH = 2
DEC_BATCH = 128
DEC_SEQ = 4
PAST_LEN = 2048
PAGE_SIZE = 128

D_MIX = D_MODEL
D_ML = D_MIX // 2
ML_HEADS = 4
ML_HEAD_DIM = D_ML // ML_HEADS
ML_CHUNK = 64
CONV_W = 4
D_NSA = D_MIX - D_ML
NSA_HEAD_DIM = 64
NSA_HEADS = D_NSA // NSA_HEAD_DIM
NSA_KV_HEADS = 4
NSA_REP = NSA_HEADS // NSA_KV_HEADS
KV_W = NSA_KV_HEADS * NSA_HEAD_DIM
CMP_BLOCK = 32
CMP_STRIDE = 16
CMP_HIDDEN = 256
SEL_BLOCK = 64
N_SEL = 16
WINDOW = 512
NSA_QBLOCK = 32
D_FF = 4 * D_MODEL
D_PLE = 256
N_KV_SLOTS = 4
IN_COLS = 3 * D_ML + 2 * ML_HEADS + D_NSA + 6 * KV_W + 3 * NSA_HEADS
EPS = 1e-6
NEG = -1e30
FORCE = 1e6

kernel_name = "hymba_mlstm_nsa_decoder_step"


def _rms(x):
    xf = x.astype(jnp.float32)
    return (xf * lax.rsqrt(jnp.mean(xf * xf, axis=-1, keepdims=True) + EPS)).astype(x.dtype)


def _rmsnorm(x, g):
    return _rms(x) * g


def _split_in(z):
    sizes = [D_ML, D_ML, D_ML, ML_HEADS, ML_HEADS, D_NSA] + [KV_W] * 6 + [3 * NSA_HEADS]
    offs = np.cumsum(sizes)[:-1].tolist()
    return jnp.split(z, offs, axis=-1)


def _softmax_masked(s, mask, axes):
    s = jnp.where(mask, s, NEG)
    mx = jnp.max(s, axis=axes, keepdims=True)
    e = jnp.where(mask, jnp.exp(s - mx), 0.0)
    den = jnp.sum(e, axis=axes, keepdims=True)
    return e / jnp.maximum(den, 1e-30)


def _mlstm(q, k, v, log_i, log_f, c0, n0, m0):
    b, t, h, d = q.shape
    L = math.gcd(t, ML_CHUNK)
    nc = t // L

    def chunks(a):
        a = a.astype(jnp.float32).reshape((b, nc, L) + a.shape[2:])
        return jnp.swapaxes(jnp.moveaxis(a, 1, 0), 2, 3)

    causal = jnp.tril(jnp.ones((L, L), bool))

    def step(carry, inp):
        C, n, m = carry
        qc, kc, vc, li, lf = inp
        bc = jnp.cumsum(lf, axis=-1)
        logD = bc[..., :, None] - bc[..., None, :] + li[..., None, :]
        logD = jnp.where(causal, logD, -jnp.inf)
        inter = m[..., None] + bc
        mt = jnp.maximum(jnp.max(logD, axis=-1), inter)
        Dm = jnp.exp(logD - mt[..., None])
        a_int = jnp.exp(inter - mt)
        S = jnp.einsum('bhld,bhsd->bhls', qc, kc) * Dm
        num = jnp.einsum('bhls,bhsd->bhld', S, vc) + a_int[..., None] * jnp.einsum('bhld,bhde->bhle', qc, C)
        den = jnp.sum(S, axis=-1) + a_int * jnp.einsum('bhld,bhd->bhl', qc, n)
        hh = num / jnp.maximum(jnp.abs(den), jnp.exp(-mt))[..., None]
        mL = mt[..., -1]
        w = jnp.exp(bc[..., -1:] - bc + li - mL[..., None])
        dec = jnp.exp(m + bc[..., -1] - mL)
        C_new = dec[..., None, None] * C + jnp.einsum('bhs,bhsd,bhse->bhde', w, kc, vc)
        n_new = dec[..., None] * n + jnp.einsum('bhs,bhsd->bhd', w, kc)
        return (C_new, n_new, mL), hh

    carry0 = (c0.astype(jnp.float32), n0.astype(jnp.float32), m0.astype(jnp.float32))
    (cN, nN, mN), hs = lax.scan(step, carry0, (chunks(q), chunks(k), chunks(v), chunks(log_i), chunks(log_f)))
    hs = jnp.moveaxis(jnp.swapaxes(hs, 2, 3), 0, 1).reshape(b, t, h, d)
    return hs, (cN, nN, mN)


def _compress(k, pe, w1, b1, w2, b2):
    bsz, tk, g, dk = k.shape
    nc = (tk - CMP_BLOCK) // CMP_STRIDE + 1
    idx = np.arange(nc)[:, None] * CMP_STRIDE + np.arange(CMP_BLOCK)[None, :]
    blocks = k[:, idx] + pe[:, None, :]
    flat = jnp.swapaxes(blocks, 2, 3).reshape(bsz, nc, g, CMP_BLOCK * dk)
    hid = jax.nn.gelu(flat @ w1 + b1)
    return hid @ w2 + b2


def _sel_blocks(k):
    bsz, tk, g, dk = k.shape
    ns = -(-tk // SEL_BLOCK)
    k = jnp.pad(k, ((0, 0), (0, ns * SEL_BLOCK - tk), (0, 0), (0, 0)))
    return jnp.transpose(k.reshape(bsz, ns, SEL_BLOCK, g, dk), (0, 3, 1, 2, 4))


def _cmp_to_sel_map(nc, ns):
    r = SEL_BLOCK // CMP_STRIDE
    c = CMP_BLOCK // CMP_STRIDE
    j = np.arange(ns)[:, None, None]
    a = np.arange(r)[None, :, None]
    bb = np.arange(c)[None, None, :]
    i = r * j + a - bb
    jj = np.broadcast_to(j, i.shape)
    ok = (i >= 0) & (i < nc)
    M = np.zeros((nc, ns), np.float32)
    np.add.at(M, (i[ok], jj[ok]), 1.0)
    return M


def _gather_blocks(kb, idx):
    return jax.vmap(jax.vmap(lambda a, ix: a[ix]))(kb, idx)


def _nsa_core(q, qpos, gates, kcmp, vcmp, ksb, vsb, kw, vw, kwpos):
    scale = NSA_HEAD_DIM ** -0.5
    nc = kcmp.shape[1]
    blk_end = jnp.arange(nc, dtype=jnp.int32) * CMP_STRIDE + (CMP_BLOCK - 1)
    s = jnp.einsum('btgrd,bngd->bgrtn', q, kcmp, preferred_element_type=jnp.float32) * scale
    p_cmp = _softmax_masked(s, blk_end[None, :] <= qpos[:, None], -1)
    o_cmp = jnp.einsum('bgrtn,bngd->btgrd', p_cmp.astype(vcmp.dtype), vcmp)
    ns = ksb.shape[2]
    imp = jnp.einsum('bgrtn,ns->bgts', p_cmp, jnp.asarray(_cmp_to_sel_map(nc, ns)))
    blk = jnp.arange(ns, dtype=jnp.int32)[None, :]
    cur = (qpos // SEL_BLOCK)[:, None]
    forced = (blk == 0) | (blk == cur) | (blk == cur - 1)
    score = jnp.where(blk <= cur, jnp.where(forced, FORCE, imp), -1.0)
    top, idx = lax.top_k(score, min(N_SEL, ns))
    kg = _gather_blocks(ksb, idx)
    vg = _gather_blocks(vsb, idx)
    kpos = idx[..., None] * SEL_BLOCK + jnp.arange(SEL_BLOCK, dtype=jnp.int32)
    m_sel = (top >= 0)[..., None] & (kpos <= qpos[None, None, :, None, None])
    s = jnp.einsum('btgrd,bgtnld->bgrtnl', q, kg, preferred_element_type=jnp.float32) * scale
    p = _softmax_masked(s, m_sel[:, :, None], (-2, -1))
    o_sel = jnp.einsum('bgrtnl,bgtnld->btgrd', p.astype(vg.dtype), vg)
    s = jnp.einsum('btgrd,bkgd->bgrtk', q, kw, preferred_element_type=jnp.float32) * scale
    dq = qpos[:, None] - kwpos[None, :]
    m_win = (dq >= 0) & (dq < WINDOW) & (kwpos[None, :] >= 0)
    p = _softmax_masked(s, m_win, -1)
    o_win = jnp.einsum('bgrtk,bkgd->btgrd', p.astype(vw.dtype), vw)
    return gates[..., 0:1] * o_cmp + gates[..., 1:2] * o_sel + gates[..., 2:3] * o_win


def _mixer(xn, i, prm, mem):
    (c_in, v_ml, o_ml, i_ml, f_ml, q_n, k_c, v_c, k_s, v_s, k_w, v_w, g_n) = _split_in(xn @ prm['w_in'][i])
    bsz, t, _ = xn.shape
    H, Dh = ML_HEADS, ML_HEAD_DIM
    G, R, dk = NSA_KV_HEADS, NSA_REP, NSA_HEAD_DIM
    if mem is None:
        conv_buf = jnp.zeros((bsz, CONV_W - 1, D_ML), c_in.dtype)
        c0 = jnp.zeros((bsz, H, Dh, Dh), jnp.float32)
        n0 = jnp.zeros((bsz, H, Dh), jnp.float32)
        m0 = jnp.zeros((bsz, H), jnp.float32)
    else:
        conv_buf, c0, n0, m0 = mem['conv'].astype(c_in.dtype), mem['C'], mem['n'], mem['m']
    cpad = jnp.concatenate([conv_buf, c_in], axis=1)
    new_conv = cpad[:, t:]
    cw = prm['conv_w'][i]
    conv = sum((cpad[:, j:j + t] * cw[j] for j in range(CONV_W)), prm['conv_b'][i])
    ch = jax.nn.silu(conv).reshape(bsz, t, H, Dh)
    qm = jnp.einsum('bthd,hde->bthe', ch, prm['w_q_ml'][i])
    km = jnp.einsum('bthd,hde->bthe', ch, prm['w_k_ml'][i]) * (Dh ** -0.5)
    vm = v_ml.reshape(bsz, t, H, Dh)
    log_i = (i_ml + prm['b_i'][i]).astype(jnp.float32)
    log_f = jax.nn.log_sigmoid((f_ml + prm['b_f'][i]).astype(jnp.float32))
    hm, (cN, nN, mN) = _mlstm(qm, km, vm, log_i, log_f, c0, n0, m0)
    hm = _rmsnorm(hm.astype(xn.dtype), prm['g_ml'][i]) * jax.nn.sigmoid(o_ml).reshape(bsz, t, H, Dh)
    q = q_n.reshape(bsz, t, G, R, dk)
    gates = jax.nn.sigmoid(g_n).reshape(bsz, t, G, R, 3)
    k_c, v_c, k_s, v_s, k_w, v_w = [a.reshape(bsz, t, G, dk) for a in (k_c, v_c, k_s, v_s, k_w, v_w)]
    new_rows = jnp.stack([k_c, v_c, k_s, v_s], axis=2)
    win_new = jnp.stack([k_w, v_w], axis=2)
    if mem is None:
        kc_full, vc_full, ks_full, vs_full = k_c, v_c, k_s, v_s
        new_win = win_new[:, -min(WINDOW, t):]
    else:
        past = mem['kv'][mem['page_table']]
        past = past.reshape(bsz, -1, N_KV_SLOTS, G, dk).astype(new_rows.dtype)
        past_len = past.shape[1]
        full = jnp.concatenate([past, new_rows], axis=1)
        kc_full, vc_full, ks_full, vs_full = full[:, :, 0], full[:, :, 1], full[:, :, 2], full[:, :, 3]
        wb = mem['win'].shape[1]
        wfull = jnp.concatenate([mem['win'].astype(win_new.dtype), win_new], axis=1)
        new_win = wfull[:, -wb:]
    kcmp = _compress(kc_full, prm['cmp_pe'][i, 0], prm['cmp_w1'][i, 0], prm['cmp_b1'][i, 0], prm['cmp_w2'][i, 0], prm['cmp_b2'][i, 0])
    vcmp = _compress(vc_full, prm['cmp_pe'][i, 1], prm['cmp_w1'][i, 1], prm['cmp_b1'][i, 1], prm['cmp_w2'][i, 1], prm['cmp_b2'][i, 1])
    ksb = _sel_blocks(ks_full)
    vsb = _sel_blocks(vs_full)
    if mem is None:
        qb = math.gcd(t, NSA_QBLOCK)
        nq = t // qb
        kw_pad = jnp.pad(k_w, ((0, 0), (WINDOW, 0), (0, 0), (0, 0)))
        vw_pad = jnp.pad(v_w, ((0, 0), (WINDOW, 0), (0, 0), (0, 0)))

        def blk(args):
            qblk, gblk, t0 = args
            qpos = t0 + jnp.arange(qb, dtype=jnp.int32)
            kwb = lax.dynamic_slice_in_dim(kw_pad, t0, WINDOW + qb, axis=1)
            vwb = lax.dynamic_slice_in_dim(vw_pad, t0, WINDOW + qb, axis=1)
            kpos = t0 - WINDOW + jnp.arange(WINDOW + qb, dtype=jnp.int32)
            return _nsa_core(qblk, qpos, gblk, kcmp, vcmp, ksb, vsb, kwb, vwb, kpos)

        qbl = jnp.moveaxis(q.reshape(bsz, nq, qb, G, R, dk), 1, 0)
        gbl = jnp.moveaxis(gates.reshape(bsz, nq, qb, G, R, 3), 1, 0)
        t0s = jnp.arange(nq, dtype=jnp.int32) * qb
        on = lax.map(blk, (qbl, gbl, t0s))
        on = jnp.moveaxis(on, 0, 1).reshape(bsz, t, D_NSA)
    else:
        qpos = past_len + jnp.arange(t, dtype=jnp.int32)
        kpos = past_len - wb + jnp.arange(wb + t, dtype=jnp.int32)
        on = _nsa_core(q, qpos, gates, kcmp, vcmp, ksb, vsb, wfull[:, :, 0], wfull[:, :, 1], kpos).reshape(bsz, t, D_NSA)
    mixed = jnp.concatenate([hm.reshape(bsz, t, D_ML), on.astype(hm.dtype)], axis=-1)
    return mixed, (new_rows, new_win, cN, nN, mN, new_conv)


def _layer(h, pl, i, prm, mem):
    mixed, st = _mixer(_rmsnorm(h, prm['g_pre_mix'][i]), i, prm, mem)
    h = h + _rmsnorm(mixed @ prm['w_out'][i], prm['g_post_mix'][i])
    u = jnp.square(jax.nn.relu(_rmsnorm(h, prm['g_pre_mlp'][i]) @ prm['w_up'][i]))
    h = h + _rmsnorm(u @ prm['w_down'][i], prm['g_post_mlp'][i])
    gate = jax.nn.sigmoid(_rms(h) @ prm['w_pl_gate'][i])
    h = h + gate * _rmsnorm(pl @ prm['w_pl'][i], prm['g_pl'][i])
    return h, st


def setup_inputs(seed: int = 0) -> dict:
    key = jax.random.key(seed)
    ks = iter(jax.random.split(key, 48))

    def nrm(shape, scale):
        return scale * jax.random.normal(next(ks), shape, jnp.float32)

    def gain(shape):
        return 1.0 + nrm(shape, 0.01)

    n_pages = PAST_LEN // PAGE_SIZE
    n_used = DEC_BATCH * n_pages
    n_pool = n_used + max(1, n_used // 4)
    wb = min(WINDOW, PAST_LEN)
    G, dk, H, Dh = NSA_KV_HEADS, NSA_HEAD_DIM, ML_HEADS, ML_HEAD_DIM
    page_table = jax.random.permutation(next(ks), n_pool)[:n_used].reshape(DEC_BATCH, n_pages).astype(jnp.int32)
    return {
        "x_prompt": nrm((BATCH, SEQ, D_MODEL), 1.0),
        "x_sample": nrm((DEC_BATCH, DEC_SEQ, D_MODEL), 1.0),
        "cache_kv": nrm((DEPTH, n_pool, PAGE_SIZE, N_KV_SLOTS, G, dk), 1.0),
        "cache_win": nrm((DEPTH, DEC_BATCH, wb, 2, G, dk), 1.0),
        "state_C": nrm((DEPTH, DEC_BATCH, H, Dh, Dh), Dh ** -0.5),
        "state_n": nrm((DEPTH, DEC_BATCH, H, Dh), Dh ** -0.5),
        "state_m": nrm((DEPTH, DEC_BATCH, H), 0.5),
        "state_conv": nrm((DEPTH, DEC_BATCH, CONV_W - 1, D_ML), 1.0),
        "page_table": page_table,
        "p_prompt": nrm((DEPTH, BATCH, SEQ, D_PLE), 1.0),
        "p_sample": nrm((DEPTH, DEC_BATCH, DEC_SEQ, D_PLE), 1.0),
        "g_pre_mix": gain((DEPTH, D_MODEL)),
        "w_in": nrm((DEPTH, D_MODEL, IN_COLS), D_MODEL ** -0.5),
        "conv_w": nrm((DEPTH, CONV_W, D_ML), CONV_W ** -0.5),
        "conv_b": nrm((DEPTH, D_ML), 0.01),
        "w_q_ml": nrm((DEPTH, H, Dh, Dh), Dh ** -0.5),
        "w_k_ml": nrm((DEPTH, H, Dh, Dh), Dh ** -0.5),
        "b_i": nrm((DEPTH, H), 0.1),
        "b_f": jnp.linspace(3.0, 6.0, H, dtype=jnp.float32)[None, :] + nrm((DEPTH, H), 0.1),
        "g_ml": gain((DEPTH, H, Dh)),
        "cmp_pe": nrm((DEPTH, 2, CMP_BLOCK, dk), 0.1),
        "cmp_w1": nrm((DEPTH, 2, CMP_BLOCK * dk, CMP_HIDDEN), (CMP_BLOCK * dk) ** -0.5),
        "cmp_b1": nrm((DEPTH, 2, CMP_HIDDEN), 0.01),
        "cmp_w2": nrm((DEPTH, 2, CMP_HIDDEN, dk), CMP_HIDDEN ** -0.5),
        "cmp_b2": nrm((DEPTH, 2, dk), 0.01),
        "w_out": nrm((DEPTH, D_MIX, D_MODEL), D_MIX ** -0.5),
        "g_post_mix": gain((DEPTH, D_MODEL)),
        "g_pre_mlp": gain((DEPTH, D_MODEL)),
        "w_up": nrm((DEPTH, D_MODEL, D_FF), D_MODEL ** -0.5),
        "w_down": nrm((DEPTH, D_FF, D_MODEL), D_FF ** -0.5),
        "g_post_mlp": gain((DEPTH, D_MODEL)),
        "w_pl": nrm((DEPTH, D_PLE, D_MODEL), D_PLE ** -0.5),
        "g_pl": gain((DEPTH, D_MODEL)),
        "w_pl_gate": nrm((DEPTH, D_MODEL, D_MODEL), D_MODEL ** -0.5),
    }


def reference(x_prompt, x_sample, cache_kv, cache_win, state_C, state_n, state_m, state_conv, page_table,
              p_prompt, p_sample, g_pre_mix, w_in, conv_w, conv_b, w_q_ml, w_k_ml, b_i, b_f, g_ml,
              cmp_pe, cmp_w1, cmp_b1, cmp_w2, cmp_b2, w_out, g_post_mix, g_pre_mlp, w_up, w_down,
              g_post_mlp, w_pl, g_pl, w_pl_gate):
    prm = dict(g_pre_mix=g_pre_mix, w_in=w_in, conv_w=conv_w, conv_b=conv_b, w_q_ml=w_q_ml, w_k_ml=w_k_ml,
               b_i=b_i, b_f=b_f, g_ml=g_ml, cmp_pe=cmp_pe, cmp_w1=cmp_w1, cmp_b1=cmp_b1, cmp_w2=cmp_w2,
               cmp_b2=cmp_b2, w_out=w_out, g_post_mix=g_post_mix, g_pre_mlp=g_pre_mlp, w_up=w_up,
               w_down=w_down, g_post_mlp=g_post_mlp, w_pl=w_pl, g_pl=g_pl, w_pl_gate=w_pl_gate)
    hp, hs = x_prompt, x_sample
    sp, ss = [], []
    for i in range(DEPTH):
        mem = dict(kv=cache_kv[i], page_table=page_table, win=cache_win[i], C=state_C[i], n=state_n[i],
                   m=state_m[i], conv=state_conv[i])
        hp, st_p = _layer(hp, p_prompt[i], i, prm, None)
        hs, st_s = _layer(hs, p_sample[i], i, prm, mem)
        sp.append(st_p)
        ss.append(st_s)

    def stk(lst, j):
        return jnp.stack([s[j] for s in lst])

    return (hp, hs, stk(sp, 0), stk(ss, 0), stk(sp, 1), stk(ss, 1), stk(sp, 2), stk(ss, 2),
            stk(sp, 3), stk(ss, 3), stk(sp, 4), stk(ss, 4), stk(sp, 5), stk(ss, 5))
```

```python
import functools

import numpy as np
import jax
import jax.numpy as jnp
from jax import lax
from jax.experimental import pallas as pl
from jax.experimental.pallas import tpu as pltpu

ML_HEADS = 4
NSA_HEAD_DIM = 64
NSA_KV_HEADS = 4
CMP_BLOCK = 32
CMP_STRIDE = 16
SEL_BLOCK = 64
N_SEL = 16
WINDOW = 512
EPS = 1e-6
NEG = -1e30
FORCE = 1e6

LANES = 128
SUBLANES = 8
VMEM_LIMIT_BYTES = 56 * 1024 * 1024

ML_CHUNK = 128
GROUP_LANES = NSA_KV_HEADS * NSA_HEAD_DIM

F32 = jnp.float32
BF16 = jnp.bfloat16


def _dot(a, b):
    return jnp.dot(a, b, preferred_element_type=F32)


def _dot_nt(a, b):
    return lax.dot_general(a, b, (((1,), (1,)), ((), ())), preferred_element_type=F32)


def _params(semantics):
    return pltpu.CompilerParams(dimension_semantics=semantics, vmem_limit_bytes=VMEM_LIMIT_BYTES)


def _rms_rows(x):
    return x * lax.rsqrt(jnp.mean(x * x, axis=-1, keepdims=True) + EPS)


def _norm_matmul_kernel(x_ref, g_ref, w_ref, *rest, seg_tiles, has_small, act):
    if has_small:
        ws_ref, rest = rest[0], rest[1:]
    n_out = len(seg_tiles) + (1 if has_small else 0)
    out_refs, xn_ref = rest[:n_out], rest[n_out]
    j = pl.program_id(1)

    @pl.when(j == 0)
    def _():
        xn = (_rms_rows(x_ref[...]) * g_ref[...]).astype(BF16)
        xn_ref[...] = xn
        if has_small:
            out_refs[-1][...] = _dot(xn, ws_ref[...])

    y = _dot(xn_ref[...], w_ref[...])
    if act == "relu2":
        y = jnp.square(jnp.maximum(y, 0.0))
    start = 0
    for k, n in enumerate(seg_tiles):
        @pl.when((j >= start) & (j < start + n))
        def _(k=k):
            out_refs[k][...] = y.astype(out_refs[k].dtype)
        start += n


def _norm_matmul(x, g, w, seg_cols, *, w_small=None, tm, tn, act=None, out_dtype=F32):
    M, K = x.shape
    seg_tiles = tuple(c // tn for c in seg_cols)
    assert all(c % tn == 0 for c in seg_cols) and M % tm == 0
    has_small = w_small is not None
    starts = np.concatenate([[0], np.cumsum(seg_tiles)[:-1]]).tolist()

    def out_map(start, n):
        return lambda i, j: (i, jnp.clip(j - start, 0, n - 1))

    in_specs = [pl.BlockSpec((tm, K), lambda i, j: (i, 0)),
                pl.BlockSpec((1, K), lambda i, j: (0, 0)),
                pl.BlockSpec((K, tn), lambda i, j: (0, j))]
    args = [x, g.reshape(1, K), w]
    out_specs = [pl.BlockSpec((tm, tn), out_map(s, n)) for s, n in zip(starts, seg_tiles)]
    out_shape = [jax.ShapeDtypeStruct((M, c), out_dtype) for c in seg_cols]
    if has_small:
        in_specs.append(pl.BlockSpec((K, LANES), lambda i, j: (0, 0)))
        args.append(w_small)
        out_specs.append(pl.BlockSpec((tm, LANES), lambda i, j: (i, 0)))
        out_shape.append(jax.ShapeDtypeStruct((M, LANES), F32))
    return pl.pallas_call(
        functools.partial(_norm_matmul_kernel, seg_tiles=seg_tiles, has_small=has_small, act=act),
        grid=(M // tm, sum(seg_tiles)),
        in_specs=in_specs, out_specs=out_specs, out_shape=out_shape,
        scratch_shapes=[pltpu.VMEM((tm, K), BF16)],
        compiler_params=_params(("parallel", "arbitrary")),
        name="norm_matmul",
    )(*args)


def _matmul_norm_res_kernel(a_ref, w_ref, h_ref, g_ref, o_ref, acc_ref):
    k = pl.program_id(1)

    @pl.when(k == 0)
    def _():
        acc_ref[...] = jnp.zeros_like(acc_ref)

    acc_ref[...] += _dot(a_ref[...], w_ref[...])

    @pl.when(k == pl.num_programs(1) - 1)
    def _():
        o_ref[...] = h_ref[...] + _rms_rows(acc_ref[...]) * g_ref[...]


def _matmul_norm_res(a, w, h, g, *, tm, tk):
    M, K = a.shape
    N = w.shape[1]
    assert M % tm == 0 and K % tk == 0
    return pl.pallas_call(
        _matmul_norm_res_kernel,
        grid=(M // tm, K // tk),
        in_specs=[pl.BlockSpec((tm, tk), lambda i, k: (i, k)),
                  pl.BlockSpec((tk, N), lambda i, k: (k, 0)),
                  pl.BlockSpec((tm, N), lambda i, k: (i, 0)),
                  pl.BlockSpec((1, N), lambda i, k: (0, 0))],
        out_specs=pl.BlockSpec((tm, N), lambda i, k: (i, 0)),
        out_shape=jax.ShapeDtypeStruct((M, N), F32),
        scratch_shapes=[pltpu.VMEM((tm, N), F32)],
        compiler_params=_params(("parallel", "arbitrary")),
        name="matmul_norm_res",
    )(a, w, h, g.reshape(1, N))


def _ple_kernel(h_ref, p_ref, wg_ref, wp_ref, g_ref, o_ref):
    h = h_ref[...]
    gate = jax.nn.sigmoid(_dot(_rms_rows(h).astype(BF16), wg_ref[...]))
    e = _dot(p_ref[...].astype(BF16), wp_ref[...])
    o_ref[...] = h + gate * (_rms_rows(e) * g_ref[...])


def _ple(h, p, wg, wp, g, *, tm):
    M, N = h.shape
    P = p.shape[1]
    assert M % tm == 0
    return pl.pallas_call(
        _ple_kernel,
        grid=(M // tm,),
        in_specs=[pl.BlockSpec((tm, N), lambda i: (i, 0)),
                  pl.BlockSpec((tm, P), lambda i: (i, 0)),
                  pl.BlockSpec((N, N), lambda i: (0, 0)),
                  pl.BlockSpec((P, N), lambda i: (0, 0)),
                  pl.BlockSpec((1, N), lambda i: (0, 0))],
        out_specs=pl.BlockSpec((tm, N), lambda i: (i, 0)),
        out_shape=jax.ShapeDtypeStruct((M, N), F32),
        compiler_params=_params(("parallel",)),
        name="ple",
    )(h, p, wg, wp, g.reshape(1, N))


def _shift_rows(x, prev8, s):
    rows = x.shape[0]
    xs = pltpu.roll(x, s, axis=0)
    rid = lax.broadcasted_iota(jnp.int32, (SUBLANES, x.shape[1]), 0)
    head = jnp.where(rid < s, pltpu.roll(prev8, s, axis=0), xs[:SUBLANES])
    if rows == SUBLANES:
        return head
    return jnp.concatenate([head, xs[SUBLANES:]], axis=0)


def _mlstm_kernel(cin_ref, v_ref, o_ref, sm_ref, prev_ref, c0_ref, n0_ref, m0_ref,
                  cw_ref, cb_ref, wq_ref, wkt_ref, gb_ref, gml_ref, ltri_ref,
                  hm_ref, cout_ref, nout_ref, mout_ref,
                  carry_ref, cext_ref, m_ref, chpad_ref, vpad_ref, gpad_ref,
                  *, L, rows, valid, H, Dh):
    b = pl.program_id(0)
    c = pl.program_id(1)
    DE = Dh + LANES

    @pl.when(c == 0)
    def _():
        carry_ref[...] = prev_ref[0]
        m_ref[...] = m0_ref[0]
        for h in range(H):
            cext_ref[h, :, :Dh] = c0_ref[0, h]
            n_row = n0_ref[0, :, h * Dh:(h + 1) * Dh]
            cext_ref[h, :, Dh:] = jnp.broadcast_to(n_row, (LANES, Dh)).T

    if rows < L:
        @pl.when((b == 0) & (c == 0))
        def _():
            chpad_ref[...] = jnp.zeros_like(chpad_ref)
            vpad_ref[...] = jnp.zeros_like(vpad_ref)
            gpad_ref[...] = jnp.zeros_like(gpad_ref)

    x = cin_ref[...]
    prev8 = carry_ref[...]
    conv = cb_ref[...] + cw_ref[3:4, :] * x
    for s in (1, 2, 3):
        conv = conv + cw_ref[3 - s:4 - s, :] * _shift_rows(x, prev8, s)
    carry_ref[...] = x[rows - SUBLANES:, :]
    ch = conv * jax.nn.sigmoid(conv)

    if rows < L:
        chpad_ref[:rows, :] = ch
        vpad_ref[:rows, :] = v_ref[...]
        gpad_ref[:rows, :] = sm_ref[...]
        ch, v_all, sm = chpad_ref[...], vpad_ref[...], gpad_ref[...]
    else:
        v_all, sm = v_ref[...], sm_ref[...]
    ch = ch.astype(BF16)

    lane = lax.broadcasted_iota(jnp.int32, (L, LANES), 1)
    rid = lax.broadcasted_iota(jnp.int32, (L, LANES), 0)
    pre = sm + gb_ref[...]
    gates = jnp.where(lane < H, pre, jax.nn.log_sigmoid(pre))
    gates = jnp.where(rid < valid, gates, jnp.where(lane < H, NEG, 0.0))
    hi = gates.astype(BF16)
    r1 = gates - hi.astype(F32)
    mid = r1.astype(BF16)
    lo = (r1 - mid.astype(F32)).astype(BF16)
    ltri = ltri_ref[...]
    bc_all = _dot(ltri, hi) + _dot(ltri, mid) + _dot(ltri, lo)
    colform = jnp.where(lane < H, gates, bc_all)
    rowform = colform.T

    r_i = lax.broadcasted_iota(jnp.int32, (L, L), 0)
    c_i = lax.broadcasted_iota(jnp.int32, (L, L), 1)
    causal = c_i <= r_i
    lane1 = lax.broadcasted_iota(jnp.int32, (1, LANES), 1)
    m_row = m_ref[...]
    m_new_row = m_row
    ones = jnp.ones((L, LANES), F32)

    for h in range(H):
        hs = slice(h * Dh, (h + 1) * Dh)
        ch_h = ch[:, hs]
        q_h = _dot(ch_h, wq_ref[h]).astype(BF16)
        kt_h = _dot_nt(wkt_ref[h], ch_h).astype(BF16)
        v_h = v_all[:, hs]
        li_col = colform[:, h:h + 1]
        bc_col = colform[:, H + h:H + h + 1]
        li_row = rowform[h:h + 1, :]
        bc_row = rowform[H + h:H + h + 1, :]
        m_h = m_row[:, h:h + 1]

        logd = jnp.where(causal, bc_col - bc_row + li_row, NEG)
        inter = m_h + bc_col
        mt = jnp.maximum(jnp.max(logd, axis=-1, keepdims=True), inter)
        dm = jnp.exp(logd - mt)
        a_int = jnp.exp(inter - mt)
        s = _dot(q_h, kt_h) * dm
        v_ext = jnp.concatenate([v_h, ones], axis=1)
        cext = cext_ref[h]
        tot = _dot(s.astype(BF16), v_ext.astype(BF16)) + a_int * _dot(q_h, cext.astype(BF16))
        num = tot[:, :Dh]
        den = tot[:, Dh:Dh + 1]
        hh = num / jnp.maximum(jnp.abs(den), jnp.exp(-mt))

        m_last = mt[L - 1:L, :]
        bc_last = bc_col[L - 1:L, :]
        w_col = jnp.exp(bc_last - bc_col + li_col - m_last)
        dec = jnp.exp(m_h + bc_last - m_last)
        cext_ref[h] = dec * cext + _dot(kt_h, (w_col * v_ext).astype(BF16))
        m_new_row = jnp.where(lane1 == h, m_last, m_new_row)

        y = _rms_rows(hh) * gml_ref[:, hs]
        y = y[:rows] * jax.nn.sigmoid(o_ref[:, hs])
        hm_ref[:, hs] = y

    m_ref[...] = m_new_row

    @pl.when(c == pl.num_programs(1) - 1)
    def _():
        mout_ref[0] = m_new_row
        for h in range(H):
            cout_ref[0, h] = cext_ref[h, :, :Dh]
            nout_ref[0, :, h * Dh:(h + 1) * Dh] = cext_ref[h, :, Dh:].T[0:1, :]


def _mlstm(cin, v, o, small, prev, c0, n0, m0, cw, cb, wq, wkt, gb, gml, *, B, rows, valid):
    L = ML_CHUNK
    H = ML_HEADS
    D = cin.shape[1]
    Dh = D // H
    nc = cin.shape[0] // (B * rows)
    assert rows == L or (nc == 1 and rows == SUBLANES)
    ltri = jnp.asarray(np.tril(np.ones((L, L), np.float32)), BF16)
    tok = lambda b, c: (b * nc + c, 0)
    per_b3 = lambda b, c: (b, 0, 0)
    const2 = lambda b, c: (0, 0)
    const3 = lambda b, c: (0, 0, 0)
    hm, cout, nout, mout = pl.pallas_call(
        functools.partial(_mlstm_kernel, L=L, rows=rows, valid=valid, H=H, Dh=Dh),
        grid=(B, nc),
        in_specs=[pl.BlockSpec((rows, D), tok), pl.BlockSpec((rows, D), tok), pl.BlockSpec((rows, D), tok),
                  pl.BlockSpec((rows, LANES), tok),
                  pl.BlockSpec((1, SUBLANES, D), per_b3),
                  pl.BlockSpec((1, H, Dh, Dh), lambda b, c: (b, 0, 0, 0)),
                  pl.BlockSpec((1, 1, D), per_b3),
                  pl.BlockSpec((1, 1, LANES), per_b3),
                  pl.BlockSpec((4, D), const2), pl.BlockSpec((1, D), const2),
                  pl.BlockSpec((H, Dh, Dh), const3), pl.BlockSpec((H, Dh, Dh), const3),
                  pl.BlockSpec((1, LANES), const2), pl.BlockSpec((1, D), const2),
                  pl.BlockSpec((L, L), const2)],
        out_specs=[pl.BlockSpec((rows, D), tok),
                   pl.BlockSpec((1, H, Dh, Dh), lambda b, c: (b, 0, 0, 0)),
                   pl.BlockSpec((1, 1, D), per_b3),
                   pl.BlockSpec((1, 1, LANES), per_b3)],
        out_shape=[jax.ShapeDtypeStruct(cin.shape, F32),
                   jax.ShapeDtypeStruct((B, H, Dh, Dh), F32),
                   jax.ShapeDtypeStruct((B, 1, D), F32),
                   jax.ShapeDtypeStruct((B, 1, LANES), F32)],
        scratch_shapes=[pltpu.VMEM((SUBLANES, D), F32),
                        pltpu.VMEM((H, Dh, Dh + LANES), F32),
                        pltpu.VMEM((1, LANES), F32),
                        pltpu.VMEM((L, D), F32), pltpu.VMEM((L, D), F32), pltpu.VMEM((L, LANES), F32)],
        compiler_params=_params(("arbitrary", "arbitrary")),
        name="mlstm",
    )(cin, v, o, small, prev, c0, n0, m0, cw, cb, wq, wkt, gb, gml, ltri)
    return hm, cout, nout, mout


def _deinterleave(perm, x):
    n = x.shape[0] // LANES
    per = LANES // CMP_STRIDE
    ys = [_dot(perm, x[t * LANES:(t + 1) * LANES].astype(BF16)) for t in range(n)]
    return [jnp.concatenate([y[l * per:(l + 1) * per] for y in ys], axis=0) for l in range(CMP_STRIDE)]


def _pe_bias(slot, pe_ref, w1_ref, b1_ref):
    pe = jnp.broadcast_to(pe_ref[slot:slot + 1, :], (SUBLANES, pe_ref.shape[1])).astype(BF16)
    return _dot(pe, w1_ref[slot]) + b1_ref[slot:slot + 1, :]


def _compress_slot(xs, slot, bias, w1_ref, w2_ref, b2_ref):
    dk = NSA_HEAD_DIM
    half = CMP_BLOCK // 2
    nrow = xs[0].shape[0]
    out = None
    for g in range(NSA_KV_HEADS):
        gs = slice(g * dk, (g + 1) * dk)
        first = None
        second = None
        for l in range(half):
            xg = xs[l][:, gs].astype(BF16)
            a = _dot(xg, w1_ref[slot, l * dk:(l + 1) * dk, :])
            bb = _dot(xg, w1_ref[slot, (half + l) * dk:(half + l + 1) * dk, :])
            first = a if first is None else first + a
            second = bb if second is None else second + bb
        hid = jax.nn.gelu(first + pltpu.roll(second, nrow - 1, axis=0) + bias)
        og = _dot(hid.astype(BF16), w2_ref[slot, g])
        out = og if out is None else out + og
    return out + b2_ref[slot:slot + 1, :]


def _expand_q(q, tq):
    slab = NSA_KV_HEADS * tq
    lane_g = lax.broadcasted_iota(jnp.int32, (slab, GROUP_LANES), 1) // NSA_HEAD_DIM
    row_g = lax.broadcasted_iota(jnp.int32, (slab, GROUP_LANES), 0) // tq
    own = lane_g == row_g
    n_rep = q.shape[1] // GROUP_LANES
    slabs = []
    for r in range(n_rep):
        qr = q[:, r * GROUP_LANES:(r + 1) * GROUP_LANES] * (NSA_HEAD_DIM ** -0.5)
        slabs.append(jnp.where(own, jnp.concatenate([qr] * NSA_KV_HEADS, axis=0), 0.0).astype(BF16))
    return jnp.concatenate(slabs, axis=0), own


def _attend(carry, qbd, kc, vc, mask):
    m, l, acc = carry
    s = jnp.where(mask, _dot_nt(qbd, kc), NEG)
    m_new = jnp.maximum(m, jnp.max(s, axis=1, keepdims=True))
    alpha = jnp.exp(m - m_new)
    p = jnp.where(mask, jnp.exp(s - m_new), 0.0)
    l = alpha * l + jnp.sum(p, axis=1, keepdims=True)
    acc = alpha * acc + _dot(p.astype(BF16), vc)
    return m_new, l, acc


def _attend_init(rows):
    return (jnp.full((rows, 1), NEG, F32), jnp.zeros((rows, 1), F32), jnp.zeros((rows, GROUP_LANES), F32))


def _attend_out(carry):
    _, l, acc = carry
    return acc / jnp.maximum(l, 1e-30)


def _cmp_branch(qbd, qpos, kcmp, vcmp, mmap, n_cmp, n_rep):
    rows = qbd.shape[0]
    lane = lax.broadcasted_iota(jnp.int32, (rows, kcmp.shape[0]), 1)
    mask = (lane * CMP_STRIDE + (CMP_BLOCK - 1) <= qpos) & (lane < n_cmp)
    s = jnp.where(mask, _dot_nt(qbd, kcmp), NEG)
    e = jnp.where(mask, jnp.exp(s - jnp.max(s, axis=1, keepdims=True)), 0.0)
    p = (e / jnp.maximum(jnp.sum(e, axis=1, keepdims=True), 1e-30)).astype(BF16)
    o = _dot(p, vcmp)
    imp_rows = _dot(p, mmap)
    slab = rows // n_rep
    imp = imp_rows[:slab]
    for r in range(1, n_rep):
        imp = imp + imp_rows[r * slab:(r + 1) * slab]
    return o, imp


def _select_blocks(imp, qpos, n_sel_blocks):
    rows = imp.shape[0]
    blk = lax.broadcasted_iota(jnp.int32, (rows, LANES), 1)
    cur = qpos // SEL_BLOCK
    forced = (blk == 0) | (blk == cur) | (blk == cur - 1)
    score = jnp.where(blk <= cur, jnp.where(forced, FORCE, imp), -1.0)
    score = jnp.where(blk < n_sel_blocks, score, -2.0)
    rank = jnp.zeros((rows, LANES), F32)
    for j in range(n_sel_blocks):
        col = score[:, j:j + 1]
        ahead = (col > score) | ((col == score) & (blk > j))
        rank = rank + jnp.where(ahead, 1.0, 0.0)
    chosen = (rank < float(min(N_SEL, n_sel_blocks))) & (score >= 0.0)
    return jnp.where(chosen, 1.0, 0.0)


def _combine_out(o_cmp, o_sel, o_win, gates, own, out_ref, tq, n_rep, row_index):
    comb = gates[:, 0:1] * o_cmp + gates[:, 1:2] * o_sel + gates[:, 2:3] * o_win
    slab = NSA_KV_HEADS * tq
    for r in range(n_rep):
        kept = jnp.where(own, comb[r * slab:(r + 1) * slab], 0.0)
        piece = kept[:tq]
        for g in range(1, NSA_KV_HEADS):
            piece = piece + kept[g * tq:(g + 1) * tq]
        out_ref[row_index + (slice(None), slice(r * GROUP_LANES, (r + 1) * GROUP_LANES))] = piece.astype(out_ref.dtype)


def _compress_prompt_kernel(kc_ref, vc_ref, pe_ref, w1_ref, b1_ref, w2_ref, b2_ref, perm_ref, ko_ref, vo_ref):
    for slot, (src, dst) in enumerate(((kc_ref, ko_ref), (vc_ref, vo_ref))):
        bias = _pe_bias(slot, pe_ref, w1_ref, b1_ref)[0:1]
        xs = _deinterleave(perm_ref[...], src[...])
        dst[0] = _compress_slot(xs, slot, bias, w1_ref, w2_ref, b2_ref)


def _compress_prompt(kv4, cw, *, B, T):
    nrow = T // CMP_STRIDE
    assert (T - CMP_BLOCK) // CMP_STRIDE + 1 <= nrow and T % LANES == 0
    full = lambda a: pl.BlockSpec(a.shape, lambda b: (0,) * a.ndim)
    return pl.pallas_call(
        _compress_prompt_kernel,
        grid=(B,),
        in_specs=[pl.BlockSpec((T, GROUP_LANES), lambda b: (b, 0)),
                  pl.BlockSpec((T, GROUP_LANES), lambda b: (b, 1))] + [full(a) for a in cw],
        out_specs=[pl.BlockSpec((1, nrow, GROUP_LANES), lambda b: (b, 0, 0))] * 2,
        out_shape=[jax.ShapeDtypeStruct((B, nrow, GROUP_LANES), F32)] * 2,
        compiler_params=_params(("parallel",)),
        name="compress_prompt",
    )(kv4, kv4, *cw)


def _nsa_prompt_kernel(q_ref, gt_ref, kcmp_ref, vcmp_ref, ks_ref, vs_ref, kw_ref, vw_ref,
                       mmap_ref, exp_ref, out_ref,
                       ksb_ref, vsb_ref, kwb_ref, vwb_ref, *, tq, ck, n_cmp, n_sel_blocks):
    qt = pl.program_id(1)
    n_rep = q_ref.shape[1] // GROUP_LANES
    slab = NSA_KV_HEADS * tq
    rows = n_rep * slab

    @pl.when(qt == 0)
    def _():
        ksb_ref[...] = ks_ref[...].astype(BF16)
        vsb_ref[...] = vs_ref[...].astype(BF16)
        kwb_ref[...] = kw_ref[...].astype(BF16)
        vwb_ref[...] = vw_ref[...].astype(BF16)

    t0 = qt * tq
    rid = lax.broadcasted_iota(jnp.int32, (rows, 1), 0)
    qpos = t0 + rid % tq
    qbd, own = _expand_q(q_ref[...], tq)
    gates = jax.nn.sigmoid(gt_ref[...])

    o_cmp, imp = _cmp_branch(qbd, qpos, kcmp_ref[0].astype(BF16), vcmp_ref[0].astype(BF16),
                             mmap_ref[...], n_cmp, n_rep)
    sel = _select_blocks(imp, qpos[:slab], n_sel_blocks).astype(BF16)
    sel = jnp.concatenate([sel] * n_rep, axis=0)

    kpos0 = lax.broadcasted_iota(jnp.int32, (1, ck), 1)

    def sel_body(c, carry):
        off = pl.multiple_of(c * ck, ck)
        kpos = kpos0 + c * ck
        mask = (_dot(sel, exp_ref[c]) > 0.5) & (kpos <= qpos)
        return _attend(carry, qbd, ksb_ref[pl.ds(off, ck), :], vsb_ref[pl.ds(off, ck), :], mask)

    n_causal = (t0 + tq + ck - 1) // ck
    o_sel = _attend_out(lax.fori_loop(0, n_causal, sel_body, _attend_init(rows)))

    def win_body(c, carry):
        off = pl.multiple_of(c * ck, ck)
        kpos = kpos0 + c * ck
        mask = (kpos <= qpos) & (kpos > qpos - WINDOW)
        return _attend(carry, qbd, kwb_ref[pl.ds(off, ck), :], vwb_ref[pl.ds(off, ck), :], mask)

    c_lo = jnp.maximum(t0 - (WINDOW - 1), 0) // ck
    o_win = _attend_out(lax.fori_loop(c_lo, n_causal, win_body, _attend_init(rows)))

    _combine_out(o_cmp, o_sel, o_win, gates, own, out_ref, tq, n_rep, ())


def _cmp_to_sel(n_cmp, n_sel_blocks):
    r = SEL_BLOCK // CMP_STRIDE
    c = CMP_BLOCK // CMP_STRIDE
    m = np.zeros((LANES, LANES), np.float32)
    for j in range(n_sel_blocks):
        for a in range(r):
            for b in range(c):
                i = r * j + a - b
                if 0 <= i < n_cmp:
                    m[i, j] += 1.0
    return jnp.asarray(m, BF16)


def _block_expand(n_chunks, ck, n_sel_blocks):
    e = np.zeros((n_chunks, LANES, ck), np.float32)
    key = np.arange(n_chunks * ck).reshape(n_chunks, ck)
    for c in range(n_chunks):
        e[c, key[c] // SEL_BLOCK, np.arange(ck)] = 1.0
    e[:, n_sel_blocks:, :] = 0.0
    return jnp.asarray(e, BF16)


def _nsa_prompt(q, gates_rows, kcmp, vcmp, kv4, win2, *, B, T, tq, ck):
    M, QW = q.shape
    n_rep = QW // GROUP_LANES
    rows = n_rep * NSA_KV_HEADS * tq
    nqt = T // tq
    n_cmp = (T - CMP_BLOCK) // CMP_STRIDE + 1
    n_sel_blocks = -(-T // SEL_BLOCK)
    assert T % tq == 0 and T % ck == 0 and n_sel_blocks <= LANES and kcmp.shape[1] <= LANES
    mmap = _cmp_to_sel(n_cmp, n_sel_blocks)[:kcmp.shape[1]]
    expand = _block_expand(T // ck, ck, n_sel_blocks)
    const = lambda a: pl.BlockSpec(a.shape, lambda b, t: (0,) * a.ndim)
    kv_spec = lambda col: pl.BlockSpec((T, GROUP_LANES), lambda b, t: (b, col))
    return pl.pallas_call(
        functools.partial(_nsa_prompt_kernel, tq=tq, ck=ck, n_cmp=n_cmp, n_sel_blocks=n_sel_blocks),
        grid=(B, nqt),
        in_specs=[pl.BlockSpec((tq, QW), lambda b, t: (b * nqt + t, 0)),
                  pl.BlockSpec((rows, 3), lambda b, t: (b * nqt + t, 0)),
                  pl.BlockSpec((1,) + kcmp.shape[1:], lambda b, t: (b, 0, 0)),
                  pl.BlockSpec((1,) + vcmp.shape[1:], lambda b, t: (b, 0, 0)),
                  kv_spec(2), kv_spec(3), kv_spec(0), kv_spec(1),
                  const(mmap), const(expand)],
        out_specs=pl.BlockSpec((tq, QW), lambda b, t: (b * nqt + t, 0)),
        out_shape=jax.ShapeDtypeStruct((M, QW), BF16),
        scratch_shapes=[pltpu.VMEM((T, GROUP_LANES), BF16)] * 4,
        compiler_params=_params(("arbitrary", "arbitrary")),
        name="nsa_prompt",
    )(q, gates_rows, kcmp, vcmp, kv4, kv4, win2, win2, mmap, expand)


def _nsa_decode_kernel(pt_ref, *refs, n_pages, page, tq, t_new, past_len, n_cmp, n_sel_blocks, wb):
    del pt_ref
    pages = refs[:n_pages]
    (q_ref, gt_ref, kvn_ref, wn_ref, cw_ref, pe_ref, w1_ref, b1_ref, w2_ref, b2_ref, perm_ref,
     mmap_ref, exp_ref, out_ref, nw_ref, newk_ref, neww_ref, bias_ref) = refs[n_pages:]
    b = pl.program_id(0)
    n_rep = q_ref.shape[2] // GROUP_LANES
    slab = NSA_KV_HEADS * tq
    rows = n_rep * slab

    @pl.when(b == 0)
    def _():
        newk_ref[...] = jnp.zeros_like(newk_ref)
        neww_ref[...] = jnp.zeros_like(neww_ref)

    newk_ref[:tq, :] = kvn_ref[0]
    neww_ref[:tq, :] = wn_ref[0]

    rid = lax.broadcasted_iota(jnp.int32, (rows, 1), 0)
    qpos = past_len + rid % tq
    qbd, own = _expand_q(q_ref[0], tq)
    gates = jax.nn.sigmoid(gt_ref[...])

    @pl.when(b == 0)
    def _():
        for slot in range(2):
            bias_ref[slot] = _pe_bias(slot, pe_ref, w1_ref, b1_ref)

    cmp_kv = []
    for slot in range(2):
        lanes = slice(slot * GROUP_LANES, (slot + 1) * GROUP_LANES)
        xs = _deinterleave(perm_ref[...], jnp.concatenate([pg[0, :, lanes] for pg in pages], axis=0))
        cmp_kv.append(_compress_slot(xs, slot, bias_ref[slot, 0:1, :], w1_ref, w2_ref, b2_ref).astype(BF16))

    o_cmp, imp = _cmp_branch(qbd, qpos, cmp_kv[0], cmp_kv[1], mmap_ref[...], n_cmp, n_rep)
    sel = _select_blocks(imp, qpos[:slab], n_sel_blocks).astype(BF16)
    sel = jnp.concatenate([sel] * n_rep, axis=0)

    kpos0 = lax.broadcasted_iota(jnp.int32, (1, page), 1)
    ks_l = slice(2 * GROUP_LANES, 3 * GROUP_LANES)
    vs_l = slice(3 * GROUP_LANES, 4 * GROUP_LANES)
    carry = _attend_init(rows)
    for p in range(n_pages + 1):
        kpos = kpos0 + p * page
        mask = (_dot(sel, exp_ref[p]) > 0.5) & (kpos <= qpos)
        if p < n_pages:
            kc, vc = pages[p][0, :, ks_l].astype(BF16), pages[p][0, :, vs_l].astype(BF16)
        else:
            kc, vc = newk_ref[:, ks_l].astype(BF16), newk_ref[:, vs_l].astype(BF16)
        carry = _attend(carry, qbd, kc, vc, mask)
    o_sel = _attend_out(carry)

    carry = _attend_init(rows)
    for p in range(wb // page + 1):
        kpos = kpos0 + (past_len - wb + p * page)
        mask = (kpos <= qpos) & (kpos > qpos - WINDOW)
        if p < wb // page:
            kc = cw_ref[0, p * page:(p + 1) * page, :GROUP_LANES].astype(BF16)
            vc = cw_ref[0, p * page:(p + 1) * page, GROUP_LANES:].astype(BF16)
        else:
            kc, vc = neww_ref[:, :GROUP_LANES].astype(BF16), neww_ref[:, GROUP_LANES:].astype(BF16)
        carry = _attend(carry, qbd, kc, vc, mask)
    o_win = _attend_out(carry)

    _combine_out(o_cmp, o_sel, o_win, gates, own, out_ref, tq, n_rep, (0,))

    n_new = t_new
    w_old = cw_ref[0]
    shifted = pltpu.roll(w_old, wb - n_new, axis=0)
    nw_ref[0, :wb - SUBLANES, :] = shifted[:wb - SUBLANES]
    tail_id = lax.broadcasted_iota(jnp.int32, (SUBLANES, w_old.shape[1]), 0)
    nw_ref[0, wb - SUBLANES:, :] = jnp.where(tail_id < SUBLANES - n_new, shifted[wb - SUBLANES:],
                                            pltpu.roll(wn_ref[0], SUBLANES - n_new, axis=0))


def _nsa_decode(q8, gates_rows, kvnew8, winnew8, cache, cache_win, page_table, cw, *, past_len, t_new):
    B, tq, QW = q8.shape
    n_pages = page_table.shape[1]
    page = cache.shape[1]
    wb = cache_win.shape[1]
    n_rep = QW // GROUP_LANES
    rows = n_rep * NSA_KV_HEADS * tq
    tk = past_len + t_new
    n_cmp = (tk - CMP_BLOCK) // CMP_STRIDE + 1
    n_sel_blocks = -(-tk // SEL_BLOCK)
    nrow = past_len // CMP_STRIDE
    assert tq == SUBLANES and past_len == n_pages * page and wb % page == 0 and page == LANES
    assert n_cmp <= nrow <= LANES and (n_cmp - 1) * CMP_STRIDE + CMP_BLOCK <= past_len
    assert n_sel_blocks <= LANES and 0 < t_new < tq and wb == WINDOW
    mmap = _cmp_to_sel(n_cmp, n_sel_blocks)[:nrow]
    expand = _block_expand(n_pages + 1, page, n_sel_blocks)
    const = lambda a: pl.BlockSpec(a.shape, lambda b, pt: (0,) * a.ndim)
    page_spec = lambda p: pl.BlockSpec((1, page, cache.shape[2]), lambda b, pt: (pt[b, p], 0, 0))
    per_b = lambda a: pl.BlockSpec((1,) + a.shape[1:], lambda b, pt: (b, 0, 0))
    consts = list(cw) + [mmap, expand]
    grid_spec = pltpu.PrefetchScalarGridSpec(
        num_scalar_prefetch=1,
        grid=(B,),
        in_specs=[page_spec(p) for p in range(n_pages)]
        + [per_b(q8), pl.BlockSpec((rows, 3), lambda b, pt: (b, 0)), per_b(kvnew8), per_b(winnew8),
           per_b(cache_win)] + [const(a) for a in consts],
        out_specs=[per_b(q8), per_b(cache_win)],
        scratch_shapes=[pltpu.VMEM((page, kvnew8.shape[2]), F32), pltpu.VMEM((page, winnew8.shape[2]), F32),
                        pltpu.VMEM((2, SUBLANES, cw[2].shape[1]), F32)],
    )
    return pl.pallas_call(
        functools.partial(_nsa_decode_kernel, n_pages=n_pages, page=page, tq=tq, t_new=t_new, past_len=past_len,
                          n_cmp=n_cmp, n_sel_blocks=n_sel_blocks, wb=wb),
        grid_spec=grid_spec,
        out_shape=[jax.ShapeDtypeStruct(q8.shape, F32), jax.ShapeDtypeStruct(cache_win.shape, F32)],
        compiler_params=_params(("arbitrary",)),
        name="nsa_decode",
    )(page_table, *([cache] * n_pages), q8, gates_rows, kvnew8, winnew8, cache_win, *consts)


def _layer_weights(i, prm, d_ml, d_nsa):
    H, G, dk = ML_HEADS, NSA_KV_HEADS, NSA_HEAD_DIM
    R = d_nsa // (G * dk)
    kvw = G * dk
    w_in = prm["w_in"][i]
    o = 0
    c_in, o = w_in[:, o:o + d_ml], o + d_ml
    v_ml, o = w_in[:, o:o + d_ml], o + d_ml
    o_ml, o = w_in[:, o:o + d_ml], o + d_ml
    i_ml, o = w_in[:, o:o + H], o + H
    f_ml, o = w_in[:, o:o + H], o + H
    q_n, o = w_in[:, o:o + d_nsa], o + d_nsa
    kv, o = w_in[:, o:o + 6 * kvw], o + 6 * kvw
    g_n = w_in[:, o:]
    K = w_in.shape[0]
    q_perm = q_n.reshape(K, G, R, dk).transpose(0, 2, 1, 3).reshape(K, d_nsa)
    w_main = jnp.concatenate([c_in, v_ml, o_ml, q_perm, kv], axis=1).astype(BF16)
    small = jnp.concatenate([i_ml, f_ml, g_n], axis=1)
    w_small = jnp.pad(small, ((0, 0), (0, LANES - small.shape[1]))).astype(BF16)
    w_out = prm["w_out"][i]
    w_out_nsa = w_out[d_ml:].reshape(G, R, dk, -1).transpose(1, 0, 2, 3).reshape(d_nsa, -1)
    w_out_p = jnp.concatenate([w_out[:d_ml], w_out_nsa], axis=0).astype(BF16)
    Dh = d_ml // H
    gate_bias = jnp.pad(jnp.concatenate([prm["b_i"][i], prm["b_f"][i]]), (0, LANES - 2 * H)).reshape(1, LANES)
    w2 = prm["cmp_w2"][i]
    w2_placed = jnp.stack([jnp.stack([jnp.pad(w2[s], ((0, 0), (g * dk, (G - 1 - g) * dk))) for g in range(G)])
                           for s in range(2)]).astype(BF16)
    perm = np.zeros((LANES, LANES), np.float32)
    tok = np.arange(LANES)
    perm[(tok % CMP_STRIDE) * (LANES // CMP_STRIDE) + tok // CMP_STRIDE, tok] = 1.0
    cmp_w = (prm["cmp_pe"][i].reshape(2, -1), prm["cmp_w1"][i].astype(BF16), prm["cmp_b1"][i], w2_placed,
             jnp.tile(prm["cmp_b2"][i], (1, G)), jnp.asarray(perm, BF16))
    return dict(
        w_main=w_main, w_small=w_small, w_out=w_out_p,
        w_up=prm["w_up"][i].astype(BF16), w_down=prm["w_down"][i].astype(BF16),
        w_pl=prm["w_pl"][i].astype(BF16), w_pl_gate=prm["w_pl_gate"][i].astype(BF16),
        conv_w=prm["conv_w"][i], conv_b=prm["conv_b"][i].reshape(1, -1),
        wq=prm["w_q_ml"][i].astype(BF16),
        wkt=(jnp.swapaxes(prm["w_k_ml"][i], 1, 2) * (Dh ** -0.5)).astype(BF16),
        gate_bias=gate_bias, g_ml=prm["g_ml"][i].reshape(1, -1), cmp=cmp_w)


def _gate_rows(small, B, nqt, tq, G, R):
    g = small[:, 2 * ML_HEADS:2 * ML_HEADS + 3 * G * R].reshape(B, nqt, tq, G, R, 3)
    return g.transpose(0, 1, 4, 3, 2, 5).reshape(B * nqt * R * G * tq, 3)


def _pad_time(a, B, t, tp):
    return jnp.pad(a.reshape(B, t, -1), ((0, 0), (0, tp - t), (0, 0)))


def _layer(h, pl_e, i, prm, lw, mem, *, B, T):
    M, D = h.shape
    G, dk, H = NSA_KV_HEADS, NSA_HEAD_DIM, ML_HEADS
    d_ml = lw["conv_w"].shape[1]
    d_nsa = lw["w_out"].shape[0] - d_ml
    R = d_nsa // (G * dk)
    Dh = d_ml // H
    kvw = G * dk
    tm = min(M, 512)

    cin, v, o, q, kv4, win2, small = _norm_matmul(
        h, prm["g_pre_mix"][i], lw["w_main"], (d_ml, d_ml, d_ml, d_nsa, 4 * kvw, 2 * kvw),
        w_small=lw["w_small"], tm=tm, tn=512)

    ml_w = (lw["conv_w"], lw["conv_b"], lw["wq"], lw["wkt"], lw["gate_bias"], lw["g_ml"])
    if mem is None:
        zeros = lambda *s: jnp.zeros(s, F32)
        hm, c_new, n_new, m_new = _mlstm(
            cin, v, o, small, zeros(B, SUBLANES, d_ml), zeros(B, H, Dh, Dh), zeros(B, 1, d_ml), zeros(B, 1, LANES),
            *ml_w, B=B, rows=ML_CHUNK, valid=ML_CHUNK)
        kcmp, vcmp = _compress_prompt(kv4, lw["cmp"], B=B, T=T)
        tq = 64
        on = _nsa_prompt(q, _gate_rows(small, B, T // tq, tq, G, R), kcmp, vcmp, kv4, win2, B=B, T=T, tq=tq, ck=256)
        new_win = win2.reshape(B, T, 2, G, dk)[:, T - min(WINDOW, T):]
    else:
        tp = SUBLANES
        pad = lambda a: _pad_time(a, B, T, tp)
        prev = jnp.pad(mem["conv"], ((0, 0), (SUBLANES - mem["conv"].shape[1], 0), (0, 0)))
        m0 = jnp.pad(mem["m"], ((0, 0), (0, LANES - H))).reshape(B, 1, LANES)
        hm8, c_new, n_new, m_new = _mlstm(
            pad(cin).reshape(B * tp, d_ml), pad(v).reshape(B * tp, d_ml), pad(o).reshape(B * tp, d_ml),
            pad(small).reshape(B * tp, LANES), prev, mem["C"], mem["n"].reshape(B, 1, d_ml), m0,
            *ml_w, B=B, rows=tp, valid=T)
        hm = hm8.reshape(B, tp, d_ml)[:, :T].reshape(M, d_ml)
        n_pool, page = mem["kv"].shape[:2]
        on8, new_win = _nsa_decode(
            pad(q), _gate_rows(pad(small).reshape(B * tp, LANES), B, 1, tp, G, R), pad(kv4), pad(win2),
            mem["kv"].reshape(n_pool, page, -1), mem["win"].reshape(B, mem["win"].shape[1], -1),
            mem["page_table"], lw["cmp"], past_len=mem["page_table"].shape[1] * page, t_new=T)
        on = on8[:, :T].reshape(M, d_nsa)
        new_win = new_win.reshape(mem["win"].shape)

    mixed = jnp.concatenate([hm.astype(BF16), on.astype(BF16)], axis=1)
    h = _matmul_norm_res(mixed, lw["w_out"], h, prm["g_post_mix"][i], tm=tm, tk=1024)
    u, = _norm_matmul(h, prm["g_pre_mlp"][i], lw["w_up"], (lw["w_up"].shape[1],), tm=tm, tn=1024,
                      act="relu2", out_dtype=BF16)
    h = _matmul_norm_res(u, lw["w_down"], h, prm["g_post_mlp"][i], tm=tm, tk=1024)
    h = _ple(h, pl_e, lw["w_pl_gate"], lw["w_pl"], prm["g_pl"][i], tm=min(M, 256))

    state = (kv4.reshape(B, T, 4, G, dk), new_win, c_new, n_new.reshape(B, H, Dh),
             m_new[:, 0, :H], cin.reshape(B, T, d_ml)[:, T - 3:])
    return h, state


def kernel(x_prompt, x_sample, cache_kv, cache_win, state_C, state_n, state_m, state_conv, page_table,
           p_prompt, p_sample, g_pre_mix, w_in, conv_w, conv_b, w_q_ml, w_k_ml, b_i, b_f, g_ml,
           cmp_pe, cmp_w1, cmp_b1, cmp_w2, cmp_b2, w_out, g_post_mix, g_pre_mlp, w_up, w_down,
           g_post_mlp, w_pl, g_pl, w_pl_gate):
    prm = dict(g_pre_mix=g_pre_mix, w_in=w_in, conv_w=conv_w, conv_b=conv_b, w_q_ml=w_q_ml, w_k_ml=w_k_ml,
               b_i=b_i, b_f=b_f, g_ml=g_ml, cmp_pe=cmp_pe, cmp_w1=cmp_w1, cmp_b1=cmp_b1, cmp_w2=cmp_w2,
               cmp_b2=cmp_b2, w_out=w_out, g_post_mix=g_post_mix, g_pre_mlp=g_pre_mlp, w_up=w_up,
               w_down=w_down, g_post_mlp=g_post_mlp, w_pl=w_pl, g_pl=g_pl, w_pl_gate=w_pl_gate)
    Bp, Tp, D = x_prompt.shape
    Bs, Ts, _ = x_sample.shape
    depth = w_in.shape[0]
    d_ml = conv_w.shape[2]
    d_nsa = w_out.shape[1] - d_ml
    hp = x_prompt.reshape(Bp * Tp, D)
    hs = x_sample.reshape(Bs * Ts, D)
    sp, ss = [], []
    for i in range(depth):
        lw = _layer_weights(i, prm, d_ml, d_nsa)
        mem = dict(kv=cache_kv[i], page_table=page_table, win=cache_win[i], C=state_C[i], n=state_n[i],
                   m=state_m[i], conv=state_conv[i])
        hp, st_p = _layer(hp, p_prompt[i].reshape(Bp * Tp, -1), i, prm, lw, None, B=Bp, T=Tp)
        hs, st_s = _layer(hs, p_sample[i].reshape(Bs * Ts, -1), i, prm, lw, mem, B=Bs, T=Ts)
        sp.append(st_p)
        ss.append(st_s)

    stk = lambda lst, j: jnp.stack([s[j] for s in lst])
    return (hp.reshape(Bp, Tp, D), hs.reshape(Bs, Ts, D), stk(sp, 0), stk(ss, 0), stk(sp, 1), stk(ss, 1),
            stk(sp, 2), stk(ss, 2), stk(sp, 3), stk(ss, 3), stk(sp, 4), stk(ss, 4), stk(sp, 5), stk(ss, 5))
```

```python
import functools

import numpy as np
import jax
import jax.numpy as jnp
from jax import lax
from jax.experimental import pallas as pl
from jax.experimental.pallas import tpu as pltpu

ML_HEADS = 4
NSA_HEAD_DIM = 64
NSA_KV_HEADS = 4
CMP_BLOCK = 32
CMP_STRIDE = 16
SEL_BLOCK = 64
N_SEL = 16
WINDOW = 512
EPS = 1e-6
NEG = -1e30
FORCE = 1e6

LANES = 128
SUBLANES = 8
VMEM_LIMIT_BYTES = 56 * 1024 * 1024

ML_CHUNK = 128
GROUP_LANES = NSA_KV_HEADS * NSA_HEAD_DIM
QK_SCALE = NSA_HEAD_DIM ** -0.5 * 1.4426950408889634
MASK_BIG = 1e30

F32 = jnp.float32
BF16 = jnp.bfloat16


def _dot(a, b):
    return jnp.dot(a, b, preferred_element_type=F32)


def _dot_nt(a, b):
    return lax.dot_general(a, b, (((1,), (1,)), ((), ())), preferred_element_type=F32)


def _params(semantics):
    return pltpu.CompilerParams(dimension_semantics=semantics, vmem_limit_bytes=VMEM_LIMIT_BYTES)


def _rms_rows(x):
    return x * lax.rsqrt(jnp.mean(x * x, axis=-1, keepdims=True) + EPS)


def _norm_matmul_kernel(x_ref, g_ref, w_ref, *rest, seg_tiles, has_small, act):
    if has_small:
        ws_ref, rest = rest[0], rest[1:]
    n_out = len(seg_tiles) + (1 if has_small else 0)
    out_refs, xn_ref = rest[:n_out], rest[n_out]
    j = pl.program_id(1)

    @pl.when(j == 0)
    def _():
        xn = (_rms_rows(x_ref[...]) * g_ref[...]).astype(BF16)
        xn_ref[...] = xn
        if has_small:
            out_refs[-1][...] = _dot(xn, ws_ref[...])

    y = _dot(xn_ref[...], w_ref[...])
    if act == "relu2":
        y = jnp.square(jnp.maximum(y, 0.0))
    start = 0
    for k, n in enumerate(seg_tiles):
        @pl.when((j >= start) & (j < start + n))
        def _(k=k):
            out_refs[k][...] = y.astype(out_refs[k].dtype)
        start += n


def _norm_matmul(x, g, w, seg_cols, *, w_small=None, tm, tn, act=None, out_dtype=F32):
    M, K = x.shape
    seg_tiles = tuple(c // tn for c in seg_cols)
    assert all(c % tn == 0 for c in seg_cols) and M % tm == 0
    has_small = w_small is not None
    starts = np.concatenate([[0], np.cumsum(seg_tiles)[:-1]]).tolist()

    def out_map(start, n):
        return lambda i, j: (i, jnp.clip(j - start, 0, n - 1))

    in_specs = [pl.BlockSpec((tm, K), lambda i, j: (i, 0)),
                pl.BlockSpec((1, K), lambda i, j: (0, 0)),
                pl.BlockSpec((K, tn), lambda i, j: (0, j))]
    args = [x, g.reshape(1, K), w]
    out_specs = [pl.BlockSpec((tm, tn), out_map(s, n)) for s, n in zip(starts, seg_tiles)]
    out_shape = [jax.ShapeDtypeStruct((M, c), out_dtype) for c in seg_cols]
    if has_small:
        in_specs.append(pl.BlockSpec((K, LANES), lambda i, j: (0, 0)))
        args.append(w_small)
        out_specs.append(pl.BlockSpec((tm, LANES), lambda i, j: (i, 0)))
        out_shape.append(jax.ShapeDtypeStruct((M, LANES), F32))
    return pl.pallas_call(
        functools.partial(_norm_matmul_kernel, seg_tiles=seg_tiles, has_small=has_small, act=act),
        grid=(M // tm, sum(seg_tiles)),
        in_specs=in_specs, out_specs=out_specs, out_shape=out_shape,
        scratch_shapes=[pltpu.VMEM((tm, K), BF16)],
        compiler_params=_params(("parallel", "arbitrary")),
        name="norm_matmul",
    )(*args)


def _norm_matmul_t_kernel(x_ref, g_ref, wt_ref, o_ref, xn_ref):
    @pl.when(pl.program_id(1) == 0)
    def _():
        xn_ref[...] = (_rms_rows(x_ref[...]) * g_ref[...]).astype(BF16)

    o_ref[0] = _dot_nt(wt_ref[...], xn_ref[...])


def _norm_matmul_t(x, g, wt, *, B, T, tm, tn):
    M, K = x.shape
    N = wt.shape[0]
    assert T % tm == 0 and N % tn == 0 and M == B * T
    per = T // tm
    return pl.pallas_call(
        _norm_matmul_t_kernel,
        grid=(M // tm, N // tn),
        in_specs=[pl.BlockSpec((tm, K), lambda i, j: (i, 0)),
                  pl.BlockSpec((1, K), lambda i, j: (0, 0)),
                  pl.BlockSpec((tn, K), lambda i, j: (j, 0))],
        out_specs=pl.BlockSpec((1, tn, tm), lambda i, j: (i // per, j, i % per)),
        out_shape=jax.ShapeDtypeStruct((B, N, T), F32),
        scratch_shapes=[pltpu.VMEM((tm, K), BF16)],
        compiler_params=_params(("parallel", "arbitrary")),
        name="norm_matmul_t",
    )(x, g.reshape(1, K), wt)


def _matmul_norm_res_kernel(a_ref, w_ref, h_ref, g_ref, o_ref, acc_ref):
    k = pl.program_id(1)

    @pl.when(k == 0)
    def _():
        acc_ref[...] = jnp.zeros_like(acc_ref)

    acc_ref[...] += _dot(a_ref[...], w_ref[...])

    @pl.when(k == pl.num_programs(1) - 1)
    def _():
        o_ref[...] = h_ref[...] + _rms_rows(acc_ref[...]) * g_ref[...]


def _matmul_norm_res(a, w, h, g, *, tm, tk):
    M, K = a.shape
    N = w.shape[1]
    assert M % tm == 0 and K % tk == 0
    return pl.pallas_call(
        _matmul_norm_res_kernel,
        grid=(M // tm, K // tk),
        in_specs=[pl.BlockSpec((tm, tk), lambda i, k: (i, k)),
                  pl.BlockSpec((tk, N), lambda i, k: (k, 0)),
                  pl.BlockSpec((tm, N), lambda i, k: (i, 0)),
                  pl.BlockSpec((1, N), lambda i, k: (0, 0))],
        out_specs=pl.BlockSpec((tm, N), lambda i, k: (i, 0)),
        out_shape=jax.ShapeDtypeStruct((M, N), F32),
        scratch_shapes=[pltpu.VMEM((tm, N), F32)],
        compiler_params=_params(("parallel", "arbitrary")),
        name="matmul_norm_res",
    )(a, w, h, g.reshape(1, N))


def _ple_kernel(h_ref, p_ref, wg_ref, wp_ref, g_ref, o_ref):
    h = h_ref[...]
    gate = jax.nn.sigmoid(_dot(_rms_rows(h).astype(BF16), wg_ref[...]))
    e = _dot(p_ref[...].astype(BF16), wp_ref[...])
    o_ref[...] = h + gate * (_rms_rows(e) * g_ref[...])


def _ple(h, p, wg, wp, g, *, tm):
    M, N = h.shape
    P = p.shape[1]
    assert M % tm == 0
    return pl.pallas_call(
        _ple_kernel,
        grid=(M // tm,),
        in_specs=[pl.BlockSpec((tm, N), lambda i: (i, 0)),
                  pl.BlockSpec((tm, P), lambda i: (i, 0)),
                  pl.BlockSpec((N, N), lambda i: (0, 0)),
                  pl.BlockSpec((P, N), lambda i: (0, 0)),
                  pl.BlockSpec((1, N), lambda i: (0, 0))],
        out_specs=pl.BlockSpec((tm, N), lambda i: (i, 0)),
        out_shape=jax.ShapeDtypeStruct((M, N), F32),
        compiler_params=_params(("parallel",)),
        name="ple",
    )(h, p, wg, wp, g.reshape(1, N))


def _shift_rows(x, prev8, s):
    rows = x.shape[0]
    xs = pltpu.roll(x, s, axis=0)
    rid = lax.broadcasted_iota(jnp.int32, (SUBLANES, x.shape[1]), 0)
    head = jnp.where(rid < s, pltpu.roll(prev8, s, axis=0), xs[:SUBLANES])
    if rows == SUBLANES:
        return head
    return jnp.concatenate([head, xs[SUBLANES:]], axis=0)


def _mlstm_kernel(cin_ref, v_ref, o_ref, sm_ref, prev_ref, c0_ref, n0_ref, m0_ref,
                  cw_ref, cb_ref, wq_ref, wkt_ref, gb_ref, gml_ref, ltri_ref,
                  hm_ref, cout_ref, nout_ref, mout_ref,
                  carry_ref, cext_ref, m_ref, chpad_ref, vpad_ref, gpad_ref,
                  *, L, rows, valid, H, Dh):
    b = pl.program_id(0)
    c = pl.program_id(1)
    DE = Dh + LANES

    @pl.when(c == 0)
    def _():
        carry_ref[...] = prev_ref[0]
        m_ref[...] = m0_ref[0]
        for h in range(H):
            cext_ref[h, :, :Dh] = c0_ref[0, h]
            n_row = n0_ref[0, :, h * Dh:(h + 1) * Dh]
            cext_ref[h, :, Dh:] = jnp.broadcast_to(n_row, (LANES, Dh)).T

    if rows < L:
        @pl.when((b == 0) & (c == 0))
        def _():
            chpad_ref[...] = jnp.zeros_like(chpad_ref)
            vpad_ref[...] = jnp.zeros_like(vpad_ref)
            gpad_ref[...] = jnp.zeros_like(gpad_ref)

    x = cin_ref[...]
    prev8 = carry_ref[...]
    conv = cb_ref[...] + cw_ref[3:4, :] * x
    for s in (1, 2, 3):
        conv = conv + cw_ref[3 - s:4 - s, :] * _shift_rows(x, prev8, s)
    carry_ref[...] = x[rows - SUBLANES:, :]
    ch = conv * jax.nn.sigmoid(conv)

    if rows < L:
        chpad_ref[:rows, :] = ch
        vpad_ref[:rows, :] = v_ref[...]
        gpad_ref[:rows, :] = sm_ref[...]
        ch, v_all, sm = chpad_ref[...], vpad_ref[...], gpad_ref[...]
    else:
        v_all, sm = v_ref[...], sm_ref[...]
    ch = ch.astype(BF16)

    lane = lax.broadcasted_iota(jnp.int32, (L, LANES), 1)
    rid = lax.broadcasted_iota(jnp.int32, (L, LANES), 0)
    pre = sm + gb_ref[...]
    gates = jnp.where(lane < H, pre, jax.nn.log_sigmoid(pre))
    gates = jnp.where(rid < valid, gates, jnp.where(lane < H, NEG, 0.0))
    hi = gates.astype(BF16)
    r1 = gates - hi.astype(F32)
    mid = r1.astype(BF16)
    lo = (r1 - mid.astype(F32)).astype(BF16)
    ltri = ltri_ref[...]
    bc_all = _dot(ltri, hi) + _dot(ltri, mid) + _dot(ltri, lo)
    colform = jnp.where(lane < H, gates, bc_all)
    rowform = colform.T

    r_i = lax.broadcasted_iota(jnp.int32, (L, L), 0)
    c_i = lax.broadcasted_iota(jnp.int32, (L, L), 1)
    causal = c_i <= r_i
    lane1 = lax.broadcasted_iota(jnp.int32, (1, LANES), 1)
    m_row = m_ref[...]
    m_new_row = m_row
    ones = jnp.ones((L, LANES), F32)

    for h in range(H):
        hs = slice(h * Dh, (h + 1) * Dh)
        ch_h = ch[:, hs]
        q_h = _dot(ch_h, wq_ref[h]).astype(BF16)
        kt_h = _dot_nt(wkt_ref[h], ch_h).astype(BF16)
        v_h = v_all[:, hs]
        li_col = colform[:, h:h + 1]
        bc_col = colform[:, H + h:H + h + 1]
        li_row = rowform[h:h + 1, :]
        bc_row = rowform[H + h:H + h + 1, :]
        m_h = m_row[:, h:h + 1]

        logd = jnp.where(causal, bc_col - bc_row + li_row, NEG)
        inter = m_h + bc_col
        mt = jnp.maximum(jnp.max(logd, axis=-1, keepdims=True), inter)
        dm = jnp.exp(logd - mt)
        a_int = jnp.exp(inter - mt)
        s = _dot(q_h, kt_h) * dm
        v_ext = jnp.concatenate([v_h, ones], axis=1)
        cext = cext_ref[h]
        tot = _dot(s.astype(BF16), v_ext.astype(BF16)) + a_int * _dot(q_h, cext.astype(BF16))
        num = tot[:, :Dh]
        den = tot[:, Dh:Dh + 1]
        hh = num / jnp.maximum(jnp.abs(den), jnp.exp(-mt))

        m_last = mt[L - 1:L, :]
        bc_last = bc_col[L - 1:L, :]
        w_col = jnp.exp(bc_last - bc_col + li_col - m_last)
        dec = jnp.exp(m_h + bc_last - m_last)
        cext_ref[h] = dec * cext + _dot(kt_h, (w_col * v_ext).astype(BF16))
        m_new_row = jnp.where(lane1 == h, m_last, m_new_row)

        y = _rms_rows(hh) * gml_ref[:, hs]
        y = y[:rows] * jax.nn.sigmoid(o_ref[:, hs])
        hm_ref[:, hs] = y

    m_ref[...] = m_new_row

    @pl.when(c == pl.num_programs(1) - 1)
    def _():
        mout_ref[0] = m_new_row
        for h in range(H):
            cout_ref[0, h] = cext_ref[h, :, :Dh]
            nout_ref[0, :, h * Dh:(h + 1) * Dh] = cext_ref[h, :, Dh:].T[0:1, :]


def _mlstm(cin, v, o, small, prev, c0, n0, m0, cw, cb, wq, wkt, gb, gml, *, B, rows, valid, c_off=0):
    L = ML_CHUNK
    H = ML_HEADS
    D = cin.shape[1]
    Dh = D // H
    nc = cin.shape[0] // (B * rows)
    assert rows == L or (nc == 1 and rows == SUBLANES)
    ltri = jnp.asarray(np.tril(np.ones((L, L), np.float32)), BF16)
    tok = lambda b, c: (b * nc + c, 0)
    per_b3 = lambda b, c: (b, 0, 0)
    const2 = lambda b, c: (0, 0)
    const3 = lambda b, c: (0, 0, 0)
    hm, cout, nout, mout = pl.pallas_call(
        functools.partial(_mlstm_kernel, L=L, rows=rows, valid=valid, H=H, Dh=Dh),
        grid=(B, nc),
        in_specs=[pl.BlockSpec((rows, D), tok), pl.BlockSpec((rows, D), tok), pl.BlockSpec((rows, D), tok),
                  pl.BlockSpec((rows, LANES), tok),
                  pl.BlockSpec((1, SUBLANES, D), per_b3),
                  pl.BlockSpec((1, H, Dh, Dh), lambda b, c: (c_off + b, 0, 0, 0)),
                  pl.BlockSpec((1, 1, D), per_b3),
                  pl.BlockSpec((1, 1, LANES), per_b3),
                  pl.BlockSpec((4, D), const2), pl.BlockSpec((1, D), const2),
                  pl.BlockSpec((H, Dh, Dh), const3), pl.BlockSpec((H, Dh, Dh), const3),
                  pl.BlockSpec((1, LANES), const2), pl.BlockSpec((1, D), const2),
                  pl.BlockSpec((L, L), const2)],
        out_specs=[pl.BlockSpec((rows, D), tok),
                   pl.BlockSpec((1, H, Dh, Dh), lambda b, c: (b, 0, 0, 0)),
                   pl.BlockSpec((1, 1, D), per_b3),
                   pl.BlockSpec((1, 1, LANES), per_b3)],
        out_shape=[jax.ShapeDtypeStruct(cin.shape, F32),
                   jax.ShapeDtypeStruct((B, H, Dh, Dh), F32),
                   jax.ShapeDtypeStruct((B, 1, D), F32),
                   jax.ShapeDtypeStruct((B, 1, LANES), F32)],
        scratch_shapes=[pltpu.VMEM((SUBLANES, D), F32),
                        pltpu.VMEM((H, Dh, Dh + LANES), F32),
                        pltpu.VMEM((1, LANES), F32),
                        pltpu.VMEM((L, D), F32), pltpu.VMEM((L, D), F32), pltpu.VMEM((L, LANES), F32)],
        compiler_params=_params(("arbitrary", "arbitrary")),
        name="mlstm",
    )(cin, v, o, small, prev, c0, n0, m0, cw, cb, wq, wkt, gb, gml, ltri)
    return hm, cout, nout, mout


def _deinterleave_t(perm, groups):
    per = LANES // CMP_STRIDE
    ys = [_dot_nt(perm, x_t.astype(BF16)) for x_t in groups]
    return [jnp.concatenate([y[l * per:(l + 1) * per] for y in ys], axis=0) for l in range(CMP_STRIDE)]


def _pe_bias(slot, pe_ref, w1_ref, b1_ref):
    pe = jnp.broadcast_to(pe_ref[slot:slot + 1, :], (SUBLANES, pe_ref.shape[1])).astype(BF16)
    return _dot(pe, w1_ref[slot]) + b1_ref[slot:slot + 1, :]


def _compress_slot(xs, slot, bias, w1_ref, w2_ref, b2_ref):
    dk = NSA_HEAD_DIM
    half = CMP_BLOCK // 2
    nrow = xs[0].shape[0]
    pack = GROUP_LANES // dk
    out = None
    for g in range(NSA_KV_HEADS):
        gs = slice(g * dk, (g + 1) * dk)
        first = None
        second = None
        for l0 in range(0, half, pack):
            xg = jnp.concatenate([xs[l0 + j][:, gs] for j in range(pack)], axis=1).astype(BF16)
            a = _dot(xg, w1_ref[slot, l0 * dk:(l0 + pack) * dk, :])
            bb = _dot(xg, w1_ref[slot, (half + l0) * dk:(half + l0 + pack) * dk, :])
            first = a if first is None else first + a
            second = bb if second is None else second + bb
        hid = jax.nn.gelu(first + pltpu.roll(second, nrow - 1, axis=0) + bias)
        og = _dot(hid.astype(BF16), w2_ref[slot, g])
        out = og if out is None else out + og
    return out + b2_ref[slot:slot + 1, :]


def _expand_q(q, tq):
    slab = NSA_KV_HEADS * tq
    lane_g = lax.broadcasted_iota(jnp.int32, (slab, GROUP_LANES), 1) // NSA_HEAD_DIM
    row_g = lax.broadcasted_iota(jnp.int32, (slab, GROUP_LANES), 0) // tq
    own = lane_g == row_g
    n_rep = q.shape[1] // GROUP_LANES
    slabs = []
    for r in range(n_rep):
        qr = q[:, r * GROUP_LANES:(r + 1) * GROUP_LANES] * QK_SCALE
        slabs.append(jnp.where(own, jnp.concatenate([qr] * NSA_KV_HEADS, axis=0), 0.0).astype(BF16))
    return slabs, own


def _attend(carry, q, k, v, bias, *, transposed):
    m, l, acc = carry
    s = _dot(q, k) if transposed else _dot_nt(q, k)
    if bias is not None:
        rows, ck = s.shape
        s = (s.reshape(rows // bias.shape[0], bias.shape[0], ck) + bias[None]).reshape(rows, ck)
    m_new = jnp.maximum(m, jnp.max(s, axis=1, keepdims=True))
    alpha = jnp.exp2(m - m_new)
    p = jnp.exp2(s - m_new)
    l = alpha * l + jnp.sum(p, axis=1, keepdims=True)
    pv = _dot_nt(p.astype(BF16), v) if transposed else _dot(p.astype(BF16), v)
    return m_new, l, alpha * acc + pv


def _attend_init(rows):
    return (jnp.full((rows, 1), NEG, F32), jnp.zeros((rows, 1), F32), jnp.zeros((rows, GROUP_LANES), F32))


def _cmp_rows(q, mask, kcmp, vcmp):
    rows, slab = q.shape[0], mask.shape[0]
    s = _dot_nt(q, kcmp).reshape(rows // slab, slab, mask.shape[1])
    s = jnp.where(mask[None], s, NEG)
    e = jnp.where(mask[None], jnp.exp2(s - jnp.max(s, axis=2, keepdims=True)), 0.0)
    p = e * (1.0 / jnp.maximum(jnp.sum(e, axis=2, keepdims=True), 1e-30))
    p = p.reshape(rows, mask.shape[1]).astype(BF16)
    return _dot(p, vcmp), p


def _cmp_mask(rows, n_rows_cmp, qpos_col, n_cmp):
    lane = lax.broadcasted_iota(jnp.int32, (rows, n_rows_cmp), 1)
    return (lane * CMP_STRIDE + (CMP_BLOCK - 1) <= qpos_col) & (lane < n_cmp)


def _block_scores(imp, blk, cur, n_sel_blocks):
    forced = (blk == 0) | (blk == cur) | (blk == cur - 1)
    score = jnp.where(blk <= cur, jnp.where(forced, FORCE, imp), -1.0)
    return jnp.where(blk < n_sel_blocks, score, -2.0)


def _select_blocks(imp, qpos_col, n_sel_blocks):
    rows = imp.shape[0]
    blk = lax.broadcasted_iota(jnp.int32, (rows, LANES), 1)
    score = _block_scores(imp, blk, qpos_col // SEL_BLOCK, n_sel_blocks)
    rank = jnp.zeros((rows, LANES), F32)
    for j in range(n_sel_blocks):
        col = score[:, j:j + 1]
        ge = jnp.where(col >= score, 1.0, 0.0)
        gt = jnp.where(col > score, 1.0, 0.0)
        rank = rank + jnp.where(blk > j, ge, gt)
    chosen = jnp.where(rank < float(min(N_SEL, n_sel_blocks)), score, -1.0) >= 0.0
    return jnp.where(chosen, 0.0, -MASK_BIG).astype(BF16)


def _select_blocks_t(imp_t, qpos_row, n_sel_blocks):
    nb = -(-n_sel_blocks // SUBLANES) * SUBLANES
    cols = imp_t.shape[1]
    blk = lax.broadcasted_iota(jnp.int32, (nb, cols), 0)
    score = _block_scores(imp_t[:nb], blk, qpos_row // SEL_BLOCK, n_sel_blocks)
    rank = jnp.zeros((nb, cols), F32)
    for j in range(n_sel_blocks):
        row = score[j:j + 1, :]
        ge = jnp.where(row >= score, 1.0, 0.0)
        gt = jnp.where(row > score, 1.0, 0.0)
        rank = rank + jnp.where(blk > j, ge, gt)
    chosen = jnp.where(rank < float(min(N_SEL, n_sel_blocks)), score, -1.0) >= 0.0
    neg_t = jnp.where(chosen, 0.0, -MASK_BIG)
    neg_t = jnp.concatenate([neg_t, jnp.zeros((LANES - nb, cols), F32)], axis=0)
    return neg_t.T.astype(BF16)


def _combine_slab(o_cmp, sel, win, gates, own, tq):
    comb = gates[:, 0:1] * o_cmp
    for col, (_, l, acc) in ((1, sel), (2, win)):
        comb = comb + (gates[:, col:col + 1] * (1.0 / jnp.maximum(l, 1e-30))) * acc
    kept = jnp.where(own, comb, 0.0)
    piece = kept[:tq]
    for g in range(1, NSA_KV_HEADS):
        piece = piece + kept[g * tq:(g + 1) * tq]
    return piece


def _compress_prompt_kernel(kc_ref, vc_ref, pe_ref, w1_ref, b1_ref, w2_ref, b2_ref, perm_ref, ko_ref, vo_ref):
    T = kc_ref.shape[2]
    for slot, (src, dst) in enumerate(((kc_ref, ko_ref), (vc_ref, vo_ref))):
        bias = _pe_bias(slot, pe_ref, w1_ref, b1_ref)[0:1]
        xs = _deinterleave_t(perm_ref[...], [src[0, :, t * LANES:(t + 1) * LANES] for t in range(T // LANES)])
        dst[0] = _compress_slot(xs, slot, bias, w1_ref, w2_ref, b2_ref)


def _compress_prompt(kvw_t, cw, *, B, T):
    nrow = T // CMP_STRIDE
    assert (T - CMP_BLOCK) // CMP_STRIDE + 1 <= nrow and T % LANES == 0
    full = lambda a: pl.BlockSpec(a.shape, lambda b: (0,) * a.ndim)
    return pl.pallas_call(
        _compress_prompt_kernel,
        grid=(B,),
        in_specs=[pl.BlockSpec((1, GROUP_LANES, T), lambda b: (b, 0, 0)),
                  pl.BlockSpec((1, GROUP_LANES, T), lambda b: (b, 1, 0))] + [full(a) for a in cw],
        out_specs=[pl.BlockSpec((1, nrow, GROUP_LANES), lambda b: (b, 0, 0))] * 2,
        out_shape=[jax.ShapeDtypeStruct((B, nrow, GROUP_LANES), F32)] * 2,
        compiler_params=_params(("parallel",)),
        name="compress_prompt",
    )(kvw_t, kvw_t, *cw)


def _nsa_prompt_kernel(q_ref, gt_ref, kcmp_ref, vcmp_ref, ks_ref, vs_ref, kw_ref, vw_ref,
                       mmap_t_ref, exp_ref, out_ref,
                       kx_ref, vsb_ref, kwb_ref, vwb_ref, *, tq, ck, n_cmp, n_sel_blocks):
    qt = pl.program_id(1)
    n_rep = q_ref.shape[1] // GROUP_LANES
    slab = NSA_KV_HEADS * tq
    n_chunks = kx_ref.shape[0]

    @pl.when(qt == 0)
    def _():
        for c in range(n_chunks):
            cs = slice(c * ck, (c + 1) * ck)
            kx_ref[c, :GROUP_LANES, :] = ks_ref[0, :, cs].astype(BF16)
            kx_ref[c, GROUP_LANES:, :] = exp_ref[c]
            vsb_ref[c] = vs_ref[0, :, cs].astype(BF16)
            kwb_ref[c] = kw_ref[0, :, cs].astype(BF16)
            vwb_ref[c] = vw_ref[0, :, cs].astype(BF16)

    t0 = qt * tq
    qpos = t0 + lax.broadcasted_iota(jnp.int32, (slab, 1), 0) % tq
    qpos_row = t0 + lax.broadcasted_iota(jnp.int32, (1, slab), 1) % tq
    qs, own = _expand_q(q_ref[...], tq)
    gates = jax.nn.sigmoid(gt_ref[...])
    reps = range(n_rep)

    q_all = jnp.concatenate(qs, axis=0)
    kcmp, vcmp = kcmp_ref[0].astype(BF16), vcmp_ref[0].astype(BF16)
    o_cmp, p_cmp = _cmp_rows(q_all, _cmp_mask(slab, kcmp.shape[0], qpos, n_cmp), kcmp, vcmp)
    imp_rows_t = _dot_nt(mmap_t_ref[...], p_cmp)
    imp_t = imp_rows_t[:, :slab]
    for r in range(1, n_rep):
        imp_t = imp_t + imp_rows_t[:, r * slab:(r + 1) * slab]
    sel_neg = _select_blocks_t(imp_t, qpos_row, n_sel_blocks)
    qx = jnp.concatenate([q_all, jnp.concatenate([sel_neg] * n_rep, axis=0)], axis=1)

    kpos0 = lax.broadcasted_iota(jnp.int32, (1, ck), 1)
    init = _attend_init(n_rep * slab)
    c_diag = t0 // ck

    def sel_body(c, carry):
        return _attend(carry, qx, kx_ref[c], vsb_ref[c], None, transposed=True)

    sel = lax.fori_loop(0, c_diag, sel_body, init)
    causal = jnp.where(kpos0 + c_diag * ck <= qpos, 0.0, NEG)
    sel = _attend(sel, qx, kx_ref[c_diag], vsb_ref[c_diag], causal, transposed=True)

    def win_body(c, carry):
        kpos = kpos0 + c * ck
        bias = jnp.where(kpos <= qpos, jnp.where(kpos > qpos - WINDOW, 0.0, NEG), NEG)
        return _attend(carry, q_all, kwb_ref[c], vwb_ref[c], bias, transposed=True)

    c_lo = jnp.maximum(t0 - (WINDOW - 1), 0) // ck
    win = lax.fori_loop(c_lo, c_diag + 1, win_body, init)

    for r in reps:
        rs = slice(r * slab, (r + 1) * slab)
        piece = _combine_slab(o_cmp[rs], [a[rs] for a in sel], [a[rs] for a in win], gates[rs], own, tq)
        out_ref[:, r * GROUP_LANES:(r + 1) * GROUP_LANES] = piece.astype(out_ref.dtype)


def _cmp_to_sel(n_cmp, n_sel_blocks):
    r = SEL_BLOCK // CMP_STRIDE
    c = CMP_BLOCK // CMP_STRIDE
    m = np.zeros((LANES, LANES), np.float32)
    for j in range(n_sel_blocks):
        for a in range(r):
            for b in range(c):
                i = r * j + a - b
                if 0 <= i < n_cmp:
                    m[i, j] += 1.0
    return jnp.asarray(m, BF16)


def _block_expand(n_chunks, ck, n_sel_blocks):
    e = np.zeros((n_chunks, LANES, ck), np.float32)
    key = np.arange(n_chunks * ck).reshape(n_chunks, ck)
    for c in range(n_chunks):
        e[c, key[c] // SEL_BLOCK, np.arange(ck)] = 1.0
    e[:, n_sel_blocks:, :] = 0.0
    return jnp.asarray(e, BF16)


def _nsa_prompt(q, gates_rows, kcmp, vcmp, kvw_t, *, B, T, tq, ck):
    M, QW = q.shape
    n_rep = QW // GROUP_LANES
    rows = n_rep * NSA_KV_HEADS * tq
    nqt = T // tq
    n_cmp = (T - CMP_BLOCK) // CMP_STRIDE + 1
    n_sel_blocks = -(-T // SEL_BLOCK)
    assert T % ck == 0 and ck % tq == 0 and n_sel_blocks <= LANES and kcmp.shape[1] <= LANES
    mmap_t = _cmp_to_sel(n_cmp, n_sel_blocks).T[:, :kcmp.shape[1]]
    expand = _block_expand(T // ck, ck, n_sel_blocks)
    const = lambda a: pl.BlockSpec(a.shape, lambda b, t: (0,) * a.ndim)
    kv_spec = lambda slot: pl.BlockSpec((1, GROUP_LANES, T), lambda b, t: (b, slot, 0))
    chunks = lambda kdim: pltpu.VMEM((T // ck, kdim, ck), BF16)
    return pl.pallas_call(
        functools.partial(_nsa_prompt_kernel, tq=tq, ck=ck, n_cmp=n_cmp, n_sel_blocks=n_sel_blocks),
        grid=(B, nqt),
        in_specs=[pl.BlockSpec((tq, QW), lambda b, t: (b * nqt + t, 0)),
                  pl.BlockSpec((rows, 3), lambda b, t: (b * nqt + t, 0)),
                  pl.BlockSpec((1,) + kcmp.shape[1:], lambda b, t: (b, 0, 0)),
                  pl.BlockSpec((1,) + vcmp.shape[1:], lambda b, t: (b, 0, 0)),
                  kv_spec(2), kv_spec(3), kv_spec(4), kv_spec(5),
                  const(mmap_t), const(expand)],
        out_specs=pl.BlockSpec((tq, QW), lambda b, t: (b * nqt + t, 0)),
        out_shape=jax.ShapeDtypeStruct((M, QW), BF16),
        scratch_shapes=[chunks(GROUP_LANES + LANES), chunks(GROUP_LANES), chunks(GROUP_LANES), chunks(GROUP_LANES)],
        compiler_params=_params(("arbitrary", "arbitrary")),
        name="nsa_prompt",
    )(q, gates_rows, kcmp, vcmp, kvw_t, kvw_t, kvw_t, kvw_t, mmap_t, expand)


def _nsa_decode_kernel(pt_ref, *refs, n_pages, page, tq, t_new, past_len, n_cmp, n_sel_blocks, wb):
    del pt_ref
    pages = refs[:n_pages]
    (q_ref, gt_ref, kvn_ref, wn_ref, cw_ref, pe_ref, w1_ref, b1_ref, w2_ref, b2_ref, perm_ref,
     mmap_ref, exp_ref, out_ref, nw_ref, newk_ref, neww_ref, bias_ref) = refs[n_pages:]
    b = pl.program_id(0)
    n_rep = q_ref.shape[2] // GROUP_LANES
    slab = NSA_KV_HEADS * tq
    reps = range(n_rep)

    @pl.when(b == 0)
    def _():
        newk_ref[...] = jnp.zeros_like(newk_ref)
        neww_ref[...] = jnp.zeros_like(neww_ref)
        for slot in range(2):
            bias_ref[slot] = _pe_bias(slot, pe_ref, w1_ref, b1_ref)

    newk_ref[:tq, :] = kvn_ref[0]
    neww_ref[:tq, :] = wn_ref[0]

    qpos = past_len + lax.broadcasted_iota(jnp.int32, (slab, 1), 0) % tq
    qs, own = _expand_q(q_ref[0], tq)
    gates = jax.nn.sigmoid(gt_ref[...])

    cmp_kv = []
    for slot in range(2):
        xs = _deinterleave_t(perm_ref[...], [pg[0, slot] for pg in pages])
        cmp_kv.append(_compress_slot(xs, slot, bias_ref[slot, 0:1, :], w1_ref, w2_ref, b2_ref).astype(BF16))

    q_all = jnp.concatenate(qs, axis=0)
    o_cmp, p_cmp = _cmp_rows(q_all, _cmp_mask(slab, cmp_kv[0].shape[0], qpos, n_cmp), cmp_kv[0], cmp_kv[1])
    imp_rows = _dot(p_cmp, mmap_ref[...])
    imp = imp_rows[:slab]
    for r in range(1, n_rep):
        imp = imp + imp_rows[r * slab:(r + 1) * slab]
    neg_all = jnp.concatenate([_select_blocks(imp, qpos, n_sel_blocks)] * n_rep, axis=0)

    kpos0 = lax.broadcasted_iota(jnp.int32, (1, page), 1)
    own_rows = jnp.where(kpos0 + past_len <= jnp.concatenate([qpos] * n_rep, axis=0), 0.0, NEG)

    def softmax_values(scores, values):
        m = scores[0].max(axis=1, keepdims=True)
        for s in scores[1:]:
            m = jnp.maximum(m, s.max(axis=1, keepdims=True))
        l, acc = None, None
        for s, (v, v_t) in zip(scores, values):
            p = jnp.exp2(s - m)
            ls = p.sum(axis=1, keepdims=True)
            pv = _dot_nt(p.astype(BF16), v) if v_t else _dot(p.astype(BF16), v)
            l, acc = (ls, pv) if l is None else (l + ls, acc + pv)
        return None, l, acc

    scores = [_dot(q_all, pages[p][0, 2].astype(BF16)) + _dot(neg_all, exp_ref[p]) for p in range(n_pages)]
    values = [(pages[p][0, 3].astype(BF16), True) for p in range(n_pages)]
    k_new = newk_ref[:, 2 * GROUP_LANES:3 * GROUP_LANES].astype(BF16)
    scores.append(_dot_nt(q_all, k_new) + _dot(neg_all, exp_ref[n_pages]) + own_rows)
    values.append((newk_ref[:, 3 * GROUP_LANES:].astype(BF16), False))
    sel = softmax_values(scores, values)

    kpos = past_len - wb + lax.broadcasted_iota(jnp.int32, (1, wb), 1)
    in_band = jnp.where(kpos > jnp.concatenate([qpos] * n_rep, axis=0) - WINDOW, 0.0, NEG)
    scores = [_dot(q_all, cw_ref[0, 0].astype(BF16)) + in_band,
              _dot_nt(q_all, neww_ref[:, :GROUP_LANES].astype(BF16)) + own_rows]
    values = [(cw_ref[0, 1].astype(BF16), True), (neww_ref[:, GROUP_LANES:].astype(BF16), False)]
    win = softmax_values(scores, values)

    for r in reps:
        rs = slice(r * slab, (r + 1) * slab)
        piece = _combine_slab(o_cmp[rs], [None, sel[1][rs], sel[2][rs]], [None, win[1][rs], win[2][rs]],
                              gates[rs], own, tq)
        out_ref[0, :, r * GROUP_LANES:(r + 1) * GROUP_LANES] = piece

    new_t = neww_ref[...].T
    lane = lax.broadcasted_iota(jnp.int32, (GROUP_LANES, LANES), 1)
    for slot in range(2):
        shifted = pltpu.roll(cw_ref[0, slot], wb - t_new, axis=1)
        fresh = pltpu.roll(new_t[slot * GROUP_LANES:(slot + 1) * GROUP_LANES], LANES - t_new, axis=1)
        nw_ref[0, slot, :, :wb - LANES] = shifted[:, :wb - LANES]
        nw_ref[0, slot, :, wb - LANES:] = jnp.where(lane >= LANES - t_new, fresh, shifted[:, wb - LANES:])


def _nsa_decode(q8, gates_rows, kvnew8, winnew8, cache, cache_win, page_table, cw, *, past_len, t_new, win_off):
    B, tq, QW = q8.shape
    n_pages = page_table.shape[1]
    page = cache.shape[3]
    wb = cache_win.shape[3]
    n_rep = QW // GROUP_LANES
    rows = n_rep * NSA_KV_HEADS * tq
    tk = past_len + t_new
    n_cmp = (tk - CMP_BLOCK) // CMP_STRIDE + 1
    n_sel_blocks = -(-tk // SEL_BLOCK)
    nrow = past_len // CMP_STRIDE
    assert tq == SUBLANES and past_len == n_pages * page and wb % page == 0 and page == LANES
    assert n_cmp <= nrow <= LANES and (n_cmp - 1) * CMP_STRIDE + CMP_BLOCK <= past_len
    assert n_sel_blocks <= LANES and 0 < t_new < tq and wb == WINDOW
    mmap = _cmp_to_sel(n_cmp, n_sel_blocks)[:nrow]
    expand = _block_expand(n_pages + 1, page, n_sel_blocks)
    const = lambda a: pl.BlockSpec(a.shape, lambda b, pt: (0,) * a.ndim)
    page_spec = lambda p: pl.BlockSpec((1,) + cache.shape[1:], lambda b, pt: (pt[b, p], 0, 0, 0))
    per_b = lambda a: pl.BlockSpec((1,) + a.shape[1:], lambda b, pt: (b, 0, 0))
    win_block = (1,) + cache_win.shape[1:]
    consts = list(cw) + [mmap, expand]
    grid_spec = pltpu.PrefetchScalarGridSpec(
        num_scalar_prefetch=1,
        grid=(B,),
        in_specs=[page_spec(p) for p in range(n_pages)]
        + [per_b(q8), pl.BlockSpec((rows, 3), lambda b, pt: (b, 0)), per_b(kvnew8), per_b(winnew8),
           pl.BlockSpec(win_block, lambda b, pt: (win_off + b, 0, 0, 0))] + [const(a) for a in consts],
        out_specs=[per_b(q8), pl.BlockSpec(win_block, lambda b, pt: (b, 0, 0, 0))],
        scratch_shapes=[pltpu.VMEM((page, kvnew8.shape[2]), F32), pltpu.VMEM((page, winnew8.shape[2]), F32),
                        pltpu.VMEM((2, SUBLANES, cw[2].shape[1]), F32)],
    )
    return pl.pallas_call(
        functools.partial(_nsa_decode_kernel, n_pages=n_pages, page=page, tq=tq, t_new=t_new, past_len=past_len,
                          n_cmp=n_cmp, n_sel_blocks=n_sel_blocks, wb=wb),
        grid_spec=grid_spec,
        out_shape=[jax.ShapeDtypeStruct(q8.shape, F32), jax.ShapeDtypeStruct((B,) + cache_win.shape[1:], F32)],
        compiler_params=_params(("arbitrary",)),
        name="nsa_decode",
    )(page_table, *([cache] * n_pages), q8, gates_rows, kvnew8, winnew8, cache_win, *consts)


def _layer_weights(i, prm, d_ml, d_nsa):
    H, G, dk = ML_HEADS, NSA_KV_HEADS, NSA_HEAD_DIM
    R = d_nsa // (G * dk)
    kvw = G * dk
    w_in = prm["w_in"][i]
    o = 0
    c_in, o = w_in[:, o:o + d_ml], o + d_ml
    v_ml, o = w_in[:, o:o + d_ml], o + d_ml
    o_ml, o = w_in[:, o:o + d_ml], o + d_ml
    i_ml, o = w_in[:, o:o + H], o + H
    f_ml, o = w_in[:, o:o + H], o + H
    q_n, o = w_in[:, o:o + d_nsa], o + d_nsa
    kv, o = w_in[:, o:o + 6 * kvw], o + 6 * kvw
    g_n = w_in[:, o:]
    K = w_in.shape[0]
    q_perm = q_n.reshape(K, G, R, dk).transpose(0, 2, 1, 3).reshape(K, d_nsa)
    w_main = jnp.concatenate([c_in, v_ml, o_ml, q_perm, kv], axis=1).astype(BF16)
    w_kvw_t = kv.T.astype(BF16)
    small = jnp.concatenate([i_ml, f_ml, g_n], axis=1)
    w_small = jnp.pad(small, ((0, 0), (0, LANES - small.shape[1]))).astype(BF16)
    w_out = prm["w_out"][i]
    w_out_nsa = w_out[d_ml:].reshape(G, R, dk, -1).transpose(1, 0, 2, 3).reshape(d_nsa, -1)
    w_out_p = jnp.concatenate([w_out[:d_ml], w_out_nsa], axis=0).astype(BF16)
    Dh = d_ml // H
    gate_bias = jnp.pad(jnp.concatenate([prm["b_i"][i], prm["b_f"][i]]), (0, LANES - 2 * H)).reshape(1, LANES)
    w2 = prm["cmp_w2"][i]
    w2_placed = jnp.stack([jnp.stack([jnp.pad(w2[s], ((0, 0), (g * dk, (G - 1 - g) * dk))) for g in range(G)])
                           for s in range(2)]).astype(BF16)
    perm = np.zeros((LANES, LANES), np.float32)
    tok = np.arange(LANES)
    perm[(tok % CMP_STRIDE) * (LANES // CMP_STRIDE) + tok // CMP_STRIDE, tok] = 1.0
    cmp_w = (prm["cmp_pe"][i].reshape(2, -1), prm["cmp_w1"][i].astype(BF16), prm["cmp_b1"][i], w2_placed,
             jnp.tile(prm["cmp_b2"][i], (1, G)), jnp.asarray(perm, BF16))
    return dict(
        w_main=w_main, w_kvw_t=w_kvw_t, w_small=w_small, w_out=w_out_p,
        w_up=prm["w_up"][i].astype(BF16), w_down=prm["w_down"][i].astype(BF16),
        w_pl=prm["w_pl"][i].astype(BF16), w_pl_gate=prm["w_pl_gate"][i].astype(BF16),
        conv_w=prm["conv_w"][i], conv_b=prm["conv_b"][i].reshape(1, -1),
        wq=prm["w_q_ml"][i].astype(BF16),
        wkt=(jnp.swapaxes(prm["w_k_ml"][i], 1, 2) * (Dh ** -0.5)).astype(BF16),
        gate_bias=gate_bias, g_ml=prm["g_ml"][i].reshape(1, -1), cmp=cmp_w)


def _gate_rows(small, B, nqt, tq, G, R):
    g = small[:, 2 * ML_HEADS:2 * ML_HEADS + 3 * G * R].reshape(B, nqt, tq, G, R, 3)
    return g.transpose(0, 1, 4, 3, 2, 5).reshape(B * nqt * R * G * tq, 3)


def _pad_time(a, B, t, tp):
    return jnp.pad(a.reshape(B, t, -1), ((0, 0), (0, tp - t), (0, 0)))


def _layer(h, pl_e, i, prm, lw, mem, *, B, T):
    M, D = h.shape
    G, dk, H = NSA_KV_HEADS, NSA_HEAD_DIM, ML_HEADS
    d_ml = lw["conv_w"].shape[1]
    d_nsa = lw["w_out"].shape[0] - d_ml
    R = d_nsa // (G * dk)
    Dh = d_ml // H
    kvw = G * dk
    tm = min(M, 512)
    g_pre = prm["g_pre_mix"][i]
    lanes_last = lambda a, lead: a.reshape(lead + (G, dk, a.shape[-1]))

    ml_w = (lw["conv_w"], lw["conv_b"], lw["wq"], lw["wkt"], lw["gate_bias"], lw["g_ml"])
    if mem is None:
        cin, v, o, q, small = _norm_matmul(h, g_pre, lw["w_main"], (d_ml, d_ml, d_ml, d_nsa),
                                           w_small=lw["w_small"], tm=tm, tn=512)
        kvw_t = _norm_matmul_t(h, g_pre, lw["w_kvw_t"], B=B, T=T, tm=tm, tn=512)
        zeros = lambda *s: jnp.zeros(s, F32)
        hm, c_new, n_new, m_new = _mlstm(
            cin, v, o, small, zeros(B, SUBLANES, d_ml), zeros(B, H, Dh, Dh), zeros(B, 1, d_ml), zeros(B, 1, LANES),
            *ml_w, B=B, rows=ML_CHUNK, valid=ML_CHUNK)
        kcmp, vcmp = _compress_prompt(kvw_t, lw["cmp"], B=B, T=T)
        tq = 64
        on = _nsa_prompt(q, _gate_rows(small, B, T // tq, tq, G, R), kcmp, vcmp, kvw_t, B=B, T=T, tq=tq, ck=256)
        wlen = min(WINDOW, T)
        new_rows = lanes_last(kvw_t[:, :4 * kvw], (B, 4)).transpose(0, 4, 1, 2, 3)
        new_win = lanes_last(kvw_t[:, 4 * kvw:, T - wlen:], (B, 2)).transpose(0, 4, 1, 2, 3)
    else:
        cin, v, o, q, kv4, win2, small = _norm_matmul(
            h, g_pre, lw["w_main"], (d_ml, d_ml, d_ml, d_nsa, 4 * kvw, 2 * kvw),
            w_small=lw["w_small"], tm=tm, tn=512)
        tp = SUBLANES
        pad = lambda a: _pad_time(a, B, T, tp)
        prev = jnp.pad(mem["conv"], ((0, 0), (SUBLANES - mem["conv"].shape[1], 0), (0, 0)))
        m0 = jnp.pad(mem["m"], ((0, 0), (0, LANES - H))).reshape(B, 1, LANES)
        hm8, c_new, n_new, m_new = _mlstm(
            pad(cin).reshape(B * tp, d_ml), pad(v).reshape(B * tp, d_ml), pad(o).reshape(B * tp, d_ml),
            pad(small).reshape(B * tp, LANES), prev, mem["C"], mem["n"].reshape(B, 1, d_ml), m0,
            *ml_w, B=B, rows=tp, valid=T, c_off=i * B)
        hm = hm8.reshape(B, tp, d_ml)[:, :T].reshape(M, d_ml)
        page = mem["kv"].shape[3]
        on8, new_win_t = _nsa_decode(
            pad(q), _gate_rows(pad(small).reshape(B * tp, LANES), B, 1, tp, G, R), pad(kv4), pad(win2),
            mem["kv"], mem["win"], mem["page_table"], lw["cmp"],
            past_len=mem["page_table"].shape[1] * page, t_new=T, win_off=i * B)
        on = on8[:, :T].reshape(M, d_nsa)
        new_rows = kv4.reshape(B, T, 4, G, dk)
        new_win = lanes_last(new_win_t, (B, 2)).transpose(0, 4, 1, 2, 3)

    mixed = jnp.concatenate([hm.astype(BF16), on.astype(BF16)], axis=1)
    h = _matmul_norm_res(mixed, lw["w_out"], h, prm["g_post_mix"][i], tm=tm, tk=1024)
    u, = _norm_matmul(h, prm["g_pre_mlp"][i], lw["w_up"], (lw["w_up"].shape[1],), tm=min(M, 1024), tn=1024,
                      act="relu2", out_dtype=BF16)
    h = _matmul_norm_res(u, lw["w_down"], h, prm["g_post_mlp"][i], tm=tm, tk=1024)
    h = _ple(h, pl_e, lw["w_pl_gate"], lw["w_pl"], prm["g_pl"][i], tm=min(M, 256))

    state = (new_rows, new_win, c_new, n_new.reshape(B, H, Dh), m_new[:, 0, :H], cin.reshape(B, T, d_ml)[:, T - 3:])
    return h, state


def kernel(x_prompt, x_sample, cache_kv, cache_win, state_C, state_n, state_m, state_conv, page_table,
           p_prompt, p_sample, g_pre_mix, w_in, conv_w, conv_b, w_q_ml, w_k_ml, b_i, b_f, g_ml,
           cmp_pe, cmp_w1, cmp_b1, cmp_w2, cmp_b2, w_out, g_post_mix, g_pre_mlp, w_up, w_down,
           g_post_mlp, w_pl, g_pl, w_pl_gate):
    prm = dict(g_pre_mix=g_pre_mix, w_in=w_in, conv_w=conv_w, conv_b=conv_b, w_q_ml=w_q_ml, w_k_ml=w_k_ml,
               b_i=b_i, b_f=b_f, g_ml=g_ml, cmp_pe=cmp_pe, cmp_w1=cmp_w1, cmp_b1=cmp_b1, cmp_w2=cmp_w2,
               cmp_b2=cmp_b2, w_out=w_out, g_post_mix=g_post_mix, g_pre_mlp=g_pre_mlp, w_up=w_up,
               w_down=w_down, g_post_mlp=g_post_mlp, w_pl=w_pl, g_pl=g_pl, w_pl_gate=w_pl_gate)
    Bp, Tp, D = x_prompt.shape
    Bs, Ts, _ = x_sample.shape
    depth = w_in.shape[0]
    d_ml = conv_w.shape[2]
    d_nsa = w_out.shape[1] - d_ml
    hp = x_prompt.reshape(Bp * Tp, D)
    hs = x_sample.reshape(Bs * Ts, D)
    n_pool, page = cache_kv.shape[1:3]
    kv_t = jnp.transpose(cache_kv, (0, 1, 3, 4, 5, 2)).reshape(depth * n_pool, cache_kv.shape[3], -1, page)
    win_t = jnp.transpose(cache_win, (0, 1, 3, 4, 5, 2)).reshape(depth * Bs, cache_win.shape[3], -1, cache_win.shape[2])
    c_all = state_C.reshape((depth * Bs,) + state_C.shape[2:])
    sp, ss = [], []
    for i in range(depth):
        lw = _layer_weights(i, prm, d_ml, d_nsa)
        mem = dict(kv=kv_t, page_table=page_table + i * n_pool, win=win_t, C=c_all, n=state_n[i],
                   m=state_m[i], conv=state_conv[i])
        hp, st_p = _layer(hp, p_prompt[i].reshape(Bp * Tp, -1), i, prm, lw, None, B=Bp, T=Tp)
        hs, st_s = _layer(hs, p_sample[i].reshape(Bs * Ts, -1), i, prm, lw, mem, B=Bs, T=Ts)
        sp.append(st_p)
        ss.append(st_s)

    stk = lambda lst, j: jnp.stack([s[j] for s in lst])
    return (hp.reshape(Bp, Tp, D), hs.reshape(Bs, Ts, D), stk(sp, 0), stk(ss, 0), stk(sp, 1), stk(ss, 1),
            stk(sp, 2), stk(ss, 2), stk(sp, 3), stk(ss, 3), stk(sp, 4), stk(ss, 4), stk(sp, 5), stk(ss, 5))
```

```python
import functools

import numpy as np
import jax
import jax.numpy as jnp
from jax import lax
from jax.experimental import pallas as pl
from jax.experimental.pallas import tpu as pltpu

ML_HEADS = 4
NSA_HEAD_DIM = 64
NSA_KV_HEADS = 4
CMP_BLOCK = 32
CMP_STRIDE = 16
SEL_BLOCK = 64
N_SEL = 16
WINDOW = 512
EPS = 1e-6
NEG = -1e30
FORCE = 1e6

LANES = 128
SUBLANES = 8
VMEM_LIMIT_BYTES = 56 * 1024 * 1024

ML_CHUNK = 128
MLSTM_SEQS_PER_STEP = 1
GROUP_LANES = NSA_KV_HEADS * NSA_HEAD_DIM
QK_SCALE = NSA_HEAD_DIM ** -0.5 * 1.4426950408889634
MASK_BIG = 1e30

F32 = jnp.float32
BF16 = jnp.bfloat16


def _dot(a, b):
    return jnp.dot(a, b, preferred_element_type=F32)


def _dot_nt(a, b):
    return lax.dot_general(a, b, (((1,), (1,)), ((), ())), preferred_element_type=F32)


def _params(semantics):
    return pltpu.CompilerParams(dimension_semantics=semantics, vmem_limit_bytes=VMEM_LIMIT_BYTES)


def _rms_rows(x):
    return x * lax.rsqrt(jnp.mean(x * x, axis=-1, keepdims=True) + EPS)


def _norm_matmul_kernel(x_ref, g_ref, w_ref, *rest, seg_tiles, has_small, act):
    if has_small:
        ws_ref, rest = rest[0], rest[1:]
    n_out = len(seg_tiles) + (1 if has_small else 0)
    out_refs, xn_ref = rest[:n_out], rest[n_out]
    j = pl.program_id(1)

    @pl.when(j == 0)
    def _():
        xn = (_rms_rows(x_ref[...]) * g_ref[...]).astype(BF16)
        xn_ref[...] = xn
        if has_small:
            out_refs[-1][...] = _dot(xn, ws_ref[...])

    y = _dot(xn_ref[...], w_ref[...])
    if act == "relu2":
        y = jnp.square(jnp.maximum(y, 0.0))
    start = 0
    for k, n in enumerate(seg_tiles):
        @pl.when((j >= start) & (j < start + n))
        def _(k=k):
            out_refs[k][...] = y.astype(out_refs[k].dtype)
        start += n


def _norm_matmul(x, g, w, seg_cols, *, w_small=None, tm, tn, act=None, out_dtype=F32):
    M, K = x.shape
    seg_tiles = tuple(c // tn for c in seg_cols)
    assert all(c % tn == 0 for c in seg_cols) and M % tm == 0
    has_small = w_small is not None
    starts = np.concatenate([[0], np.cumsum(seg_tiles)[:-1]]).tolist()

    def out_map(start, n):
        return lambda i, j: (i, jnp.clip(j - start, 0, n - 1))

    in_specs = [pl.BlockSpec((tm, K), lambda i, j: (i, 0)),
                pl.BlockSpec((1, K), lambda i, j: (0, 0)),
                pl.BlockSpec((K, tn), lambda i, j: (0, j))]
    args = [x, g.reshape(1, K), w]
    out_specs = [pl.BlockSpec((tm, tn), out_map(s, n)) for s, n in zip(starts, seg_tiles)]
    out_shape = [jax.ShapeDtypeStruct((M, c), out_dtype) for c in seg_cols]
    if has_small:
        in_specs.append(pl.BlockSpec((K, LANES), lambda i, j: (0, 0)))
        args.append(w_small)
        out_specs.append(pl.BlockSpec((tm, LANES), lambda i, j: (i, 0)))
        out_shape.append(jax.ShapeDtypeStruct((M, LANES), F32))
    return pl.pallas_call(
        functools.partial(_norm_matmul_kernel, seg_tiles=seg_tiles, has_small=has_small, act=act),
        grid=(M // tm, sum(seg_tiles)),
        in_specs=in_specs, out_specs=out_specs, out_shape=out_shape,
        scratch_shapes=[pltpu.VMEM((tm, K), BF16)],
        compiler_params=_params(("parallel", "arbitrary")),
        name="norm_matmul",
    )(*args)


def _norm_matmul_t_kernel(x_ref, g_ref, wt_ref, o_ref, xn_ref):
    @pl.when(pl.program_id(1) == 0)
    def _():
        xn_ref[...] = (_rms_rows(x_ref[...]) * g_ref[...]).astype(BF16)

    o_ref[0] = _dot_nt(wt_ref[...], xn_ref[...])


def _norm_matmul_t(x, g, wt, *, B, T, tm, tn):
    M, K = x.shape
    N = wt.shape[0]
    assert T % tm == 0 and N % tn == 0 and M == B * T
    per = T // tm
    return pl.pallas_call(
        _norm_matmul_t_kernel,
        grid=(M // tm, N // tn),
        in_specs=[pl.BlockSpec((tm, K), lambda i, j: (i, 0)),
                  pl.BlockSpec((1, K), lambda i, j: (0, 0)),
                  pl.BlockSpec((tn, K), lambda i, j: (j, 0))],
        out_specs=pl.BlockSpec((1, tn, tm), lambda i, j: (i // per, j, i % per)),
        out_shape=jax.ShapeDtypeStruct((B, N, T), F32),
        scratch_shapes=[pltpu.VMEM((tm, K), BF16)],
        compiler_params=_params(("parallel", "arbitrary")),
        name="norm_matmul_t",
    )(x, g.reshape(1, K), wt)


def _matmul_norm_res_kernel(a_ref, w_ref, h_ref, g_ref, o_ref, acc_ref):
    k = pl.program_id(1)

    @pl.when(k == 0)
    def _():
        acc_ref[...] = jnp.zeros_like(acc_ref)

    acc_ref[...] += _dot(a_ref[...], w_ref[...])

    @pl.when(k == pl.num_programs(1) - 1)
    def _():
        o_ref[...] = h_ref[...] + _rms_rows(acc_ref[...]) * g_ref[...]


def _matmul_norm_res(a, w, h, g, *, tm, tk):
    M, K = a.shape
    N = w.shape[1]
    assert M % tm == 0 and K % tk == 0
    return pl.pallas_call(
        _matmul_norm_res_kernel,
        grid=(M // tm, K // tk),
        in_specs=[pl.BlockSpec((tm, tk), lambda i, k: (i, k)),
                  pl.BlockSpec((tk, N), lambda i, k: (k, 0)),
                  pl.BlockSpec((tm, N), lambda i, k: (i, 0)),
                  pl.BlockSpec((1, N), lambda i, k: (0, 0))],
        out_specs=pl.BlockSpec((tm, N), lambda i, k: (i, 0)),
        out_shape=jax.ShapeDtypeStruct((M, N), F32),
        scratch_shapes=[pltpu.VMEM((tm, N), F32)],
        compiler_params=_params(("parallel", "arbitrary")),
        name="matmul_norm_res",
    )(a, w, h, g.reshape(1, N))


def _ple_kernel(h_ref, p_ref, wg_ref, wp_ref, g_ref, o_ref):
    h = h_ref[...]
    gate = jax.nn.sigmoid(_dot(_rms_rows(h).astype(BF16), wg_ref[...]))
    e = _dot(p_ref[...].astype(BF16), wp_ref[...])
    o_ref[...] = h + gate * (_rms_rows(e) * g_ref[...])


def _ple(h, p, wg, wp, g, *, tm):
    M, N = h.shape
    P = p.shape[1]
    assert M % tm == 0
    return pl.pallas_call(
        _ple_kernel,
        grid=(M // tm,),
        in_specs=[pl.BlockSpec((tm, N), lambda i: (i, 0)),
                  pl.BlockSpec((tm, P), lambda i: (i, 0)),
                  pl.BlockSpec((N, N), lambda i: (0, 0)),
                  pl.BlockSpec((P, N), lambda i: (0, 0)),
                  pl.BlockSpec((1, N), lambda i: (0, 0))],
        out_specs=pl.BlockSpec((tm, N), lambda i: (i, 0)),
        out_shape=jax.ShapeDtypeStruct((M, N), F32),
        compiler_params=_params(("parallel",)),
        name="ple",
    )(h, p, wg, wp, g.reshape(1, N))


def _shift_rows(x, prev8, s):
    rows = x.shape[0]
    xs = pltpu.roll(x, s, axis=0)
    rid = lax.broadcasted_iota(jnp.int32, (SUBLANES, x.shape[1]), 0)
    head = jnp.where(rid < s, pltpu.roll(prev8, s, axis=0), xs[:SUBLANES])
    if rows == SUBLANES:
        return head
    return jnp.concatenate([head, xs[SUBLANES:]], axis=0)


def _mlstm_kernel(*refs, nb, **static):
    n_seq, n_shared = 8, 7
    seq_in, shared = refs[:n_seq], refs[n_seq:n_seq + n_shared]
    rest = refs[n_seq + n_shared:]
    per_seq = [[r.at[j] for r in seq_in] + list(shared) + [r.at[j] for r in rest] for j in range(nb)]
    for args in per_seq:
        _mlstm_load_state(*args, **static)
    m_rows = [_mlstm_seq(*args, **static) for args in per_seq]
    for args, m_row in zip(per_seq, m_rows):
        _mlstm_store_state(*args, m_row, **static)


def _mlstm_load_state(cin_ref, v_ref, o_ref, sm_ref, prev_ref, c0_ref, n0_ref, m0_ref,
                      cw_ref, cb_ref, wq_ref, wkt_ref, gb_ref, gml_ref, ltri_ref,
                      hm_ref, cout_ref, nout_ref, mout_ref,
                      carry_ref, cext_ref, m_ref, chpad_ref, vpad_ref, gpad_ref,
                      *, L, rows, valid, H, Dh):
    @pl.when(pl.program_id(1) == 0)
    def _():
        carry_ref[...] = prev_ref[...]
        m_ref[...] = m0_ref[...]
        for h in range(H):
            cext_ref[h, :, :Dh] = c0_ref[h]
            n_row = n0_ref[:, h * Dh:(h + 1) * Dh]
            cext_ref[h, :, Dh:] = jnp.broadcast_to(n_row, (LANES, Dh)).T

    if rows < L:
        @pl.when((pl.program_id(0) == 0) & (pl.program_id(1) == 0))
        def _():
            chpad_ref[...] = jnp.zeros_like(chpad_ref)
            vpad_ref[...] = jnp.zeros_like(vpad_ref)
            gpad_ref[...] = jnp.zeros_like(gpad_ref)


def _mlstm_store_state(cin_ref, v_ref, o_ref, sm_ref, prev_ref, c0_ref, n0_ref, m0_ref,
                       cw_ref, cb_ref, wq_ref, wkt_ref, gb_ref, gml_ref, ltri_ref,
                       hm_ref, cout_ref, nout_ref, mout_ref,
                       carry_ref, cext_ref, m_ref, chpad_ref, vpad_ref, gpad_ref, m_new_row,
                       *, L, rows, valid, H, Dh):
    @pl.when(pl.program_id(1) == pl.num_programs(1) - 1)
    def _():
        mout_ref[...] = m_new_row
        for h in range(H):
            cout_ref[h] = cext_ref[h, :, :Dh]
            nout_ref[:, h * Dh:(h + 1) * Dh] = cext_ref[h, :, Dh:].T[0:1, :]


def _mlstm_seq(cin_ref, v_ref, o_ref, sm_ref, prev_ref, c0_ref, n0_ref, m0_ref,
               cw_ref, cb_ref, wq_ref, wkt_ref, gb_ref, gml_ref, ltri_ref,
               hm_ref, cout_ref, nout_ref, mout_ref,
               carry_ref, cext_ref, m_ref, chpad_ref, vpad_ref, gpad_ref,
               *, L, rows, valid, H, Dh):
    x = cin_ref[...]
    prev8 = carry_ref[...]
    conv = cb_ref[...] + cw_ref[3:4, :] * x
    for s in (1, 2, 3):
        conv = conv + cw_ref[3 - s:4 - s, :] * _shift_rows(x, prev8, s)
    carry_ref[...] = x[rows - SUBLANES:, :]
    ch = conv * jax.nn.sigmoid(conv)

    if rows < L:
        chpad_ref[:rows, :] = ch
        vpad_ref[:rows, :] = v_ref[...]
        gpad_ref[:rows, :] = sm_ref[...]
        ch, v_all, sm = chpad_ref[...], vpad_ref[...], gpad_ref[...]
    else:
        v_all, sm = v_ref[...], sm_ref[...]
    ch = ch.astype(BF16)

    lane = lax.broadcasted_iota(jnp.int32, (L, LANES), 1)
    rid = lax.broadcasted_iota(jnp.int32, (L, LANES), 0)
    pre = sm + gb_ref[...]
    gates = jnp.where(lane < H, pre, jax.nn.log_sigmoid(pre))
    gates = jnp.where(rid < valid, gates, jnp.where(lane < H, NEG, 0.0))
    hi = gates.astype(BF16)
    r1 = gates - hi.astype(F32)
    mid = r1.astype(BF16)
    lo = (r1 - mid.astype(F32)).astype(BF16)
    ltri = ltri_ref[...]
    bc_all = _dot(ltri, hi) + _dot(ltri, mid) + _dot(ltri, lo)
    colform = jnp.where(lane < H, gates, bc_all)
    rowform = colform.T

    r_i = lax.broadcasted_iota(jnp.int32, (L, L), 0)
    c_i = lax.broadcasted_iota(jnp.int32, (L, L), 1)
    causal = c_i <= r_i
    lane1 = lax.broadcasted_iota(jnp.int32, (1, LANES), 1)
    m_row = m_ref[...]
    m_new_row = m_row
    ones = jnp.ones((L, LANES), F32)

    for h in range(H):
        hs = slice(h * Dh, (h + 1) * Dh)
        ch_h = ch[:, hs]
        q_h = _dot(ch_h, wq_ref[h]).astype(BF16)
        kt_h = _dot_nt(wkt_ref[h], ch_h).astype(BF16)
        v_h = v_all[:, hs]
        li_col = colform[:, h:h + 1]
        bc_col = colform[:, H + h:H + h + 1]
        li_row = rowform[h:h + 1, :]
        bc_row = rowform[H + h:H + h + 1, :]
        m_h = m_row[:, h:h + 1]

        logd = jnp.where(causal, bc_col - bc_row + li_row, NEG)
        inter = m_h + bc_col
        mt = jnp.maximum(jnp.max(logd, axis=-1, keepdims=True), inter)
        dm = jnp.exp(logd - mt)
        a_int = jnp.exp(inter - mt)
        s = _dot(q_h, kt_h) * dm
        v_ext = jnp.concatenate([v_h, ones], axis=1)
        cext = cext_ref[h]
        tot = _dot(s.astype(BF16), v_ext.astype(BF16)) + a_int * _dot(q_h, cext.astype(BF16))
        num = tot[:, :Dh]
        den = tot[:, Dh:Dh + 1]
        hh = num / jnp.maximum(jnp.abs(den), jnp.exp(-mt))

        m_last = mt[L - 1:L, :]
        bc_last = bc_col[L - 1:L, :]
        w_col = jnp.exp(bc_last - bc_col + li_col - m_last)
        dec = jnp.exp(m_h + bc_last - m_last)
        cext_ref[h] = dec * cext + _dot(kt_h, (w_col * v_ext).astype(BF16))
        m_new_row = jnp.where(lane1 == h, m_last, m_new_row)

        y = _rms_rows(hh) * gml_ref[:, hs]
        y = y[:rows] * jax.nn.sigmoid(o_ref[:, hs])
        hm_ref[:, hs] = y

    m_ref[...] = m_new_row
    return m_new_row


def _mlstm(cin, v, o, small, prev, c0, n0, m0, cw, cb, wq, wkt, gb, gml, *, B, rows, valid, c_off=0):
    L = ML_CHUNK
    H = ML_HEADS
    D = cin.shape[1]
    Dh = D // H
    nc = cin.shape[0] // (B * rows)
    nb = MLSTM_SEQS_PER_STEP
    assert rows == L or (nc == 1 and rows == SUBLANES)
    assert B % nb == 0 and c_off % nb == 0
    ltri = jnp.asarray(np.tril(np.ones((L, L), np.float32)), BF16)
    seq3 = lambda a: a.reshape(B, nc * rows, a.shape[1])
    tok = lambda g, c: (g, c, 0)
    per_g3 = lambda g, c: (g, 0, 0)
    per_g4 = lambda g, c: (g, 0, 0, 0)
    const2 = lambda g, c: (0, 0)
    const3 = lambda g, c: (0, 0, 0)
    hm, cout, nout, mout = pl.pallas_call(
        functools.partial(_mlstm_kernel, nb=nb, L=L, rows=rows, valid=valid, H=H, Dh=Dh),
        grid=(B // nb, nc),
        in_specs=[pl.BlockSpec((nb, rows, D), tok), pl.BlockSpec((nb, rows, D), tok), pl.BlockSpec((nb, rows, D), tok),
                  pl.BlockSpec((nb, rows, LANES), tok),
                  pl.BlockSpec((nb, SUBLANES, D), per_g3),
                  pl.BlockSpec((nb, H, Dh, Dh), lambda g, c: (c_off // nb + g, 0, 0, 0)),
                  pl.BlockSpec((nb, 1, D), per_g3),
                  pl.BlockSpec((nb, 1, LANES), per_g3),
                  pl.BlockSpec((4, D), const2), pl.BlockSpec((1, D), const2),
                  pl.BlockSpec((H, Dh, Dh), const3), pl.BlockSpec((H, Dh, Dh), const3),
                  pl.BlockSpec((1, LANES), const2), pl.BlockSpec((1, D), const2),
                  pl.BlockSpec((L, L), const2)],
        out_specs=[pl.BlockSpec((nb, rows, D), tok),
                   pl.BlockSpec((nb, H, Dh, Dh), per_g4),
                   pl.BlockSpec((nb, 1, D), per_g3),
                   pl.BlockSpec((nb, 1, LANES), per_g3)],
        out_shape=[jax.ShapeDtypeStruct((B, nc * rows, D), F32),
                   jax.ShapeDtypeStruct((B, H, Dh, Dh), F32),
                   jax.ShapeDtypeStruct((B, 1, D), F32),
                   jax.ShapeDtypeStruct((B, 1, LANES), F32)],
        scratch_shapes=[pltpu.VMEM((nb, SUBLANES, D), F32),
                        pltpu.VMEM((nb, H, Dh, Dh + LANES), F32),
                        pltpu.VMEM((nb, 1, LANES), F32),
                        pltpu.VMEM((nb, L, D), F32), pltpu.VMEM((nb, L, D), F32), pltpu.VMEM((nb, L, LANES), F32)],
        compiler_params=_params(("arbitrary", "arbitrary")),
        name="mlstm",
    )(seq3(cin), seq3(v), seq3(o), seq3(small), prev, c0, n0, m0, cw, cb, wq, wkt, gb, gml, ltri)
    return hm.reshape(cin.shape), cout, nout, mout


def _deinterleave_t(perm, groups):
    per = LANES // CMP_STRIDE
    ys = [_dot_nt(perm, x_t.astype(BF16)) for x_t in groups]
    return [jnp.concatenate([y[l * per:(l + 1) * per] for y in ys], axis=0) for l in range(CMP_STRIDE)]


def _pe_bias(slot, pe_ref, w1_ref, b1_ref):
    pe = jnp.broadcast_to(pe_ref[slot:slot + 1, :], (SUBLANES, pe_ref.shape[1])).astype(BF16)
    return _dot(pe, w1_ref[slot]) + b1_ref[slot:slot + 1, :]


def _compress_slot(xs, slot, bias, w1_ref, w2_ref, b2_ref):
    dk = NSA_HEAD_DIM
    half = CMP_BLOCK // 2
    nrow = xs[0].shape[0]
    pack = GROUP_LANES // dk
    out = None
    for g in range(NSA_KV_HEADS):
        gs = slice(g * dk, (g + 1) * dk)
        first = None
        second = None
        for l0 in range(0, half, pack):
            xg = jnp.concatenate([xs[l0 + j][:, gs] for j in range(pack)], axis=1).astype(BF16)
            a = _dot(xg, w1_ref[slot, l0 * dk:(l0 + pack) * dk, :])
            bb = _dot(xg, w1_ref[slot, (half + l0) * dk:(half + l0 + pack) * dk, :])
            first = a if first is None else first + a
            second = bb if second is None else second + bb
        hid = jax.nn.gelu(first + pltpu.roll(second, nrow - 1, axis=0) + bias)
        og = _dot(hid.astype(BF16), w2_ref[slot, g])
        out = og if out is None else out + og
    return out + b2_ref[slot:slot + 1, :]


def _expand_q(q, tq):
    slab = NSA_KV_HEADS * tq
    lane_g = lax.broadcasted_iota(jnp.int32, (slab, GROUP_LANES), 1) // NSA_HEAD_DIM
    row_g = lax.broadcasted_iota(jnp.int32, (slab, GROUP_LANES), 0) // tq
    own = lane_g == row_g
    n_rep = q.shape[1] // GROUP_LANES
    slabs = []
    for r in range(n_rep):
        qr = q[:, r * GROUP_LANES:(r + 1) * GROUP_LANES] * QK_SCALE
        slabs.append(jnp.where(own, jnp.concatenate([qr] * NSA_KV_HEADS, axis=0), 0.0).astype(BF16))
    return slabs, own


def _attend(carry, q, chunks):
    m, l, acc = carry
    scores = []
    for k, _, bias in chunks:
        s = _dot(q, k)
        if bias is not None:
            s = _add_slab_bias(s, bias) if jnp.ndim(bias) == 2 else s + bias
        scores.append(s)
    mx = scores[0]
    for s in scores[1:]:
        mx = jnp.maximum(mx, s)
    m_new = jnp.maximum(m, jnp.max(mx, axis=1, keepdims=True))
    alpha = jnp.exp2(m - m_new)
    ps, pv = None, None
    for s, (_, v, _) in zip(scores, chunks):
        p = jnp.exp2(s - m_new)
        ps = p if ps is None else ps + p
        d = _dot_nt(p.astype(BF16), v)
        pv = d if pv is None else pv + d
    return m_new, alpha * l + jnp.sum(ps, axis=1, keepdims=True), alpha * acc + pv


def _add_slab_bias(s, bias):
    rows, ck = s.shape
    return (s.reshape(rows // bias.shape[0], bias.shape[0], ck) + bias[None]).reshape(rows, ck)


def _softmax_values(scores, values):
    mx = scores[0]
    for s in scores[1:]:
        mx = jnp.maximum(mx, s) if s.shape == mx.shape else mx
    m = mx.max(axis=1, keepdims=True)
    for s in scores[1:]:
        if s.shape != mx.shape:
            m = jnp.maximum(m, s.max(axis=1, keepdims=True))
    l, acc, ps = None, None, None
    for s, (v, v_t) in zip(scores, values):
        p = jnp.exp2(s - m)
        if p.shape == mx.shape:
            ps = p if ps is None else ps + p
        else:
            ls = p.sum(axis=1, keepdims=True)
            l = ls if l is None else l + ls
        pv = _dot_nt(p.astype(BF16), v) if v_t else _dot(p.astype(BF16), v)
        acc = pv if acc is None else acc + pv
    ls = ps.sum(axis=1, keepdims=True)
    return m, (ls if l is None else l + ls), acc


def _attend_init(rows):
    return (jnp.full((rows, 1), NEG, F32), jnp.zeros((rows, 1), F32), jnp.zeros((rows, GROUP_LANES), F32))


def _cmp_rows(q, mask, kcmp, vcmp):
    rows, slab = q.shape[0], mask.shape[0]
    s = _dot_nt(q, kcmp).reshape(rows // slab, slab, mask.shape[1])
    s = jnp.where(mask[None], s, NEG)
    e = jnp.where(mask[None], jnp.exp2(s - jnp.max(s, axis=2, keepdims=True)), 0.0)
    p = e * (1.0 / jnp.maximum(jnp.sum(e, axis=2, keepdims=True), 1e-30))
    p = p.reshape(rows, mask.shape[1]).astype(BF16)
    return _dot(p, vcmp), p


def _cmp_mask(rows, n_rows_cmp, qpos_col, n_cmp):
    lane = lax.broadcasted_iota(jnp.int32, (rows, n_rows_cmp), 1)
    return (lane * CMP_STRIDE + (CMP_BLOCK - 1) <= qpos_col) & (lane < n_cmp)


def _block_scores(imp, blk, cur, n_sel_blocks):
    forced = (blk == 0) | (blk == cur) | (blk == cur - 1)
    score = jnp.where(blk <= cur, jnp.where(forced, FORCE, imp), -1.0)
    return jnp.where(blk < n_sel_blocks, score, -2.0)


def _select_blocks(imp, qpos_col, n_sel_blocks):
    rows = imp.shape[0]
    blk = lax.broadcasted_iota(jnp.int32, (rows, LANES), 1)
    score = _block_scores(imp, blk, qpos_col // SEL_BLOCK, n_sel_blocks)
    rank = jnp.zeros((rows, LANES), F32)
    for j in range(n_sel_blocks):
        col = score[:, j:j + 1]
        ge = jnp.where(col >= score, 1.0, 0.0)
        gt = jnp.where(col > score, 1.0, 0.0)
        rank = rank + jnp.where(blk > j, ge, gt)
    chosen = jnp.where(rank < float(min(N_SEL, n_sel_blocks)), score, -1.0) >= 0.0
    return jnp.where(chosen, 0.0, -MASK_BIG).astype(BF16)


def _select_blocks_t(imp_t, qpos_row, n_sel_blocks):
    nb = -(-n_sel_blocks // SUBLANES) * SUBLANES
    cols = imp_t.shape[1]
    blk = lax.broadcasted_iota(jnp.int32, (nb, cols), 0)
    score = _block_scores(imp_t[:nb], blk, qpos_row // SEL_BLOCK, n_sel_blocks)
    rank = jnp.zeros((nb, cols), F32)
    for j in range(n_sel_blocks):
        row = score[j:j + 1, :]
        ge = jnp.where(row >= score, 1.0, 0.0)
        gt = jnp.where(row > score, 1.0, 0.0)
        rank = rank + jnp.where(blk > j, ge, gt)
    chosen = jnp.where(rank < float(min(N_SEL, n_sel_blocks)), score, -1.0) >= 0.0
    neg_t = jnp.where(chosen, 0.0, -MASK_BIG)
    neg_t = jnp.concatenate([neg_t, jnp.zeros((LANES - nb, cols), F32)], axis=0)
    return neg_t.T.astype(BF16)


def _combine_slab(o_cmp, sel, win, gates, own, tq):
    comb = gates[:, 0:1] * o_cmp
    for col, (_, l, acc) in ((1, sel), (2, win)):
        comb = comb + (gates[:, col:col + 1] * (1.0 / jnp.maximum(l, 1e-30))) * acc
    kept = jnp.where(own, comb, 0.0)
    piece = kept[:tq]
    for g in range(1, NSA_KV_HEADS):
        piece = piece + kept[g * tq:(g + 1) * tq]
    return piece


def _compress_prompt_kernel(kc_ref, vc_ref, pe_ref, w1_ref, b1_ref, w2_ref, b2_ref, perm_ref, ko_ref, vo_ref):
    T = kc_ref.shape[2]
    for slot, (src, dst) in enumerate(((kc_ref, ko_ref), (vc_ref, vo_ref))):
        bias = _pe_bias(slot, pe_ref, w1_ref, b1_ref)[0:1]
        xs = _deinterleave_t(perm_ref[...], [src[0, :, t * LANES:(t + 1) * LANES] for t in range(T // LANES)])
        dst[0] = _compress_slot(xs, slot, bias, w1_ref, w2_ref, b2_ref)


def _compress_prompt(kvw_t, cw, *, B, T):
    nrow = T // CMP_STRIDE
    assert (T - CMP_BLOCK) // CMP_STRIDE + 1 <= nrow and T % LANES == 0
    full = lambda a: pl.BlockSpec(a.shape, lambda b: (0,) * a.ndim)
    return pl.pallas_call(
        _compress_prompt_kernel,
        grid=(B,),
        in_specs=[pl.BlockSpec((1, GROUP_LANES, T), lambda b: (b, 0, 0)),
                  pl.BlockSpec((1, GROUP_LANES, T), lambda b: (b, 1, 0))] + [full(a) for a in cw],
        out_specs=[pl.BlockSpec((1, nrow, GROUP_LANES), lambda b: (b, 0, 0))] * 2,
        out_shape=[jax.ShapeDtypeStruct((B, nrow, GROUP_LANES), F32)] * 2,
        compiler_params=_params(("parallel",)),
        name="compress_prompt",
    )(kvw_t, kvw_t, *cw)


def _nsa_prompt_kernel(q_ref, gt_ref, kcmp_ref, vcmp_ref, ks_ref, vs_ref, kw_ref, vw_ref,
                       mmap_t_ref, exp_ref, out_ref,
                       kx_ref, vsb_ref, kwb_ref, vwb_ref, *, tq, ck, n_cmp, n_sel_blocks):
    qt = pl.program_id(1)
    n_rep = q_ref.shape[1] // GROUP_LANES
    slab = NSA_KV_HEADS * tq
    n_chunks = kx_ref.shape[0]

    @pl.when(qt == 0)
    def _():
        for c in range(n_chunks):
            cs = slice(c * ck, (c + 1) * ck)
            kx_ref[c, :GROUP_LANES, :] = ks_ref[0, :, cs].astype(BF16)
            kx_ref[c, GROUP_LANES:, :] = exp_ref[c]
            vsb_ref[c] = vs_ref[0, :, cs].astype(BF16)
            kwb_ref[c] = kw_ref[0, :, cs].astype(BF16)
            vwb_ref[c] = vw_ref[0, :, cs].astype(BF16)

    t0 = qt * tq
    qpos = t0 + lax.broadcasted_iota(jnp.int32, (slab, 1), 0) % tq
    qpos_row = t0 + lax.broadcasted_iota(jnp.int32, (1, slab), 1) % tq
    qs, own = _expand_q(q_ref[...], tq)
    gates = jax.nn.sigmoid(gt_ref[...])
    reps = range(n_rep)

    q_all = jnp.concatenate(qs, axis=0)
    kcmp, vcmp = kcmp_ref[0].astype(BF16), vcmp_ref[0].astype(BF16)
    o_cmp, p_cmp = _cmp_rows(q_all, _cmp_mask(slab, kcmp.shape[0], qpos, n_cmp), kcmp, vcmp)
    imp_rows_t = _dot_nt(mmap_t_ref[...], p_cmp)
    imp_t = imp_rows_t[:, :slab]
    for r in range(1, n_rep):
        imp_t = imp_t + imp_rows_t[:, r * slab:(r + 1) * slab]
    sel_neg = _select_blocks_t(imp_t, qpos_row, n_sel_blocks)
    qx = jnp.concatenate([q_all, jnp.concatenate([sel_neg] * n_rep, axis=0)], axis=1)

    kpos0 = lax.broadcasted_iota(jnp.int32, (1, ck), 1)
    init = _attend_init(n_rep * slab)
    c_diag = t0 // ck

    def sel_body(i, carry):
        return _attend(carry, qx, [(kx_ref[2 * i], vsb_ref[2 * i], None), (kx_ref[2 * i + 1], vsb_ref[2 * i + 1], None)])

    sel = lax.fori_loop(0, c_diag // 2, sel_body, init)
    causal = jnp.where(kpos0 + c_diag * ck <= qpos, 0.0, NEG)
    odd = jnp.maximum(c_diag - 1, 0)
    no_odd = jnp.where(c_diag % 2 == 1, 0.0, NEG)
    sel = _attend(sel, qx, [(kx_ref[odd], vsb_ref[odd], no_odd), (kx_ref[c_diag], vsb_ref[c_diag], causal)])

    n_win = -(-(WINDOW - 1) // ck) + 1
    idx = [jnp.maximum(c_diag - (n_win - 1 - j), 0) for j in range(n_win)]
    gone = [jnp.where(c_diag >= n_win - 1 - j, 0.0, NEG) for j in range(n_win)]
    scores = []
    for j in range(n_win):
        s = _dot(q_all, kwb_ref[idx[j]])
        if j == n_win - 1:
            s = _add_slab_bias(s, causal)
        elif j == 0:
            s = _add_slab_bias(s, jnp.where(kpos0 + idx[0] * ck > qpos - WINDOW, 0.0, NEG) + gone[0])
        else:
            s = s + gone[j]
        scores.append(s)
    win = _softmax_values(scores, [(vwb_ref[idx[j]], True) for j in range(n_win)])

    for r in reps:
        rs = slice(r * slab, (r + 1) * slab)
        piece = _combine_slab(o_cmp[rs], [a[rs] for a in sel], [a[rs] for a in win], gates[rs], own, tq)
        out_ref[:, r * GROUP_LANES:(r + 1) * GROUP_LANES] = piece.astype(out_ref.dtype)


def _cmp_to_sel(n_cmp, n_sel_blocks):
    r = SEL_BLOCK // CMP_STRIDE
    c = CMP_BLOCK // CMP_STRIDE
    m = np.zeros((LANES, LANES), np.float32)
    for j in range(n_sel_blocks):
        for a in range(r):
            for b in range(c):
                i = r * j + a - b
                if 0 <= i < n_cmp:
                    m[i, j] += 1.0
    return jnp.asarray(m, BF16)


def _block_expand(n_chunks, ck, n_sel_blocks):
    e = np.zeros((n_chunks, LANES, ck), np.float32)
    key = np.arange(n_chunks * ck).reshape(n_chunks, ck)
    for c in range(n_chunks):
        e[c, key[c] // SEL_BLOCK, np.arange(ck)] = 1.0
    e[:, n_sel_blocks:, :] = 0.0
    return jnp.asarray(e, BF16)


def _nsa_prompt(q, gates_rows, kcmp, vcmp, kvw_t, *, B, T, tq, ck):
    M, QW = q.shape
    n_rep = QW // GROUP_LANES
    rows = n_rep * NSA_KV_HEADS * tq
    nqt = T // tq
    n_cmp = (T - CMP_BLOCK) // CMP_STRIDE + 1
    n_sel_blocks = -(-T // SEL_BLOCK)
    assert T % ck == 0 and ck % tq == 0 and n_sel_blocks <= LANES and kcmp.shape[1] <= LANES
    assert WINDOW % ck == 0
    mmap_t = _cmp_to_sel(n_cmp, n_sel_blocks).T[:, :kcmp.shape[1]]
    expand = _block_expand(T // ck, ck, n_sel_blocks)
    const = lambda a: pl.BlockSpec(a.shape, lambda b, t: (0,) * a.ndim)
    kv_spec = lambda slot: pl.BlockSpec((1, GROUP_LANES, T), lambda b, t: (b, slot, 0))
    chunks = lambda kdim: pltpu.VMEM((T // ck, kdim, ck), BF16)
    return pl.pallas_call(
        functools.partial(_nsa_prompt_kernel, tq=tq, ck=ck, n_cmp=n_cmp, n_sel_blocks=n_sel_blocks),
        grid=(B, nqt),
        in_specs=[pl.BlockSpec((tq, QW), lambda b, t: (b * nqt + t, 0)),
                  pl.BlockSpec((rows, 3), lambda b, t: (b * nqt + t, 0)),
                  pl.BlockSpec((1,) + kcmp.shape[1:], lambda b, t: (b, 0, 0)),
                  pl.BlockSpec((1,) + vcmp.shape[1:], lambda b, t: (b, 0, 0)),
                  kv_spec(2), kv_spec(3), kv_spec(4), kv_spec(5),
                  const(mmap_t), const(expand)],
        out_specs=pl.BlockSpec((tq, QW), lambda b, t: (b * nqt + t, 0)),
        out_shape=jax.ShapeDtypeStruct((M, QW), BF16),
        scratch_shapes=[chunks(GROUP_LANES + LANES), chunks(GROUP_LANES), chunks(GROUP_LANES), chunks(GROUP_LANES)],
        compiler_params=_params(("arbitrary", "arbitrary")),
        name="nsa_prompt",
    )(q, gates_rows, kcmp, vcmp, kvw_t, kvw_t, kvw_t, kvw_t, mmap_t, expand)


def _nsa_decode_kernel(pt_ref, *refs, n_pages, page, tq, t_new, past_len, n_cmp, n_sel_blocks, wb):
    del pt_ref
    pages = refs[:n_pages]
    (q_ref, gt_ref, kvn_ref, wn_ref, cw_ref, pe_ref, w1_ref, b1_ref, w2_ref, b2_ref, perm_ref,
     mmap_ref, exp_ref, out_ref, nw_ref, newk_ref, neww_ref, bias_ref) = refs[n_pages:]
    b = pl.program_id(0)
    n_rep = q_ref.shape[2] // GROUP_LANES
    slab = NSA_KV_HEADS * tq
    reps = range(n_rep)

    @pl.when(b == 0)
    def _():
        newk_ref[...] = jnp.zeros_like(newk_ref)
        neww_ref[...] = jnp.zeros_like(neww_ref)
        for slot in range(2):
            bias_ref[slot] = _pe_bias(slot, pe_ref, w1_ref, b1_ref)

    newk_ref[:tq, :] = kvn_ref[0]
    neww_ref[:tq, :] = wn_ref[0]

    qpos = past_len + lax.broadcasted_iota(jnp.int32, (slab, 1), 0) % tq
    qs, own = _expand_q(q_ref[0], tq)
    gates = jax.nn.sigmoid(gt_ref[...])

    cmp_kv = []
    for slot in range(2):
        xs = _deinterleave_t(perm_ref[...], [pg[0, slot] for pg in pages])
        cmp_kv.append(_compress_slot(xs, slot, bias_ref[slot, 0:1, :], w1_ref, w2_ref, b2_ref).astype(BF16))

    q_all = jnp.concatenate(qs, axis=0)
    o_cmp, p_cmp = _cmp_rows(q_all, _cmp_mask(slab, cmp_kv[0].shape[0], qpos, n_cmp), cmp_kv[0], cmp_kv[1])
    imp_rows = _dot(p_cmp, mmap_ref[...])
    imp = imp_rows[:slab]
    for r in range(1, n_rep):
        imp = imp + imp_rows[r * slab:(r + 1) * slab]
    neg_all = jnp.concatenate([_select_blocks(imp, qpos, n_sel_blocks)] * n_rep, axis=0)

    kpos0 = lax.broadcasted_iota(jnp.int32, (1, page), 1)
    own_rows = jnp.where(kpos0 + past_len <= jnp.concatenate([qpos] * n_rep, axis=0), 0.0, NEG)

    scores = [_dot(q_all, pages[p][0, 2].astype(BF16)) + _dot(neg_all, exp_ref[p]) for p in range(n_pages)]
    values = [(pages[p][0, 3].astype(BF16), True) for p in range(n_pages)]
    k_new = newk_ref[:, 2 * GROUP_LANES:3 * GROUP_LANES].astype(BF16)
    scores.append(_dot_nt(q_all, k_new) + _dot(neg_all, exp_ref[n_pages]) + own_rows)
    values.append((newk_ref[:, 3 * GROUP_LANES:].astype(BF16), False))
    sel = _softmax_values(scores, values)

    kpos = past_len - wb + lax.broadcasted_iota(jnp.int32, (1, wb), 1)
    in_band = jnp.where(kpos > jnp.concatenate([qpos] * n_rep, axis=0) - WINDOW, 0.0, NEG)
    scores = [_dot(q_all, cw_ref[0, 0].astype(BF16)) + in_band,
              _dot_nt(q_all, neww_ref[:, :GROUP_LANES].astype(BF16)) + own_rows]
    values = [(cw_ref[0, 1].astype(BF16), True), (neww_ref[:, GROUP_LANES:].astype(BF16), False)]
    win = _softmax_values(scores, values)

    for r in reps:
        rs = slice(r * slab, (r + 1) * slab)
        piece = _combine_slab(o_cmp[rs], [a[rs] for a in sel], [a[rs] for a in win], gates[rs], own, tq)
        out_ref[0, :, r * GROUP_LANES:(r + 1) * GROUP_LANES] = piece

    new_t = neww_ref[...].T
    lane = lax.broadcasted_iota(jnp.int32, (GROUP_LANES, LANES), 1)
    for slot in range(2):
        shifted = pltpu.roll(cw_ref[0, slot], wb - t_new, axis=1)
        fresh = pltpu.roll(new_t[slot * GROUP_LANES:(slot + 1) * GROUP_LANES], LANES - t_new, axis=1)
        nw_ref[0, slot, :, :wb - LANES] = shifted[:, :wb - LANES]
        nw_ref[0, slot, :, wb - LANES:] = jnp.where(lane >= LANES - t_new, fresh, shifted[:, wb - LANES:])


def _nsa_decode(q8, gates_rows, kvnew8, winnew8, cache, cache_win, page_table, cw, *, past_len, t_new, win_off):
    B, tq, QW = q8.shape
    n_pages = page_table.shape[1]
    page = cache.shape[3]
    wb = cache_win.shape[3]
    n_rep = QW // GROUP_LANES
    rows = n_rep * NSA_KV_HEADS * tq
    tk = past_len + t_new
    n_cmp = (tk - CMP_BLOCK) // CMP_STRIDE + 1
    n_sel_blocks = -(-tk // SEL_BLOCK)
    nrow = past_len // CMP_STRIDE
    assert tq == SUBLANES and past_len == n_pages * page and wb % page == 0 and page == LANES
    assert n_cmp <= nrow <= LANES and (n_cmp - 1) * CMP_STRIDE + CMP_BLOCK <= past_len
    assert n_sel_blocks <= LANES and 0 < t_new < tq and wb == WINDOW
    mmap = _cmp_to_sel(n_cmp, n_sel_blocks)[:nrow]
    expand = _block_expand(n_pages + 1, page, n_sel_blocks)
    const = lambda a: pl.BlockSpec(a.shape, lambda b, pt: (0,) * a.ndim)
    page_spec = lambda p: pl.BlockSpec((1,) + cache.shape[1:], lambda b, pt: (pt[b, p], 0, 0, 0))
    per_b = lambda a: pl.BlockSpec((1,) + a.shape[1:], lambda b, pt: (b, 0, 0))
    win_block = (1,) + cache_win.shape[1:]
    consts = list(cw) + [mmap, expand]
    grid_spec = pltpu.PrefetchScalarGridSpec(
        num_scalar_prefetch=1,
        grid=(B,),
        in_specs=[page_spec(p) for p in range(n_pages)]
        + [per_b(q8), pl.BlockSpec((rows, 3), lambda b, pt: (b, 0)), per_b(kvnew8), per_b(winnew8),
           pl.BlockSpec(win_block, lambda b, pt: (win_off + b, 0, 0, 0))] + [const(a) for a in consts],
        out_specs=[per_b(q8), pl.BlockSpec(win_block, lambda b, pt: (b, 0, 0, 0))],
        scratch_shapes=[pltpu.VMEM((page, kvnew8.shape[2]), F32), pltpu.VMEM((page, winnew8.shape[2]), F32),
                        pltpu.VMEM((2, SUBLANES, cw[2].shape[1]), F32)],
    )
    return pl.pallas_call(
        functools.partial(_nsa_decode_kernel, n_pages=n_pages, page=page, tq=tq, t_new=t_new, past_len=past_len,
                          n_cmp=n_cmp, n_sel_blocks=n_sel_blocks, wb=wb),
        grid_spec=grid_spec,
        out_shape=[jax.ShapeDtypeStruct(q8.shape, F32), jax.ShapeDtypeStruct((B,) + cache_win.shape[1:], F32)],
        compiler_params=_params(("arbitrary",)),
        name="nsa_decode",
    )(page_table, *([cache] * n_pages), q8, gates_rows, kvnew8, winnew8, cache_win, *consts)


def _layer_weights(i, prm, d_ml, d_nsa):
    H, G, dk = ML_HEADS, NSA_KV_HEADS, NSA_HEAD_DIM
    R = d_nsa // (G * dk)
    kvw = G * dk
    w_in = prm["w_in"][i]
    o = 0
    c_in, o = w_in[:, o:o + d_ml], o + d_ml
    v_ml, o = w_in[:, o:o + d_ml], o + d_ml
    o_ml, o = w_in[:, o:o + d_ml], o + d_ml
    i_ml, o = w_in[:, o:o + H], o + H
    f_ml, o = w_in[:, o:o + H], o + H
    q_n, o = w_in[:, o:o + d_nsa], o + d_nsa
    kv, o = w_in[:, o:o + 6 * kvw], o + 6 * kvw
    g_n = w_in[:, o:]
    K = w_in.shape[0]
    q_perm = q_n.reshape(K, G, R, dk).transpose(0, 2, 1, 3).reshape(K, d_nsa)
    w_main = jnp.concatenate([c_in, v_ml, o_ml, q_perm, kv], axis=1).astype(BF16)
    w_kvw_t = kv.T.astype(BF16)
    small = jnp.concatenate([i_ml, f_ml, g_n], axis=1)
    w_small = jnp.pad(small, ((0, 0), (0, LANES - small.shape[1]))).astype(BF16)
    w_out = prm["w_out"][i]
    w_out_nsa = w_out[d_ml:].reshape(G, R, dk, -1).transpose(1, 0, 2, 3).reshape(d_nsa, -1)
    w_out_p = jnp.concatenate([w_out[:d_ml], w_out_nsa], axis=0).astype(BF16)
    Dh = d_ml // H
    gate_bias = jnp.pad(jnp.concatenate([prm["b_i"][i], prm["b_f"][i]]), (0, LANES - 2 * H)).reshape(1, LANES)
    w2 = prm["cmp_w2"][i]
    w2_placed = jnp.stack([jnp.stack([jnp.pad(w2[s], ((0, 0), (g * dk, (G - 1 - g) * dk))) for g in range(G)])
                           for s in range(2)]).astype(BF16)
    perm = np.zeros((LANES, LANES), np.float32)
    tok = np.arange(LANES)
    perm[(tok % CMP_STRIDE) * (LANES // CMP_STRIDE) + tok // CMP_STRIDE, tok] = 1.0
    cmp_w = (prm["cmp_pe"][i].reshape(2, -1), prm["cmp_w1"][i].astype(BF16), prm["cmp_b1"][i], w2_placed,
             jnp.tile(prm["cmp_b2"][i], (1, G)), jnp.asarray(perm, BF16))
    return dict(
        w_main=w_main, w_kvw_t=w_kvw_t, w_small=w_small, w_out=w_out_p,
        w_up=prm["w_up"][i].astype(BF16), w_down=prm["w_down"][i].astype(BF16),
        w_pl=prm["w_pl"][i].astype(BF16), w_pl_gate=prm["w_pl_gate"][i].astype(BF16),
        conv_w=prm["conv_w"][i], conv_b=prm["conv_b"][i].reshape(1, -1),
        wq=prm["w_q_ml"][i].astype(BF16),
        wkt=(jnp.swapaxes(prm["w_k_ml"][i], 1, 2) * (Dh ** -0.5)).astype(BF16),
        gate_bias=gate_bias, g_ml=prm["g_ml"][i].reshape(1, -1), cmp=cmp_w)


def _gate_rows(small, B, nqt, tq, G, R):
    g = small[:, 2 * ML_HEADS:2 * ML_HEADS + 3 * G * R].reshape(B, nqt, tq, G, R, 3)
    return g.transpose(0, 1, 4, 3, 2, 5).reshape(B * nqt * R * G * tq, 3)


def _pad_time(a, B, t, tp):
    return jnp.pad(a.reshape(B, t, -1), ((0, 0), (0, tp - t), (0, 0)))


def _layer(h, pl_e, i, prm, lw, mem, *, B, T):
    M, D = h.shape
    G, dk, H = NSA_KV_HEADS, NSA_HEAD_DIM, ML_HEADS
    d_ml = lw["conv_w"].shape[1]
    d_nsa = lw["w_out"].shape[0] - d_ml
    R = d_nsa // (G * dk)
    Dh = d_ml // H
    kvw = G * dk
    tm = min(M, 512)
    g_pre = prm["g_pre_mix"][i]
    lanes_last = lambda a, lead: a.reshape(lead + (G, dk, a.shape[-1]))

    ml_w = (lw["conv_w"], lw["conv_b"], lw["wq"], lw["wkt"], lw["gate_bias"], lw["g_ml"])
    if mem is None:
        cin, v, o, q, small = _norm_matmul(h, g_pre, lw["w_main"], (d_ml, d_ml, d_ml, d_nsa),
                                           w_small=lw["w_small"], tm=min(M, 1024), tn=512)
        kvw_t = _norm_matmul_t(h, g_pre, lw["w_kvw_t"], B=B, T=T, tm=min(T, 1024), tn=512)
        zeros = lambda *s: jnp.zeros(s, F32)
        hm, c_new, n_new, m_new = _mlstm(
            cin, v, o, small, zeros(B, SUBLANES, d_ml), zeros(B, H, Dh, Dh), zeros(B, 1, d_ml), zeros(B, 1, LANES),
            *ml_w, B=B, rows=ML_CHUNK, valid=ML_CHUNK)
        kcmp, vcmp = _compress_prompt(kvw_t, lw["cmp"], B=B, T=T)
        tq = 64
        on = _nsa_prompt(q, _gate_rows(small, B, T // tq, tq, G, R), kcmp, vcmp, kvw_t, B=B, T=T, tq=tq, ck=256)
        wlen = min(WINDOW, T)
        new_rows = lanes_last(kvw_t[:, :4 * kvw], (B, 4)).transpose(0, 4, 1, 2, 3)
        new_win = lanes_last(kvw_t[:, 4 * kvw:, T - wlen:], (B, 2)).transpose(0, 4, 1, 2, 3)
    else:
        cin, v, o, q, kv4, win2, small = _norm_matmul(
            h, g_pre, lw["w_main"], (d_ml, d_ml, d_ml, d_nsa, 4 * kvw, 2 * kvw),
            w_small=lw["w_small"], tm=tm, tn=512)
        tp = SUBLANES
        pad = lambda a: _pad_time(a, B, T, tp)
        prev = jnp.pad(mem["conv"], ((0, 0), (SUBLANES - mem["conv"].shape[1], 0), (0, 0)))
        m0 = jnp.pad(mem["m"], ((0, 0), (0, LANES - H))).reshape(B, 1, LANES)
        hm8, c_new, n_new, m_new = _mlstm(
            pad(cin).reshape(B * tp, d_ml), pad(v).reshape(B * tp, d_ml), pad(o).reshape(B * tp, d_ml),
            pad(small).reshape(B * tp, LANES), prev, mem["C"], mem["n"].reshape(B, 1, d_ml), m0,
            *ml_w, B=B, rows=tp, valid=T, c_off=i * B)
        hm = hm8.reshape(B, tp, d_ml)[:, :T].reshape(M, d_ml)
        page = mem["kv"].shape[3]
        on8, new_win_t = _nsa_decode(
            pad(q), _gate_rows(pad(small).reshape(B * tp, LANES), B, 1, tp, G, R), pad(kv4), pad(win2),
            mem["kv"], mem["win"], mem["page_table"], lw["cmp"],
            past_len=mem["page_table"].shape[1] * page, t_new=T, win_off=i * B)
        on = on8[:, :T].reshape(M, d_nsa)
        new_rows = kv4.reshape(B, T, 4, G, dk)
        new_win = lanes_last(new_win_t, (B, 2)).transpose(0, 4, 1, 2, 3)

    mixed = jnp.concatenate([hm.astype(BF16), on.astype(BF16)], axis=1)
    h = _matmul_norm_res(mixed, lw["w_out"], h, prm["g_post_mix"][i], tm=tm, tk=2048)
    u, = _norm_matmul(h, prm["g_pre_mlp"][i], lw["w_up"], (lw["w_up"].shape[1],), tm=min(M, 1024), tn=1024,
                      act="relu2", out_dtype=BF16)
    h = _matmul_norm_res(u, lw["w_down"], h, prm["g_post_mlp"][i], tm=tm, tk=2048)
    h = _ple(h, pl_e, lw["w_pl_gate"], lw["w_pl"], prm["g_pl"][i], tm=min(M, 256))

    state = (new_rows, new_win, c_new, n_new.reshape(B, H, Dh), m_new[:, 0, :H], cin.reshape(B, T, d_ml)[:, T - 3:])
    return h, state


def kernel(x_prompt, x_sample, cache_kv, cache_win, state_C, state_n, state_m, state_conv, page_table,
           p_prompt, p_sample, g_pre_mix, w_in, conv_w, conv_b, w_q_ml, w_k_ml, b_i, b_f, g_ml,
           cmp_pe, cmp_w1, cmp_b1, cmp_w2, cmp_b2, w_out, g_post_mix, g_pre_mlp, w_up, w_down,
           g_post_mlp, w_pl, g_pl, w_pl_gate):
    prm = dict(g_pre_mix=g_pre_mix, w_in=w_in, conv_w=conv_w, conv_b=conv_b, w_q_ml=w_q_ml, w_k_ml=w_k_ml,
               b_i=b_i, b_f=b_f, g_ml=g_ml, cmp_pe=cmp_pe, cmp_w1=cmp_w1, cmp_b1=cmp_b1, cmp_w2=cmp_w2,
               cmp_b2=cmp_b2, w_out=w_out, g_post_mix=g_post_mix, g_pre_mlp=g_pre_mlp, w_up=w_up,
               w_down=w_down, g_post_mlp=g_post_mlp, w_pl=w_pl, g_pl=g_pl, w_pl_gate=w_pl_gate)
    Bp, Tp, D = x_prompt.shape
    Bs, Ts, _ = x_sample.shape
    depth = w_in.shape[0]
    d_ml = conv_w.shape[2]
    d_nsa = w_out.shape[1] - d_ml
    hp = x_prompt.reshape(Bp * Tp, D)
    hs = x_sample.reshape(Bs * Ts, D)
    n_pool, page = cache_kv.shape[1:3]
    kv_t = jnp.transpose(cache_kv, (0, 1, 3, 4, 5, 2)).reshape(depth * n_pool, cache_kv.shape[3], -1, page)
    win_t = jnp.transpose(cache_win, (0, 1, 3, 4, 5, 2)).reshape(depth * Bs, cache_win.shape[3], -1, cache_win.shape[2])
    c_all = state_C.reshape((depth * Bs,) + state_C.shape[2:])
    sp, ss = [], []
    for i in range(depth):
        lw = _layer_weights(i, prm, d_ml, d_nsa)
        mem = dict(kv=kv_t, page_table=page_table + i * n_pool, win=win_t, C=c_all, n=state_n[i],
                   m=state_m[i], conv=state_conv[i])
        hp, st_p = _layer(hp, p_prompt[i].reshape(Bp * Tp, -1), i, prm, lw, None, B=Bp, T=Tp)
        hs, st_s = _layer(hs, p_sample[i].reshape(Bs * Ts, -1), i, prm, lw, mem, B=Bs, T=Ts)
        sp.append(st_p)
        ss.append(st_s)

    stk = lambda lst, j: jnp.stack([s[j] for s in lst])
    return (hp.reshape(Bp, Tp, D), hs.reshape(Bs, Ts, D), stk(sp, 0), stk(ss, 0), stk(sp, 1), stk(ss, 1),
            stk(sp, 2), stk(ss, 2), stk(sp, 3), stk(ss, 3), stk(sp, 4), stk(ss, 4), stk(sp, 5), stk(ss, 5))
```

```python
import functools

import numpy as np
import jax
import jax.numpy as jnp
from jax import lax
from jax.experimental import pallas as pl
from jax.experimental.pallas import tpu as pltpu

ML_HEADS = 4
NSA_HEAD_DIM = 64
NSA_KV_HEADS = 4
CMP_BLOCK = 32
CMP_STRIDE = 16
SEL_BLOCK = 64
N_SEL = 16
WINDOW = 512
EPS = 1e-6
NEG = -1e30
FORCE = 1e6

LANES = 128
SUBLANES = 8
VMEM_LIMIT_BYTES = 56 * 1024 * 1024

ML_CHUNK = 128
MLSTM_SEQS_PER_STEP = 1
GROUP_LANES = NSA_KV_HEADS * NSA_HEAD_DIM
QK_SCALE = NSA_HEAD_DIM ** -0.5 * 1.4426950408889634
MASK_BIG = 1e30

F32 = jnp.float32
BF16 = jnp.bfloat16


def _dot(a, b):
    return jnp.dot(a, b, preferred_element_type=F32)


def _dot_nt(a, b):
    return lax.dot_general(a, b, (((1,), (1,)), ((), ())), preferred_element_type=F32)


def _params(semantics):
    return pltpu.CompilerParams(dimension_semantics=semantics, vmem_limit_bytes=VMEM_LIMIT_BYTES)


def _rms_rows(x):
    return x * lax.rsqrt(jnp.mean(x * x, axis=-1, keepdims=True) + EPS)


def _norm_matmul_kernel(x_ref, g_ref, w_ref, *rest, seg_tiles, has_small, act):
    if has_small:
        ws_ref, rest = rest[0], rest[1:]
    n_out = len(seg_tiles) + (1 if has_small else 0)
    out_refs, xn_ref = rest[:n_out], rest[n_out]
    j = pl.program_id(1)

    @pl.when(j == 0)
    def _():
        xn = (_rms_rows(x_ref[...]) * g_ref[...]).astype(BF16)
        xn_ref[...] = xn
        if has_small:
            out_refs[-1][...] = _dot(xn, ws_ref[...])

    y = _dot(xn_ref[...], w_ref[...])
    if act == "relu2":
        y = jnp.square(jnp.maximum(y, 0.0))
    start = 0
    for k, n in enumerate(seg_tiles):
        @pl.when((j >= start) & (j < start + n))
        def _(k=k):
            out_refs[k][...] = y.astype(out_refs[k].dtype)
        start += n


def _norm_matmul(x, g, w, seg_cols, *, w_small=None, tm, tn, act=None, out_dtype=F32, layer=0):
    M, K = x.shape
    seg_tiles = tuple(c // tn for c in seg_cols)
    assert all(c % tn == 0 for c in seg_cols) and M % tm == 0
    has_small = w_small is not None
    starts = np.concatenate([[0], np.cumsum(seg_tiles)[:-1]]).tolist()

    def out_map(start, n):
        return lambda i, j: (i, jnp.clip(j - start, 0, n - 1))

    in_specs = [pl.BlockSpec((tm, K), lambda i, j: (i, 0)),
                pl.BlockSpec((1, K), lambda i, j: (0, 0)),
                pl.BlockSpec((K, tn), lambda i, j: (layer, j))]
    args = [x, g.reshape(1, K), w]
    out_specs = [pl.BlockSpec((tm, tn), out_map(s, n)) for s, n in zip(starts, seg_tiles)]
    out_shape = [jax.ShapeDtypeStruct((M, c), out_dtype) for c in seg_cols]
    if has_small:
        in_specs.append(pl.BlockSpec((K, LANES), lambda i, j: (0, 0)))
        args.append(w_small)
        out_specs.append(pl.BlockSpec((tm, LANES), lambda i, j: (i, 0)))
        out_shape.append(jax.ShapeDtypeStruct((M, LANES), F32))
    return pl.pallas_call(
        functools.partial(_norm_matmul_kernel, seg_tiles=seg_tiles, has_small=has_small, act=act),
        grid=(M // tm, sum(seg_tiles)),
        in_specs=in_specs, out_specs=out_specs, out_shape=out_shape,
        scratch_shapes=[pltpu.VMEM((tm, K), BF16)],
        compiler_params=_params(("parallel", "arbitrary")),
        name="norm_matmul",
    )(*args)


def _norm_matmul_t_kernel(x_ref, g_ref, wt_ref, *rest, seg_tiles):
    out_refs, xn_ref = rest[:-1], rest[-1]
    j = pl.program_id(1)

    @pl.when(j == 0)
    def _():
        xn_ref[...] = (_rms_rows(x_ref[...]) * g_ref[...]).astype(BF16)

    y = _dot_nt(wt_ref[...], xn_ref[...])
    start = 0
    for k, n in enumerate(seg_tiles):
        @pl.when((j >= start) & (j < start + n))
        def _(k=k):
            out_refs[k][0] = y
        start += n


def _norm_matmul_t(x, g, wt, seg_rows, *, B, T, tm, tn):
    M, K = x.shape
    seg_tiles = tuple(r // tn for r in seg_rows)
    assert T % tm == 0 and all(r % tn == 0 for r in seg_rows) and M == B * T
    per = T // tm
    starts = np.concatenate([[0], np.cumsum(seg_tiles)[:-1]]).tolist()

    def out_map(start, n):
        return lambda i, j: (i // per, jnp.clip(j - start, 0, n - 1), i % per)

    return pl.pallas_call(
        functools.partial(_norm_matmul_t_kernel, seg_tiles=seg_tiles),
        grid=(M // tm, sum(seg_tiles)),
        in_specs=[pl.BlockSpec((tm, K), lambda i, j: (i, 0)),
                  pl.BlockSpec((1, K), lambda i, j: (0, 0)),
                  pl.BlockSpec((tn, K), lambda i, j: (j, 0))],
        out_specs=[pl.BlockSpec((1, tn, tm), out_map(s, n)) for s, n in zip(starts, seg_tiles)],
        out_shape=[jax.ShapeDtypeStruct((B, r, T), F32) for r in seg_rows],
        scratch_shapes=[pltpu.VMEM((tm, K), BF16)],
        compiler_params=_params(("parallel", "arbitrary")),
        name="norm_matmul_t",
    )(x, g.reshape(1, K), wt)


def _matmul_norm_res_kernel(*refs, n_a):
    a_refs = refs[:n_a]
    w_ref, h_ref, g_ref, o_ref, acc_ref = refs[n_a:]
    k = pl.program_id(1)

    @pl.when(k == 0)
    def _():
        acc_ref[...] = jnp.zeros_like(acc_ref)

    off = 0
    for a_ref in a_refs:
        width = a_ref.shape[1]
        acc_ref[...] += _dot(a_ref[...].astype(BF16), w_ref[off:off + width, :])
        off += width

    @pl.when(k == pl.num_programs(1) - 1)
    def _():
        o_ref[...] = h_ref[...] + _rms_rows(acc_ref[...]) * g_ref[...]


def _matmul_norm_res(a_parts, w, h, g, *, tm, tk, layer=0):
    M = a_parts[0].shape[0]
    K = sum(a.shape[1] for a in a_parts)
    N = w.shape[1]
    assert M % tm == 0 and K % tk == 0 and (len(a_parts) == 1 or tk == K)
    k_steps = K // tk
    a_specs = [pl.BlockSpec((tm, tk if len(a_parts) == 1 else a.shape[1]), lambda i, k: (i, k)) for a in a_parts]
    return pl.pallas_call(
        functools.partial(_matmul_norm_res_kernel, n_a=len(a_parts)),
        grid=(M // tm, k_steps),
        in_specs=a_specs + [pl.BlockSpec((tk, N), lambda i, k: (layer * k_steps + k, 0)),
                            pl.BlockSpec((tm, N), lambda i, k: (i, 0)),
                            pl.BlockSpec((1, N), lambda i, k: (0, 0))],
        out_specs=pl.BlockSpec((tm, N), lambda i, k: (i, 0)),
        out_shape=jax.ShapeDtypeStruct((M, N), F32),
        scratch_shapes=[pltpu.VMEM((tm, N), F32)],
        compiler_params=_params(("parallel", "arbitrary")),
        name="matmul_norm_res",
    )(*a_parts, w, h, g.reshape(1, N))


def _ple_kernel(h_ref, p_ref, wg_ref, wp_ref, g_ref, o_ref):
    h = h_ref[...]
    gate = jax.nn.sigmoid(_dot(_rms_rows(h).astype(BF16), wg_ref[...]))
    e = _dot(p_ref[...].astype(BF16), wp_ref[...])
    o_ref[...] = h + gate * (_rms_rows(e) * g_ref[...])


def _ple(h, p, wg, wp, g, *, tm, layer=0):
    M, N = h.shape
    P = p.shape[1]
    assert M % tm == 0
    return pl.pallas_call(
        _ple_kernel,
        grid=(M // tm,),
        in_specs=[pl.BlockSpec((tm, N), lambda i: (i, 0)),
                  pl.BlockSpec((tm, P), lambda i: (layer * (M // tm) + i, 0)),
                  pl.BlockSpec((N, N), lambda i: (0, 0)),
                  pl.BlockSpec((P, N), lambda i: (0, 0)),
                  pl.BlockSpec((1, N), lambda i: (0, 0))],
        out_specs=pl.BlockSpec((tm, N), lambda i: (i, 0)),
        out_shape=jax.ShapeDtypeStruct((M, N), F32),
        compiler_params=_params(("parallel",)),
        name="ple",
    )(h, p, wg, wp, g.reshape(1, N))


def _shift_rows(x, prev8, s):
    rows = x.shape[0]
    xs = pltpu.roll(x, s, axis=0)
    rid = lax.broadcasted_iota(jnp.int32, (SUBLANES, x.shape[1]), 0)
    head = jnp.where(rid < s, pltpu.roll(prev8, s, axis=0), xs[:SUBLANES])
    if rows == SUBLANES:
        return head
    return jnp.concatenate([head, xs[SUBLANES:]], axis=0)


def _mlstm_kernel(*refs, nb, **static):
    n_seq, n_shared = 8, 7
    seq_in, shared = refs[:n_seq], refs[n_seq:n_seq + n_shared]
    rest = refs[n_seq + n_shared:]
    per_seq = [[r.at[j] for r in seq_in] + list(shared) + [r.at[j] for r in rest] for j in range(nb)]
    for args in per_seq:
        _mlstm_load_state(*args, **static)
    m_rows = [_mlstm_seq(*args, **static) for args in per_seq]
    for args, m_row in zip(per_seq, m_rows):
        _mlstm_store_state(*args, m_row, **static)


def _mlstm_load_state(cin_ref, v_ref, o_ref, sm_ref, prev_ref, c0_ref, n0_ref, m0_ref,
                      cw_ref, cb_ref, wq_ref, wkt_ref, gb_ref, gml_ref, ltri_ref,
                      hm_ref, cout_ref, nout_ref, mout_ref,
                      carry_ref, cext_ref, m_ref, chpad_ref, vpad_ref, gpad_ref,
                      *, L, rows, valid, H, Dh):
    @pl.when(pl.program_id(1) == 0)
    def _():
        carry_ref[...] = prev_ref[...]
        m_ref[...] = m0_ref[...]
        for h in range(H):
            cext_ref[h, :, :Dh] = c0_ref[h]
            n_row = n0_ref[:, h * Dh:(h + 1) * Dh]
            cext_ref[h, :, Dh:] = jnp.broadcast_to(n_row, (LANES, Dh)).T

    if rows < L:
        @pl.when((pl.program_id(0) == 0) & (pl.program_id(1) == 0))
        def _():
            chpad_ref[...] = jnp.zeros_like(chpad_ref)
            vpad_ref[...] = jnp.zeros_like(vpad_ref)
            gpad_ref[...] = jnp.zeros_like(gpad_ref)


def _mlstm_store_state(cin_ref, v_ref, o_ref, sm_ref, prev_ref, c0_ref, n0_ref, m0_ref,
                       cw_ref, cb_ref, wq_ref, wkt_ref, gb_ref, gml_ref, ltri_ref,
                       hm_ref, cout_ref, nout_ref, mout_ref,
                       carry_ref, cext_ref, m_ref, chpad_ref, vpad_ref, gpad_ref, m_new_row,
                       *, L, rows, valid, H, Dh):
    @pl.when(pl.program_id(1) == pl.num_programs(1) - 1)
    def _():
        mout_ref[...] = m_new_row
        for h in range(H):
            cout_ref[h] = cext_ref[h, :, :Dh]
            nout_ref[:, h * Dh:(h + 1) * Dh] = cext_ref[h, :, Dh:].T[0:1, :]


def _mlstm_seq(cin_ref, v_ref, o_ref, sm_ref, prev_ref, c0_ref, n0_ref, m0_ref,
               cw_ref, cb_ref, wq_ref, wkt_ref, gb_ref, gml_ref, ltri_ref,
               hm_ref, cout_ref, nout_ref, mout_ref,
               carry_ref, cext_ref, m_ref, chpad_ref, vpad_ref, gpad_ref,
               *, L, rows, valid, H, Dh):
    x = cin_ref[...]
    prev8 = carry_ref[...]
    conv = cb_ref[...] + cw_ref[3:4, :] * x
    for s in (1, 2, 3):
        conv = conv + cw_ref[3 - s:4 - s, :] * _shift_rows(x, prev8, s)
    carry_ref[...] = x[rows - SUBLANES:, :]
    ch = conv * jax.nn.sigmoid(conv)

    if rows < L:
        chpad_ref[:rows, :] = ch
        vpad_ref[:rows, :] = v_ref[...]
        gpad_ref[:rows, :] = sm_ref[...]
        ch, v_all, sm = chpad_ref[...], vpad_ref[...], gpad_ref[...]
    else:
        v_all, sm = v_ref[...], sm_ref[...]
    ch = ch.astype(BF16)

    lane = lax.broadcasted_iota(jnp.int32, (L, LANES), 1)
    rid = lax.broadcasted_iota(jnp.int32, (L, LANES), 0)
    pre = sm + gb_ref[...]
    gates = jnp.where(lane < H, pre, jax.nn.log_sigmoid(pre))
    gates = jnp.where(rid < valid, gates, jnp.where(lane < H, NEG, 0.0))
    hi = gates.astype(BF16)
    r1 = gates - hi.astype(F32)
    mid = r1.astype(BF16)
    lo = (r1 - mid.astype(F32)).astype(BF16)
    ltri = ltri_ref[...]
    bc_all = _dot(ltri, hi) + _dot(ltri, mid) + _dot(ltri, lo)
    colform = jnp.where(lane < H, gates, bc_all)
    rowform = colform.T

    r_i = lax.broadcasted_iota(jnp.int32, (L, L), 0)
    c_i = lax.broadcasted_iota(jnp.int32, (L, L), 1)
    causal = c_i <= r_i
    lane1 = lax.broadcasted_iota(jnp.int32, (1, LANES), 1)
    m_row = m_ref[...]
    m_new_row = m_row
    ones = jnp.ones((L, LANES), F32)

    for h in range(H):
        hs = slice(h * Dh, (h + 1) * Dh)
        ch_h = ch[:, hs]
        q_h = _dot(ch_h, wq_ref[h]).astype(BF16)
        kt_h = _dot_nt(wkt_ref[h], ch_h).astype(BF16)
        v_h = v_all[:, hs]
        li_col = colform[:, h:h + 1]
        bc_col = colform[:, H + h:H + h + 1]
        li_row = rowform[h:h + 1, :]
        bc_row = rowform[H + h:H + h + 1, :]
        m_h = m_row[:, h:h + 1]

        logd = jnp.where(causal, bc_col - bc_row + li_row, NEG)
        inter = m_h + bc_col
        mt = jnp.maximum(jnp.max(logd, axis=-1, keepdims=True), inter)
        dm = jnp.exp(logd - mt)
        a_int = jnp.exp(inter - mt)
        s = _dot(q_h, kt_h) * dm
        v_ext = jnp.concatenate([v_h, ones], axis=1)
        cext = cext_ref[h]
        tot = _dot(s.astype(BF16), v_ext.astype(BF16)) + a_int * _dot(q_h, cext.astype(BF16))
        num = tot[:, :Dh]
        den = tot[:, Dh:Dh + 1]
        hh = num / jnp.maximum(jnp.abs(den), jnp.exp(-mt))

        m_last = mt[L - 1:L, :]
        bc_last = bc_col[L - 1:L, :]
        w_col = jnp.exp(bc_last - bc_col + li_col - m_last)
        dec = jnp.exp(m_h + bc_last - m_last)
        cext_ref[h] = dec * cext + _dot(kt_h, (w_col * v_ext).astype(BF16))
        m_new_row = jnp.where(lane1 == h, m_last, m_new_row)

        y = _rms_rows(hh) * gml_ref[:, hs]
        y = y[:rows] * jax.nn.sigmoid(o_ref[:, hs])
        hm_ref[:, hs] = y

    m_ref[...] = m_new_row
    return m_new_row


def _mlstm(cin, v, o, small, prev, c0, n0, m0, cw, cb, wq, wkt, gb, gml, *, B, rows, valid, c_off=0):
    L = ML_CHUNK
    H = ML_HEADS
    D = cin.shape[1]
    Dh = D // H
    nc = cin.shape[0] // (B * rows)
    nb = MLSTM_SEQS_PER_STEP
    assert rows == L or (nc == 1 and rows == SUBLANES)
    assert B % nb == 0 and c_off % nb == 0
    ltri = jnp.asarray(np.tril(np.ones((L, L), np.float32)), BF16)
    seq3 = lambda a: a.reshape(B, nc * rows, a.shape[1])
    tok = lambda g, c: (g, c, 0)
    per_g3 = lambda g, c: (g, 0, 0)
    per_g4 = lambda g, c: (g, 0, 0, 0)
    const2 = lambda g, c: (0, 0)
    const3 = lambda g, c: (0, 0, 0)
    hm, cout, nout, mout = pl.pallas_call(
        functools.partial(_mlstm_kernel, nb=nb, L=L, rows=rows, valid=valid, H=H, Dh=Dh),
        grid=(B // nb, nc),
        in_specs=[pl.BlockSpec((nb, rows, D), tok), pl.BlockSpec((nb, rows, D), tok), pl.BlockSpec((nb, rows, D), tok),
                  pl.BlockSpec((nb, rows, LANES), tok),
                  pl.BlockSpec((nb, SUBLANES, D), per_g3),
                  pl.BlockSpec((nb, H, Dh, Dh), lambda g, c: (c_off // nb + g, 0, 0, 0)),
                  pl.BlockSpec((nb, 1, D), per_g3),
                  pl.BlockSpec((nb, 1, LANES), per_g3),
                  pl.BlockSpec((4, D), const2), pl.BlockSpec((1, D), const2),
                  pl.BlockSpec((H, Dh, Dh), const3), pl.BlockSpec((H, Dh, Dh), const3),
                  pl.BlockSpec((1, LANES), const2), pl.BlockSpec((1, D), const2),
                  pl.BlockSpec((L, L), const2)],
        out_specs=[pl.BlockSpec((nb, rows, D), tok),
                   pl.BlockSpec((nb, H, Dh, Dh), per_g4),
                   pl.BlockSpec((nb, 1, D), per_g3),
                   pl.BlockSpec((nb, 1, LANES), per_g3)],
        out_shape=[jax.ShapeDtypeStruct((B, nc * rows, D), F32),
                   jax.ShapeDtypeStruct((B, H, Dh, Dh), F32),
                   jax.ShapeDtypeStruct((B, 1, D), F32),
                   jax.ShapeDtypeStruct((B, 1, LANES), F32)],
        scratch_shapes=[pltpu.VMEM((nb, SUBLANES, D), F32),
                        pltpu.VMEM((nb, H, Dh, Dh + LANES), F32),
                        pltpu.VMEM((nb, 1, LANES), F32),
                        pltpu.VMEM((nb, L, D), F32), pltpu.VMEM((nb, L, D), F32), pltpu.VMEM((nb, L, LANES), F32)],
        compiler_params=_params(("arbitrary", "arbitrary")),
        name="mlstm",
    )(seq3(cin), seq3(v), seq3(o), seq3(small), prev, c0, n0, m0, cw, cb, wq, wkt, gb, gml, ltri)
    return hm.reshape(cin.shape), cout, nout, mout


def _deinterleave_t(perm, groups):
    per = LANES // CMP_STRIDE
    ys = [_dot_nt(perm, x_t.astype(BF16)) for x_t in groups]
    return [jnp.concatenate([y[l * per:(l + 1) * per] for y in ys], axis=0) for l in range(CMP_STRIDE)]


def _pe_bias(slot, pe_ref, w1_ref, b1_ref):
    pe = jnp.broadcast_to(pe_ref[slot:slot + 1, :], (SUBLANES, pe_ref.shape[1])).astype(BF16)
    return _dot(pe, w1_ref[slot]) + b1_ref[slot:slot + 1, :]


def _compress_slot(xs, slot, bias, w1_ref, w2_ref, b2_ref):
    dk = NSA_HEAD_DIM
    half = CMP_BLOCK // 2
    nrow = xs[0].shape[0]
    pack = GROUP_LANES // dk
    G = NSA_KV_HEADS
    first = None
    second = None
    for l0 in range(0, half, pack):
        xg = jnp.concatenate(
            [jnp.concatenate([xs[l0 + j][:, g * dk:(g + 1) * dk] for j in range(pack)], axis=1) for g in range(G)],
            axis=0).astype(BF16)
        a = _dot(xg, w1_ref[slot, l0 * dk:(l0 + pack) * dk, :])
        bb = _dot(xg, w1_ref[slot, (half + l0) * dk:(half + l0 + pack) * dk, :])
        first = a if first is None else first + a
        second = bb if second is None else second + bb
    hid = jax.nn.gelu(first + pltpu.roll(second, G * nrow - 1, axis=0) + bias).astype(BF16)
    out = None
    for g in range(G):
        og = _dot(hid[g * nrow:(g + 1) * nrow], w2_ref[slot, g])
        out = og if out is None else out + og
    return out + b2_ref[slot:slot + 1, :]


def _expand_q(q, tq):
    slab = NSA_KV_HEADS * tq
    lane_g = lax.broadcasted_iota(jnp.int32, (slab, GROUP_LANES), 1) // NSA_HEAD_DIM
    row_g = lax.broadcasted_iota(jnp.int32, (slab, GROUP_LANES), 0) // tq
    own = lane_g == row_g
    n_rep = q.shape[1] // GROUP_LANES
    slabs = []
    for r in range(n_rep):
        qr = q[:, r * GROUP_LANES:(r + 1) * GROUP_LANES] * QK_SCALE
        slabs.append(jnp.where(own, jnp.concatenate([qr] * NSA_KV_HEADS, axis=0), 0.0).astype(BF16))
    return slabs, own


def _attend(carry, q, chunks):
    m, l, acc = carry
    scores = []
    for k, _, bias in chunks:
        s = _dot(q, k)
        if bias is not None:
            s = _add_slab_bias(s, bias) if jnp.ndim(bias) == 2 else s + bias
        scores.append(s)
    mx = scores[0]
    for s in scores[1:]:
        mx = jnp.maximum(mx, s)
    m_new = jnp.maximum(m, jnp.max(mx, axis=1, keepdims=True))
    alpha = jnp.exp2(m - m_new)
    ps, pv = None, None
    for s, (_, v, _) in zip(scores, chunks):
        p = jnp.exp2(s - m_new)
        ps = p if ps is None else ps + p
        d = _dot_nt(p.astype(BF16), v)
        pv = d if pv is None else pv + d
    return m_new, alpha * l + jnp.sum(ps, axis=1, keepdims=True), alpha * acc + pv


def _add_slab_bias(s, bias):
    rows, ck = s.shape
    return (s.reshape(rows // bias.shape[0], bias.shape[0], ck) + bias[None]).reshape(rows, ck)


def _softmax_values(scores, values):
    mx = scores[0]
    for s in scores[1:]:
        mx = jnp.maximum(mx, s) if s.shape == mx.shape else mx
    m = mx.max(axis=1, keepdims=True)
    for s in scores[1:]:
        if s.shape != mx.shape:
            m = jnp.maximum(m, s.max(axis=1, keepdims=True))
    l, acc, ps = None, None, None
    for s, (v, v_t) in zip(scores, values):
        p = jnp.exp2(s - m)
        if p.shape == mx.shape:
            ps = p if ps is None else ps + p
        else:
            ls = p.sum(axis=1, keepdims=True)
            l = ls if l is None else l + ls
        pv = _dot_nt(p.astype(BF16), v) if v_t else _dot(p.astype(BF16), v)
        acc = pv if acc is None else acc + pv
    ls = ps.sum(axis=1, keepdims=True)
    return m, (ls if l is None else l + ls), acc


def _attend_init(rows):
    return (jnp.full((rows, 1), NEG, F32), jnp.zeros((rows, 1), F32), jnp.zeros((rows, GROUP_LANES), F32))


def _cmp_rows(q, mask, kcmp, vcmp):
    rows, slab = q.shape[0], mask.shape[0]
    s = _dot_nt(q, kcmp).reshape(rows // slab, slab, mask.shape[1])
    s = jnp.where(mask[None], s, NEG)
    e = jnp.where(mask[None], jnp.exp2(s - jnp.max(s, axis=2, keepdims=True)), 0.0)
    p = e * (1.0 / jnp.maximum(jnp.sum(e, axis=2, keepdims=True), 1e-30))
    p = p.reshape(rows, mask.shape[1]).astype(BF16)
    return _dot(p, vcmp), p


def _cmp_mask(rows, n_rows_cmp, qpos_col, n_cmp):
    lane = lax.broadcasted_iota(jnp.int32, (rows, n_rows_cmp), 1)
    return (lane * CMP_STRIDE + (CMP_BLOCK - 1) <= qpos_col) & (lane < n_cmp)


def _block_scores(imp, blk, cur, n_sel_blocks):
    forced = (blk == 0) | (blk == cur) | (blk == cur - 1)
    score = jnp.where(blk <= cur, jnp.where(forced, FORCE, imp), -1.0)
    return jnp.where(blk < n_sel_blocks, score, -2.0)


def _select_blocks(imp, qpos_col, n_sel_blocks):
    rows = imp.shape[0]
    blk = lax.broadcasted_iota(jnp.int32, (rows, LANES), 1)
    score = _block_scores(imp, blk, qpos_col // SEL_BLOCK, n_sel_blocks)
    rank = jnp.zeros((rows, LANES), F32)
    for j in range(n_sel_blocks):
        col = score[:, j:j + 1]
        ge = jnp.where(col >= score, 1.0, 0.0)
        gt = jnp.where(col > score, 1.0, 0.0)
        rank = rank + jnp.where(blk > j, ge, gt)
    chosen = jnp.where(rank < float(min(N_SEL, n_sel_blocks)), score, -1.0) >= 0.0
    return jnp.where(chosen, 0.0, -MASK_BIG).astype(BF16)


def _select_blocks_t(imp_t, qpos_row, n_sel_blocks):
    nb = -(-n_sel_blocks // SUBLANES) * SUBLANES
    cols = imp_t.shape[1]
    blk = lax.broadcasted_iota(jnp.int32, (nb, cols), 0)
    score = _block_scores(imp_t[:nb], blk, qpos_row // SEL_BLOCK, n_sel_blocks)
    rank = jnp.zeros((nb, cols), F32)
    for j in range(n_sel_blocks):
        row = score[j:j + 1, :]
        ge = jnp.where(row >= score, 1.0, 0.0)
        gt = jnp.where(row > score, 1.0, 0.0)
        rank = rank + jnp.where(blk > j, ge, gt)
    chosen = jnp.where(rank < float(min(N_SEL, n_sel_blocks)), score, -1.0) >= 0.0
    neg_t = jnp.where(chosen, 0.0, -MASK_BIG)
    neg_t = jnp.concatenate([neg_t, jnp.zeros((LANES - nb, cols), F32)], axis=0)
    return neg_t.T.astype(BF16)


def _gate_columns(sg, r, n_rep, tq):
    lane = lambda g, c: 2 * ML_HEADS + (g * n_rep + r) * 3 + c
    return [jnp.concatenate([sg[:, lane(g, c):lane(g, c) + 1] for g in range(NSA_KV_HEADS)], axis=0)
            for c in range(3)]


def _combine_slab(o_cmp, sel, win, gates, own, tq):
    comb = gates[0] * o_cmp
    for col, (_, l, acc) in ((1, sel), (2, win)):
        comb = comb + (gates[col] * (1.0 / jnp.maximum(l, 1e-30))) * acc
    kept = jnp.where(own, comb, 0.0)
    piece = kept[:tq]
    for g in range(1, NSA_KV_HEADS):
        piece = piece + kept[g * tq:(g + 1) * tq]
    return piece


def _compress_prompt_kernel(kc_ref, vc_ref, pe_ref, w1_ref, b1_ref, w2_ref, b2_ref, perm_ref, ko_ref, vo_ref):
    T = kc_ref.shape[2]
    for slot, (src, dst) in enumerate(((kc_ref, ko_ref), (vc_ref, vo_ref))):
        bias = _pe_bias(slot, pe_ref, w1_ref, b1_ref)[0:1]
        xs = _deinterleave_t(perm_ref[...], [src[0, :, t * LANES:(t + 1) * LANES] for t in range(T // LANES)])
        dst[0] = _compress_slot(xs, slot, bias, w1_ref, w2_ref, b2_ref)


def _compress_prompt(kvw_t, cw, *, B, T):
    nrow = T // CMP_STRIDE
    assert (T - CMP_BLOCK) // CMP_STRIDE + 1 <= nrow and T % LANES == 0
    full = lambda a: pl.BlockSpec(a.shape, lambda b: (0,) * a.ndim)
    return pl.pallas_call(
        _compress_prompt_kernel,
        grid=(B,),
        in_specs=[pl.BlockSpec((1, GROUP_LANES, T), lambda b: (b, 0, 0)),
                  pl.BlockSpec((1, GROUP_LANES, T), lambda b: (b, 1, 0))] + [full(a) for a in cw],
        out_specs=[pl.BlockSpec((1, nrow, GROUP_LANES), lambda b: (b, 0, 0))] * 2,
        out_shape=[jax.ShapeDtypeStruct((B, nrow, GROUP_LANES), F32)] * 2,
        compiler_params=_params(("parallel",)),
        name="compress_prompt",
    )(kvw_t, kvw_t, *cw)


def _nsa_prompt_kernel(q_ref, sm_ref, kcmp_ref, vcmp_ref, ks_ref, vs_ref, kw_ref, vw_ref,
                       mmap_t_ref, exp_ref, out_ref,
                       kx_ref, vsb_ref, kwb_ref, vwb_ref, *, tq, ck, n_cmp, n_sel_blocks):
    qt = pl.program_id(1)
    n_rep = q_ref.shape[1] // GROUP_LANES
    slab = NSA_KV_HEADS * tq
    n_chunks = kx_ref.shape[0]

    @pl.when(qt == 0)
    def _():
        for c in range(n_chunks):
            cs = slice(c * ck, (c + 1) * ck)
            kx_ref[c, :GROUP_LANES, :] = ks_ref[0, :, cs].astype(BF16)
            kx_ref[c, GROUP_LANES:, :] = exp_ref[c]
            vsb_ref[c] = vs_ref[0, :, cs].astype(BF16)
            kwb_ref[c] = kw_ref[0, :, cs].astype(BF16)
            vwb_ref[c] = vw_ref[0, :, cs].astype(BF16)

    t0 = qt * tq
    qpos = t0 + lax.broadcasted_iota(jnp.int32, (slab, 1), 0) % tq
    qpos_row = t0 + lax.broadcasted_iota(jnp.int32, (1, slab), 1) % tq
    qs, own = _expand_q(q_ref[...], tq)
    sg = jax.nn.sigmoid(sm_ref[...])
    reps = range(n_rep)

    q_all = jnp.concatenate(qs, axis=0)
    kcmp, vcmp = kcmp_ref[0].astype(BF16), vcmp_ref[0].astype(BF16)
    o_cmp, p_cmp = _cmp_rows(q_all, _cmp_mask(slab, kcmp.shape[0], qpos, n_cmp), kcmp, vcmp)
    imp_rows_t = _dot_nt(mmap_t_ref[...], p_cmp)
    imp_t = imp_rows_t[:, :slab]
    for r in range(1, n_rep):
        imp_t = imp_t + imp_rows_t[:, r * slab:(r + 1) * slab]
    sel_neg = _select_blocks_t(imp_t, qpos_row, n_sel_blocks)
    qx = jnp.concatenate([q_all, jnp.concatenate([sel_neg] * n_rep, axis=0)], axis=1)

    kpos0 = lax.broadcasted_iota(jnp.int32, (1, ck), 1)
    init = _attend_init(n_rep * slab)
    c_diag = t0 // ck

    def sel_body(i, carry):
        return _attend(carry, qx, [(kx_ref[2 * i], vsb_ref[2 * i], None), (kx_ref[2 * i + 1], vsb_ref[2 * i + 1], None)])

    sel = lax.fori_loop(0, c_diag // 2, sel_body, init)
    causal = jnp.where(kpos0 + c_diag * ck <= qpos, 0.0, NEG)
    odd = jnp.maximum(c_diag - 1, 0)
    no_odd = jnp.where(c_diag % 2 == 1, 0.0, NEG)
    sel = _attend(sel, qx, [(kx_ref[odd], vsb_ref[odd], no_odd), (kx_ref[c_diag], vsb_ref[c_diag], causal)])

    n_win = -(-(WINDOW - 1) // ck) + 1
    idx = [jnp.maximum(c_diag - (n_win - 1 - j), 0) for j in range(n_win)]
    gone = [jnp.where(c_diag >= n_win - 1 - j, 0.0, NEG) for j in range(n_win)]
    scores = []
    for j in range(n_win):
        s = _dot(q_all, kwb_ref[idx[j]])
        if j == n_win - 1:
            s = _add_slab_bias(s, causal)
        elif j == 0:
            s = _add_slab_bias(s, jnp.where(kpos0 + idx[0] * ck > qpos - WINDOW, 0.0, NEG) + gone[0])
        else:
            s = s + gone[j]
        scores.append(s)
    win = _softmax_values(scores, [(vwb_ref[idx[j]], True) for j in range(n_win)])

    for r in reps:
        rs = slice(r * slab, (r + 1) * slab)
        piece = _combine_slab(o_cmp[rs], [a[rs] for a in sel], [a[rs] for a in win],
                              _gate_columns(sg, r, n_rep, tq), own, tq)
        out_ref[:, r * GROUP_LANES:(r + 1) * GROUP_LANES] = piece.astype(out_ref.dtype)


def _cmp_to_sel(n_cmp, n_sel_blocks):
    r = SEL_BLOCK // CMP_STRIDE
    c = CMP_BLOCK // CMP_STRIDE
    m = np.zeros((LANES, LANES), np.float32)
    for j in range(n_sel_blocks):
        for a in range(r):
            for b in range(c):
                i = r * j + a - b
                if 0 <= i < n_cmp:
                    m[i, j] += 1.0
    return jnp.asarray(m, BF16)


def _block_expand(n_chunks, ck, n_sel_blocks):
    e = np.zeros((n_chunks, LANES, ck), np.float32)
    key = np.arange(n_chunks * ck).reshape(n_chunks, ck)
    for c in range(n_chunks):
        e[c, key[c] // SEL_BLOCK, np.arange(ck)] = 1.0
    e[:, n_sel_blocks:, :] = 0.0
    return jnp.asarray(e, BF16)


def _nsa_prompt(q, small, kcmp, vcmp, kv_t, win_t, *, B, T, tq, ck):
    M, QW = q.shape
    n_rep = QW // GROUP_LANES
    nqt = T // tq
    n_cmp = (T - CMP_BLOCK) // CMP_STRIDE + 1
    n_sel_blocks = -(-T // SEL_BLOCK)
    assert T % ck == 0 and ck % tq == 0 and n_sel_blocks <= LANES and kcmp.shape[1] <= LANES
    assert WINDOW % ck == 0
    mmap_t = _cmp_to_sel(n_cmp, n_sel_blocks).T[:, :kcmp.shape[1]]
    expand = _block_expand(T // ck, ck, n_sel_blocks)
    const = lambda a: pl.BlockSpec(a.shape, lambda b, t: (0,) * a.ndim)
    kv_spec = lambda slot: pl.BlockSpec((1, GROUP_LANES, T), lambda b, t: (b, slot, 0))
    chunks = lambda kdim: pltpu.VMEM((T // ck, kdim, ck), BF16)
    return pl.pallas_call(
        functools.partial(_nsa_prompt_kernel, tq=tq, ck=ck, n_cmp=n_cmp, n_sel_blocks=n_sel_blocks),
        grid=(B, nqt),
        in_specs=[pl.BlockSpec((tq, QW), lambda b, t: (b * nqt + t, 0)),
                  pl.BlockSpec((tq, LANES), lambda b, t: (b * nqt + t, 0)),
                  pl.BlockSpec((1,) + kcmp.shape[1:], lambda b, t: (b, 0, 0)),
                  pl.BlockSpec((1,) + vcmp.shape[1:], lambda b, t: (b, 0, 0)),
                  kv_spec(2), kv_spec(3), kv_spec(0), kv_spec(1),
                  const(mmap_t), const(expand)],
        out_specs=pl.BlockSpec((tq, QW), lambda b, t: (b * nqt + t, 0)),
        out_shape=jax.ShapeDtypeStruct((M, QW), BF16),
        scratch_shapes=[chunks(GROUP_LANES + LANES), chunks(GROUP_LANES), chunks(GROUP_LANES), chunks(GROUP_LANES)],
        compiler_params=_params(("arbitrary", "arbitrary")),
        name="nsa_prompt",
    )(q, small, kcmp, vcmp, kv_t, kv_t, win_t, win_t, mmap_t, expand)


def _nsa_decode_kernel(pt_ref, *refs, n_pages, page, tq, t_new, past_len, n_cmp, n_sel_blocks, wb):
    del pt_ref
    pages = refs[:n_pages]
    (q_ref, sm_ref, kvn_ref, wn_ref, cw_ref, pe_ref, w1_ref, b1_ref, w2_ref, b2_ref, perm_ref,
     mmap_ref, exp_ref, out_ref, nw_ref, newk_ref, neww_ref, bias_ref) = refs[n_pages:]
    b = pl.program_id(0)
    n_rep = q_ref.shape[2] // GROUP_LANES
    slab = NSA_KV_HEADS * tq
    reps = range(n_rep)

    @pl.when(b == 0)
    def _():
        newk_ref[...] = jnp.zeros_like(newk_ref)
        neww_ref[...] = jnp.zeros_like(neww_ref)
        for slot in range(2):
            bias_ref[slot] = _pe_bias(slot, pe_ref, w1_ref, b1_ref)

    newk_ref[:tq, :] = kvn_ref[0]
    neww_ref[:tq, :] = wn_ref[0]

    qpos = past_len + lax.broadcasted_iota(jnp.int32, (slab, 1), 0) % tq
    qs, own = _expand_q(q_ref[0], tq)
    sg = jax.nn.sigmoid(sm_ref[0])

    cmp_kv = []
    for slot in range(2):
        xs = _deinterleave_t(perm_ref[...], [pg[0, slot] for pg in pages])
        cmp_kv.append(_compress_slot(xs, slot, bias_ref[slot, 0:1, :], w1_ref, w2_ref, b2_ref).astype(BF16))

    q_all = jnp.concatenate(qs, axis=0)
    o_cmp, p_cmp = _cmp_rows(q_all, _cmp_mask(slab, cmp_kv[0].shape[0], qpos, n_cmp), cmp_kv[0], cmp_kv[1])
    imp_rows = _dot(p_cmp, mmap_ref[...])
    imp = imp_rows[:slab]
    for r in range(1, n_rep):
        imp = imp + imp_rows[r * slab:(r + 1) * slab]
    neg_all = jnp.concatenate([_select_blocks(imp, qpos, n_sel_blocks)] * n_rep, axis=0)

    kpos0 = lax.broadcasted_iota(jnp.int32, (1, page), 1)
    own_rows = jnp.where(kpos0 + past_len <= jnp.concatenate([qpos] * n_rep, axis=0), 0.0, NEG)

    scores = [_dot(q_all, pages[p][0, 2].astype(BF16)) + _dot(neg_all, exp_ref[p]) for p in range(n_pages)]
    values = [(pages[p][0, 3].astype(BF16), True) for p in range(n_pages)]
    k_new = newk_ref[:, 2 * GROUP_LANES:3 * GROUP_LANES].astype(BF16)
    scores.append(_dot_nt(q_all, k_new) + _dot(neg_all, exp_ref[n_pages]) + own_rows)
    values.append((newk_ref[:, 3 * GROUP_LANES:].astype(BF16), False))
    sel = _softmax_values(scores, values)

    kpos = past_len - wb + lax.broadcasted_iota(jnp.int32, (1, wb), 1)
    in_band = jnp.where(kpos > jnp.concatenate([qpos] * n_rep, axis=0) - WINDOW, 0.0, NEG)
    scores = [_dot(q_all, cw_ref[0, 0].astype(BF16)) + in_band,
              _dot_nt(q_all, neww_ref[:, :GROUP_LANES].astype(BF16)) + own_rows]
    values = [(cw_ref[0, 1].astype(BF16), True), (neww_ref[:, GROUP_LANES:].astype(BF16), False)]
    win = _softmax_values(scores, values)

    for r in reps:
        rs = slice(r * slab, (r + 1) * slab)
        piece = _combine_slab(o_cmp[rs], [a[rs] for a in sel], [a[rs] for a in win],
                              _gate_columns(sg, r, n_rep, tq), own, tq)
        out_ref[0, :, r * GROUP_LANES:(r + 1) * GROUP_LANES] = piece

    new_t = neww_ref[...].T
    lane = lax.broadcasted_iota(jnp.int32, (GROUP_LANES, LANES), 1)
    for slot in range(2):
        shifted = pltpu.roll(cw_ref[0, slot], wb - t_new, axis=1)
        fresh = pltpu.roll(new_t[slot * GROUP_LANES:(slot + 1) * GROUP_LANES], LANES - t_new, axis=1)
        nw_ref[0, slot, :, :wb - LANES] = shifted[:, :wb - LANES]
        nw_ref[0, slot, :, wb - LANES:] = jnp.where(lane >= LANES - t_new, fresh, shifted[:, wb - LANES:])


def _nsa_decode(q8, small8, kvnew8, winnew8, cache, cache_win, page_table, cw, *, past_len, t_new, win_off):
    B, tq, QW = q8.shape
    n_pages = page_table.shape[1]
    page = cache.shape[3]
    wb = cache_win.shape[3]
    tk = past_len + t_new
    n_cmp = (tk - CMP_BLOCK) // CMP_STRIDE + 1
    n_sel_blocks = -(-tk // SEL_BLOCK)
    nrow = past_len // CMP_STRIDE
    assert tq == SUBLANES and past_len == n_pages * page and wb % page == 0 and page == LANES
    assert n_cmp <= nrow <= LANES and (n_cmp - 1) * CMP_STRIDE + CMP_BLOCK <= past_len
    assert n_sel_blocks <= LANES and 0 < t_new < tq and wb == WINDOW
    mmap = _cmp_to_sel(n_cmp, n_sel_blocks)[:nrow]
    expand = _block_expand(n_pages + 1, page, n_sel_blocks)
    const = lambda a: pl.BlockSpec(a.shape, lambda b, pt: (0,) * a.ndim)
    page_spec = lambda p: pl.BlockSpec((1,) + cache.shape[1:], lambda b, pt: (pt[b, p], 0, 0, 0))
    per_b = lambda a: pl.BlockSpec((1,) + a.shape[1:], lambda b, pt: (b, 0, 0))
    win_block = (1,) + cache_win.shape[1:]
    consts = list(cw) + [mmap, expand]
    grid_spec = pltpu.PrefetchScalarGridSpec(
        num_scalar_prefetch=1,
        grid=(B,),
        in_specs=[page_spec(p) for p in range(n_pages)]
        + [per_b(q8), per_b(small8), per_b(kvnew8), per_b(winnew8),
           pl.BlockSpec(win_block, lambda b, pt: (win_off + b, 0, 0, 0))] + [const(a) for a in consts],
        out_specs=[per_b(q8), pl.BlockSpec(win_block, lambda b, pt: (b, 0, 0, 0))],
        scratch_shapes=[pltpu.VMEM((page, kvnew8.shape[2]), F32), pltpu.VMEM((page, winnew8.shape[2]), F32),
                        pltpu.VMEM((2, SUBLANES, cw[2].shape[1]), F32)],
    )
    return pl.pallas_call(
        functools.partial(_nsa_decode_kernel, n_pages=n_pages, page=page, tq=tq, t_new=t_new, past_len=past_len,
                          n_cmp=n_cmp, n_sel_blocks=n_sel_blocks, wb=wb),
        grid_spec=grid_spec,
        out_shape=[jax.ShapeDtypeStruct(q8.shape, F32), jax.ShapeDtypeStruct((B,) + cache_win.shape[1:], F32)],
        compiler_params=_params(("arbitrary",)),
        name="nsa_decode",
    )(page_table, *([cache] * n_pages), q8, small8, kvnew8, winnew8, cache_win, *consts)


def _layer_weights(i, prm, d_ml, d_nsa):
    H, G, dk = ML_HEADS, NSA_KV_HEADS, NSA_HEAD_DIM
    R = d_nsa // (G * dk)
    kvw = G * dk
    w_in = prm["w_in"][i]
    o = 0
    c_in, o = w_in[:, o:o + d_ml], o + d_ml
    v_ml, o = w_in[:, o:o + d_ml], o + d_ml
    o_ml, o = w_in[:, o:o + d_ml], o + d_ml
    i_ml, o = w_in[:, o:o + H], o + H
    f_ml, o = w_in[:, o:o + H], o + H
    q_n, o = w_in[:, o:o + d_nsa], o + d_nsa
    kv, o = w_in[:, o:o + 6 * kvw], o + 6 * kvw
    g_n = w_in[:, o:]
    K = w_in.shape[0]
    q_perm = q_n.reshape(K, G, R, dk).transpose(0, 2, 1, 3).reshape(K, d_nsa)
    w_main = jnp.concatenate([c_in, v_ml, o_ml, q_perm, kv], axis=1).astype(BF16)
    w_kvw_t = kv.T.astype(BF16)
    small = jnp.concatenate([i_ml, f_ml, g_n], axis=1)
    w_small = jnp.pad(small, ((0, 0), (0, LANES - small.shape[1]))).astype(BF16)
    w_out = prm["w_out"][i]
    w_out_nsa = w_out[d_ml:].reshape(G, R, dk, -1).transpose(1, 0, 2, 3).reshape(d_nsa, -1)
    w_out_p = jnp.concatenate([w_out[:d_ml], w_out_nsa], axis=0).astype(BF16)
    Dh = d_ml // H
    gate_bias = jnp.pad(jnp.concatenate([prm["b_i"][i], prm["b_f"][i]]), (0, LANES - 2 * H)).reshape(1, LANES)
    w2 = prm["cmp_w2"][i]
    w2_placed = jnp.stack([jnp.stack([jnp.pad(w2[s], ((0, 0), (g * dk, (G - 1 - g) * dk))) for g in range(G)])
                           for s in range(2)]).astype(BF16)
    perm = np.zeros((LANES, LANES), np.float32)
    tok = np.arange(LANES)
    perm[(tok % CMP_STRIDE) * (LANES // CMP_STRIDE) + tok // CMP_STRIDE, tok] = 1.0
    cmp_w = (prm["cmp_pe"][i].reshape(2, -1), prm["cmp_w1"][i].astype(BF16), prm["cmp_b1"][i], w2_placed,
             jnp.tile(prm["cmp_b2"][i], (1, G)), jnp.asarray(perm, BF16))
    return dict(
        w_main=w_main, w_kvw_t=w_kvw_t, w_small=w_small, w_out=w_out_p,
        w_up=prm["w_up_all"], w_down=prm["w_down_all"],
        w_pl=prm["w_pl"][i].astype(BF16), w_pl_gate=prm["w_pl_gate"][i].astype(BF16),
        conv_w=prm["conv_w"][i], conv_b=prm["conv_b"][i].reshape(1, -1),
        wq=prm["w_q_ml"][i].astype(BF16),
        wkt=(jnp.swapaxes(prm["w_k_ml"][i], 1, 2) * (Dh ** -0.5)).astype(BF16),
        gate_bias=gate_bias, g_ml=prm["g_ml"][i].reshape(1, -1), cmp=cmp_w)


def _pad_time(a, B, t, tp):
    return jnp.pad(a.reshape(B, t, -1), ((0, 0), (0, tp - t), (0, 0)))


def _layer(h, pl_e, i, prm, lw, mem, *, B, T):
    M, D = h.shape
    G, dk, H = NSA_KV_HEADS, NSA_HEAD_DIM, ML_HEADS
    d_ml = lw["conv_w"].shape[1]
    d_nsa = lw["w_out"].shape[0] - d_ml
    R = d_nsa // (G * dk)
    Dh = d_ml // H
    kvw = G * dk
    tm = min(M, 512)
    g_pre = prm["g_pre_mix"][i]
    lanes_last = lambda a, lead: a.reshape(lead + (G, dk, a.shape[-1]))

    ml_w = (lw["conv_w"], lw["conv_b"], lw["wq"], lw["wkt"], lw["gate_bias"], lw["g_ml"])
    if mem is None:
        cin, v, o, q, small = _norm_matmul(h, g_pre, lw["w_main"], (d_ml, d_ml, d_ml, d_nsa),
                                           w_small=lw["w_small"], tm=min(M, 1024), tn=512)
        kv_t, win_t = _norm_matmul_t(h, g_pre, lw["w_kvw_t"], (4 * kvw, 2 * kvw), B=B, T=T, tm=min(T, 1024), tn=512)
        zeros = lambda *s: jnp.zeros(s, F32)
        hm, c_new, n_new, m_new = _mlstm(
            cin, v, o, small, zeros(B, SUBLANES, d_ml), zeros(B, H, Dh, Dh), zeros(B, 1, d_ml), zeros(B, 1, LANES),
            *ml_w, B=B, rows=ML_CHUNK, valid=ML_CHUNK)
        kcmp, vcmp = _compress_prompt(kv_t, lw["cmp"], B=B, T=T)
        on = _nsa_prompt(q, small, kcmp, vcmp, kv_t, win_t, B=B, T=T, tq=64, ck=256)
        wlen = min(WINDOW, T)
        new_rows = lanes_last(kv_t, (B, 4)).transpose(0, 4, 1, 2, 3)
        new_win = lanes_last(win_t[:, :, T - wlen:], (B, 2)).transpose(0, 4, 1, 2, 3)
    else:
        cin, v, o, q, kv4, win2, small = _norm_matmul(
            h, g_pre, lw["w_main"], (d_ml, d_ml, d_ml, d_nsa, 4 * kvw, 2 * kvw),
            w_small=lw["w_small"], tm=tm, tn=512)
        tp = SUBLANES
        pad = lambda a: _pad_time(a, B, T, tp)
        prev = jnp.pad(mem["conv"], ((0, 0), (SUBLANES - mem["conv"].shape[1], 0), (0, 0)))
        m0 = jnp.pad(mem["m"], ((0, 0), (0, LANES - H))).reshape(B, 1, LANES)
        hm8, c_new, n_new, m_new = _mlstm(
            pad(cin).reshape(B * tp, d_ml), pad(v).reshape(B * tp, d_ml), pad(o).reshape(B * tp, d_ml),
            pad(small).reshape(B * tp, LANES), prev, mem["C"], mem["n"].reshape(B, 1, d_ml), m0,
            *ml_w, B=B, rows=tp, valid=T, c_off=i * B)
        hm = hm8.reshape(B, tp, d_ml)[:, :T].reshape(M, d_ml)
        page = mem["kv"].shape[3]
        on8, new_win_t = _nsa_decode(
            pad(q), pad(small), pad(kv4), pad(win2),
            mem["kv"], mem["win"], mem["page_table"], lw["cmp"],
            past_len=mem["page_table"].shape[1] * page, t_new=T, win_off=i * B)
        on = on8[:, :T].reshape(M, d_nsa)
        new_rows = kv4.reshape(B, T, 4, G, dk)
        new_win = lanes_last(new_win_t, (B, 2)).transpose(0, 4, 1, 2, 3)

    h = _matmul_norm_res([hm, on], lw["w_out"], h, prm["g_post_mix"][i], tm=tm, tk=d_ml + d_nsa)
    u, = _norm_matmul(h, prm["g_pre_mlp"][i], lw["w_up"], (lw["w_up"].shape[1],), tm=min(M, 1024), tn=1024,
                      act="relu2", out_dtype=BF16, layer=i)
    h = _matmul_norm_res([u], lw["w_down"], h, prm["g_post_mlp"][i], tm=tm, tk=2048, layer=i)
    h = _ple(h, pl_e, lw["w_pl_gate"], lw["w_pl"], prm["g_pl"][i], tm=min(M, 256), layer=i)

    state = (new_rows, new_win, c_new, n_new.reshape(B, H, Dh), m_new[:, 0, :H], cin.reshape(B, T, d_ml)[:, T - 3:])
    return h, state


def kernel(x_prompt, x_sample, cache_kv, cache_win, state_C, state_n, state_m, state_conv, page_table,
           p_prompt, p_sample, g_pre_mix, w_in, conv_w, conv_b, w_q_ml, w_k_ml, b_i, b_f, g_ml,
           cmp_pe, cmp_w1, cmp_b1, cmp_w2, cmp_b2, w_out, g_post_mix, g_pre_mlp, w_up, w_down,
           g_post_mlp, w_pl, g_pl, w_pl_gate):
    prm = dict(g_pre_mix=g_pre_mix, w_in=w_in, conv_w=conv_w, conv_b=conv_b, w_q_ml=w_q_ml, w_k_ml=w_k_ml,
               b_i=b_i, b_f=b_f, g_ml=g_ml, cmp_pe=cmp_pe, cmp_w1=cmp_w1, cmp_b1=cmp_b1, cmp_w2=cmp_w2,
               cmp_b2=cmp_b2, w_out=w_out, g_post_mix=g_post_mix, g_pre_mlp=g_pre_mlp, w_up=w_up,
               w_down=w_down, g_post_mlp=g_post_mlp, w_pl=w_pl, g_pl=g_pl, w_pl_gate=w_pl_gate)
    Bp, Tp, D = x_prompt.shape
    Bs, Ts, _ = x_sample.shape
    depth = w_in.shape[0]
    d_ml = conv_w.shape[2]
    d_nsa = w_out.shape[1] - d_ml
    hp = x_prompt.reshape(Bp * Tp, D)
    hs = x_sample.reshape(Bs * Ts, D)
    n_pool, page = cache_kv.shape[1:3]
    kv_t = jnp.transpose(cache_kv, (0, 1, 3, 4, 5, 2)).reshape(depth * n_pool, cache_kv.shape[3], -1, page)
    win_t = jnp.transpose(cache_win, (0, 1, 3, 4, 5, 2)).reshape(depth * Bs, cache_win.shape[3], -1, cache_win.shape[2])
    c_all = state_C.reshape((depth * Bs,) + state_C.shape[2:])
    prm["w_up_all"] = w_up.astype(BF16).reshape(-1, w_up.shape[2])
    prm["w_down_all"] = w_down.astype(BF16).reshape(-1, w_down.shape[2])
    sp, ss = [], []
    for i in range(depth):
        lw = _layer_weights(i, prm, d_ml, d_nsa)
        mem = dict(kv=kv_t, page_table=page_table + i * n_pool, win=win_t, C=c_all, n=state_n[i],
                   m=state_m[i], conv=state_conv[i])
        hp, st_p = _layer(hp, p_prompt.reshape(depth * Bp * Tp, -1), i, prm, lw, None, B=Bp, T=Tp)
        hs, st_s = _layer(hs, p_sample.reshape(depth * Bs * Ts, -1), i, prm, lw, mem, B=Bs, T=Ts)
        sp.append(st_p)
        ss.append(st_s)

    stk = lambda lst, j: jnp.stack([s[j] for s in lst])
    return (hp.reshape(Bp, Tp, D), hs.reshape(Bs, Ts, D), stk(sp, 0), stk(ss, 0), stk(sp, 1), stk(ss, 1),
            stk(sp, 2), stk(ss, 2), stk(sp, 3), stk(ss, 3), stk(sp, 4), stk(ss, 4), stk(sp, 5), stk(ss, 5))
```

```python
import functools

import numpy as np
import jax
import jax.numpy as jnp
from jax import lax
from jax.experimental import pallas as pl
from jax.experimental.pallas import tpu as pltpu

ML_HEADS = 4
NSA_HEAD_DIM = 64
NSA_KV_HEADS = 4
CMP_BLOCK = 32
CMP_STRIDE = 16
SEL_BLOCK = 64
N_SEL = 16
WINDOW = 512
EPS = 1e-6
NEG = -1e30
FORCE = 1e6

LANES = 128
SUBLANES = 8
VMEM_LIMIT_BYTES = 56 * 1024 * 1024

ML_CHUNK = 128
MLSTM_SEQS_PER_STEP = 1
GROUP_LANES = NSA_KV_HEADS * NSA_HEAD_DIM
QK_SCALE = NSA_HEAD_DIM ** -0.5 * 1.4426950408889634
MASK_BIG = 1e30

F32 = jnp.float32
BF16 = jnp.bfloat16


def _dot(a, b):
    return jnp.dot(a, b, preferred_element_type=F32)


def _dot_nt(a, b):
    return lax.dot_general(a, b, (((1,), (1,)), ((), ())), preferred_element_type=F32)


def _params(semantics):
    return pltpu.CompilerParams(dimension_semantics=semantics, vmem_limit_bytes=VMEM_LIMIT_BYTES)


def _rms_rows(x):
    return x * lax.rsqrt(jnp.mean(x * x, axis=-1, keepdims=True) + EPS)


def _norm_matmul_kernel(x_ref, g_ref, w_ref, *rest, seg_tiles, has_small, act):
    if has_small:
        ws_ref, rest = rest[0], rest[1:]
    n_out = len(seg_tiles) + (1 if has_small else 0)
    out_refs, xn_ref = rest[:n_out], rest[n_out]
    j = pl.program_id(1)

    @pl.when(j == 0)
    def _():
        xn = (_rms_rows(x_ref[...]) * g_ref[...]).astype(BF16)
        xn_ref[...] = xn
        if has_small:
            out_refs[-1][...] = _dot(xn, ws_ref[...])

    y = _dot(xn_ref[...], w_ref[...])
    if act == "relu2":
        y = jnp.square(jnp.maximum(y, 0.0))
    start = 0
    for k, n in enumerate(seg_tiles):
        @pl.when((j >= start) & (j < start + n))
        def _(k=k):
            out_refs[k][...] = y.astype(out_refs[k].dtype)
        start += n


def _norm_matmul(x, g, w, seg_cols, *, w_small=None, tm, tn, act=None, out_dtype=F32, layer=0):
    M, K = x.shape
    seg_tiles = tuple(c // tn for c in seg_cols)
    assert all(c % tn == 0 for c in seg_cols) and M % tm == 0
    has_small = w_small is not None
    starts = np.concatenate([[0], np.cumsum(seg_tiles)[:-1]]).tolist()

    def out_map(start, n):
        return lambda i, j: (i, jnp.clip(j - start, 0, n - 1))

    in_specs = [pl.BlockSpec((tm, K), lambda i, j: (i, 0)),
                pl.BlockSpec((1, K), lambda i, j: (0, 0)),
                pl.BlockSpec((K, tn), lambda i, j: (layer, j))]
    args = [x, g.reshape(1, K), w]
    out_specs = [pl.BlockSpec((tm, tn), out_map(s, n)) for s, n in zip(starts, seg_tiles)]
    out_shape = [jax.ShapeDtypeStruct((M, c), out_dtype) for c in seg_cols]
    if has_small:
        in_specs.append(pl.BlockSpec((K, LANES), lambda i, j: (0, 0)))
        args.append(w_small)
        out_specs.append(pl.BlockSpec((tm, LANES), lambda i, j: (i, 0)))
        out_shape.append(jax.ShapeDtypeStruct((M, LANES), F32))
    return pl.pallas_call(
        functools.partial(_norm_matmul_kernel, seg_tiles=seg_tiles, has_small=has_small, act=act),
        grid=(M // tm, sum(seg_tiles)),
        in_specs=in_specs, out_specs=out_specs, out_shape=out_shape,
        scratch_shapes=[pltpu.VMEM((tm, K), BF16)],
        compiler_params=_params(("parallel", "arbitrary")),
        name="norm_matmul",
    )(*args)


def _norm_matmul_t_kernel(x_ref, g_ref, wt_ref, *rest, seg_tiles):
    out_refs, xn_ref = rest[:-1], rest[-1]
    j = pl.program_id(1)

    @pl.when(j == 0)
    def _():
        xn_ref[...] = (_rms_rows(x_ref[...]) * g_ref[...]).astype(BF16)

    y = _dot_nt(wt_ref[...], xn_ref[...])
    start = 0
    for k, n in enumerate(seg_tiles):
        @pl.when((j >= start) & (j < start + n))
        def _(k=k):
            out_refs[k][0] = y
        start += n


def _norm_matmul_t(x, g, wt, seg_rows, *, B, T, tm, tn):
    M, K = x.shape
    seg_tiles = tuple(r // tn for r in seg_rows)
    assert T % tm == 0 and all(r % tn == 0 for r in seg_rows) and M == B * T
    per = T // tm
    starts = np.concatenate([[0], np.cumsum(seg_tiles)[:-1]]).tolist()

    def out_map(start, n):
        return lambda i, j: (i // per, jnp.clip(j - start, 0, n - 1), i % per)

    return pl.pallas_call(
        functools.partial(_norm_matmul_t_kernel, seg_tiles=seg_tiles),
        grid=(M // tm, sum(seg_tiles)),
        in_specs=[pl.BlockSpec((tm, K), lambda i, j: (i, 0)),
                  pl.BlockSpec((1, K), lambda i, j: (0, 0)),
                  pl.BlockSpec((tn, K), lambda i, j: (j, 0))],
        out_specs=[pl.BlockSpec((1, tn, tm), out_map(s, n)) for s, n in zip(starts, seg_tiles)],
        out_shape=[jax.ShapeDtypeStruct((B, r, T), F32) for r in seg_rows],
        scratch_shapes=[pltpu.VMEM((tm, K), BF16)],
        compiler_params=_params(("parallel", "arbitrary")),
        name="norm_matmul_t",
    )(x, g.reshape(1, K), wt)


def _matmul_norm_res_kernel(*refs, n_a):
    a_refs = refs[:n_a]
    w_ref, h_ref, g_ref, o_ref, acc_ref = refs[n_a:]
    k = pl.program_id(1)

    @pl.when(k == 0)
    def _():
        acc_ref[...] = jnp.zeros_like(acc_ref)

    off = 0
    for a_ref in a_refs:
        width = a_ref.shape[1]
        acc_ref[...] += _dot(a_ref[...].astype(BF16), w_ref[off:off + width, :])
        off += width

    @pl.when(k == pl.num_programs(1) - 1)
    def _():
        o_ref[...] = h_ref[...] + _rms_rows(acc_ref[...]) * g_ref[...]


def _matmul_norm_res(a_parts, w, h, g, *, tm, tk, layer=0):
    M = a_parts[0].shape[0]
    K = sum(a.shape[1] for a in a_parts)
    N = w.shape[1]
    assert M % tm == 0 and K % tk == 0 and (len(a_parts) == 1 or tk == K)
    k_steps = K // tk
    a_specs = [pl.BlockSpec((tm, tk if len(a_parts) == 1 else a.shape[1]), lambda i, k: (i, k)) for a in a_parts]
    return pl.pallas_call(
        functools.partial(_matmul_norm_res_kernel, n_a=len(a_parts)),
        grid=(M // tm, k_steps),
        in_specs=a_specs + [pl.BlockSpec((tk, N), lambda i, k: (layer * k_steps + k, 0)),
                            pl.BlockSpec((tm, N), lambda i, k: (i, 0)),
                            pl.BlockSpec((1, N), lambda i, k: (0, 0))],
        out_specs=pl.BlockSpec((tm, N), lambda i, k: (i, 0)),
        out_shape=jax.ShapeDtypeStruct((M, N), F32),
        scratch_shapes=[pltpu.VMEM((tm, N), F32)],
        compiler_params=_params(("parallel", "arbitrary")),
        name="matmul_norm_res",
    )(*a_parts, w, h, g.reshape(1, N))


def _ple_kernel(h_ref, p_ref, wg_ref, wp_ref, g_ref, o_ref):
    h = h_ref[...]
    gate = jax.nn.sigmoid(_dot(_rms_rows(h).astype(BF16), wg_ref[...]))
    e = _dot(p_ref[...].astype(BF16), wp_ref[...])
    o_ref[...] = h + gate * (_rms_rows(e) * g_ref[...])


def _ple(h, p, wg, wp, g, *, tm, layer=0):
    M, N = h.shape
    P = p.shape[1]
    assert M % tm == 0
    return pl.pallas_call(
        _ple_kernel,
        grid=(M // tm,),
        in_specs=[pl.BlockSpec((tm, N), lambda i: (i, 0)),
                  pl.BlockSpec((tm, P), lambda i: (layer * (M // tm) + i, 0)),
                  pl.BlockSpec((N, N), lambda i: (0, 0)),
                  pl.BlockSpec((P, N), lambda i: (0, 0)),
                  pl.BlockSpec((1, N), lambda i: (0, 0))],
        out_specs=pl.BlockSpec((tm, N), lambda i: (i, 0)),
        out_shape=jax.ShapeDtypeStruct((M, N), F32),
        compiler_params=_params(("parallel",)),
        name="ple",
    )(h, p, wg, wp, g.reshape(1, N))


def _shift_rows(x, prev8, s):
    rows = x.shape[0]
    xs = pltpu.roll(x, s, axis=0)
    rid = lax.broadcasted_iota(jnp.int32, (SUBLANES, x.shape[1]), 0)
    head = jnp.where(rid < s, pltpu.roll(prev8, s, axis=0), xs[:SUBLANES])
    if rows == SUBLANES:
        return head
    return jnp.concatenate([head, xs[SUBLANES:]], axis=0)


def _mlstm_kernel(*refs, nb, aliased, **static):
    n_seq, n_shared = 8, 7
    seq_in, shared = refs[:n_seq], refs[n_seq:n_seq + n_shared]
    rest = refs[n_seq + n_shared + (1 if aliased else 0):]
    per_seq = [[r.at[j] for r in seq_in] + list(shared) + [r.at[j] for r in rest] for j in range(nb)]
    for args in per_seq:
        _mlstm_load_state(*args, **static)
    m_rows = [_mlstm_seq(*args, **static) for args in per_seq]
    for args, m_row in zip(per_seq, m_rows):
        _mlstm_store_state(*args, m_row, **static)


def _mlstm_load_state(cin_ref, v_ref, o_ref, sm_ref, prev_ref, c0_ref, n0_ref, m0_ref,
                      cw_ref, cb_ref, wq_ref, wkt_ref, gb_ref, gml_ref, ltri_ref,
                      hm_ref, cout_ref, nout_ref, mout_ref,
                      carry_ref, cext_ref, m_ref, chpad_ref, vpad_ref, gpad_ref,
                      *, L, rows, valid, H, Dh):
    @pl.when(pl.program_id(1) == 0)
    def _():
        carry_ref[...] = prev_ref[...]
        m_ref[...] = m0_ref[...]
        for h in range(H):
            cext_ref[h, :, :Dh] = c0_ref[h]
            n_row = n0_ref[:, h * Dh:(h + 1) * Dh]
            cext_ref[h, :, Dh:] = jnp.broadcast_to(n_row, (LANES, Dh)).T

    if rows < L:
        @pl.when((pl.program_id(0) == 0) & (pl.program_id(1) == 0))
        def _():
            chpad_ref[...] = jnp.zeros_like(chpad_ref)
            vpad_ref[...] = jnp.zeros_like(vpad_ref)
            gpad_ref[...] = jnp.zeros_like(gpad_ref)


def _mlstm_store_state(cin_ref, v_ref, o_ref, sm_ref, prev_ref, c0_ref, n0_ref, m0_ref,
                       cw_ref, cb_ref, wq_ref, wkt_ref, gb_ref, gml_ref, ltri_ref,
                       hm_ref, cout_ref, nout_ref, mout_ref,
                       carry_ref, cext_ref, m_ref, chpad_ref, vpad_ref, gpad_ref, m_new_row,
                       *, L, rows, valid, H, Dh):
    @pl.when(pl.program_id(1) == pl.num_programs(1) - 1)
    def _():
        mout_ref[...] = m_new_row
        for h in range(H):
            cout_ref[h] = cext_ref[h, :, :Dh]
            nout_ref[:, h * Dh:(h + 1) * Dh] = cext_ref[h, :, Dh:].T[0:1, :]


def _mlstm_seq(cin_ref, v_ref, o_ref, sm_ref, prev_ref, c0_ref, n0_ref, m0_ref,
               cw_ref, cb_ref, wq_ref, wkt_ref, gb_ref, gml_ref, ltri_ref,
               hm_ref, cout_ref, nout_ref, mout_ref,
               carry_ref, cext_ref, m_ref, chpad_ref, vpad_ref, gpad_ref,
               *, L, rows, valid, H, Dh):
    x = cin_ref[...]
    prev8 = carry_ref[...]
    conv = cb_ref[...] + cw_ref[3:4, :] * x
    for s in (1, 2, 3):
        conv = conv + cw_ref[3 - s:4 - s, :] * _shift_rows(x, prev8, s)
    carry_ref[...] = x[rows - SUBLANES:, :]
    ch = conv * jax.nn.sigmoid(conv)

    if rows < L:
        chpad_ref[:rows, :] = ch
        vpad_ref[:rows, :] = v_ref[...]
        gpad_ref[:rows, :] = sm_ref[...]
        ch, v_all, sm = chpad_ref[...], vpad_ref[...], gpad_ref[...]
    else:
        v_all, sm = v_ref[...], sm_ref[...]
    ch = ch.astype(BF16)

    lane = lax.broadcasted_iota(jnp.int32, (L, LANES), 1)
    rid = lax.broadcasted_iota(jnp.int32, (L, LANES), 0)
    pre = sm + gb_ref[...]
    gates = jnp.where(lane < H, pre, jax.nn.log_sigmoid(pre))
    gates = jnp.where(rid < valid, gates, jnp.where(lane < H, NEG, 0.0))
    hi = gates.astype(BF16)
    r1 = gates - hi.astype(F32)
    mid = r1.astype(BF16)
    lo = (r1 - mid.astype(F32)).astype(BF16)
    ltri = ltri_ref[...]
    bc_all = _dot(ltri, hi) + _dot(ltri, mid) + _dot(ltri, lo)
    colform = jnp.where(lane < H, gates, bc_all)
    rowform = colform.T

    r_i = lax.broadcasted_iota(jnp.int32, (L, L), 0)
    c_i = lax.broadcasted_iota(jnp.int32, (L, L), 1)
    causal = c_i <= r_i
    lane1 = lax.broadcasted_iota(jnp.int32, (1, LANES), 1)
    m_row = m_ref[...]
    m_new_row = m_row
    ones = jnp.ones((L, LANES), F32)

    for h in range(H):
        hs = slice(h * Dh, (h + 1) * Dh)
        ch_h = ch[:, hs]
        q_h = _dot(ch_h, wq_ref[h]).astype(BF16)
        kt_h = _dot_nt(wkt_ref[h], ch_h).astype(BF16)
        v_h = v_all[:, hs]
        li_col = colform[:, h:h + 1]
        bc_col = colform[:, H + h:H + h + 1]
        li_row = rowform[h:h + 1, :]
        bc_row = rowform[H + h:H + h + 1, :]
        m_h = m_row[:, h:h + 1]

        logd = jnp.where(causal, bc_col - bc_row + li_row, NEG)
        inter = m_h + bc_col
        mt = jnp.maximum(jnp.max(logd, axis=-1, keepdims=True), inter)
        dm = jnp.exp(logd - mt)
        a_int = jnp.exp(inter - mt)
        s = _dot(q_h, kt_h) * dm
        v_ext = jnp.concatenate([v_h, ones], axis=1)
        cext = cext_ref[h]
        tot = _dot(s.astype(BF16), v_ext.astype(BF16)) + a_int * _dot(q_h, cext.astype(BF16))
        num = tot[:, :Dh]
        den = tot[:, Dh:Dh + 1]
        hh = num / jnp.maximum(jnp.abs(den), jnp.exp(-mt))

        m_last = mt[L - 1:L, :]
        bc_last = bc_col[L - 1:L, :]
        w_col = jnp.exp(bc_last - bc_col + li_col - m_last)
        dec = jnp.exp(m_h + bc_last - m_last)
        cext_ref[h] = dec * cext + _dot(kt_h, (w_col * v_ext).astype(BF16))
        m_new_row = jnp.where(lane1 == h, m_last, m_new_row)

        y = _rms_rows(hh) * gml_ref[:, hs]
        y = y[:rows] * jax.nn.sigmoid(o_ref[:, hs])
        hm_ref[:, hs] = y

    m_ref[...] = m_new_row
    return m_new_row


def _mlstm(cin, v, o, small, prev, c0, n0, m0, cw, cb, wq, wkt, gb, gml, *, B, rows, valid, c_off=0,
           c_layers=1, c_stack=None):
    L = ML_CHUNK
    H = ML_HEADS
    D = cin.shape[1]
    Dh = D // H
    nc = cin.shape[0] // (B * rows)
    nb = MLSTM_SEQS_PER_STEP
    assert rows == L or (nc == 1 and rows == SUBLANES)
    assert B % nb == 0 and c_off % nb == 0
    ltri = jnp.asarray(np.tril(np.ones((L, L), np.float32)), BF16)
    seq3 = lambda a: a.reshape(B, nc * rows, a.shape[1])
    tok = lambda g, c: (g, c, 0)
    per_g3 = lambda g, c: (g, 0, 0)
    per_g4 = lambda g, c: (g, 0, 0, 0)
    const2 = lambda g, c: (0, 0)
    const3 = lambda g, c: (0, 0, 0)
    operands = [seq3(cin), seq3(v), seq3(o), seq3(small), prev, c0, n0, m0, cw, cb, wq, wkt, gb, gml, ltri]
    stacked = c_layers > 1
    aliased = stacked and c_stack is not None
    extra_specs, aliases = [], {}
    if aliased:
        extra_specs = [pl.BlockSpec(memory_space=pl.ANY)]
        aliases = {len(operands): 1}
        operands.append(c_stack)
    c_out_map = (lambda g, c: (c_off // nb + g, 0, 0, 0)) if stacked else per_g4
    hm, cout, nout, mout = pl.pallas_call(
        functools.partial(_mlstm_kernel, nb=nb, aliased=aliased, L=L, rows=rows, valid=valid, H=H, Dh=Dh),
        grid=(B // nb, nc),
        input_output_aliases=aliases,
        in_specs=[pl.BlockSpec((nb, rows, D), tok), pl.BlockSpec((nb, rows, D), tok), pl.BlockSpec((nb, rows, D), tok),
                  pl.BlockSpec((nb, rows, LANES), tok),
                  pl.BlockSpec((nb, SUBLANES, D), per_g3),
                  pl.BlockSpec((nb, H, Dh, Dh), lambda g, c: (c_off // nb + g, 0, 0, 0)),
                  pl.BlockSpec((nb, 1, D), per_g3),
                  pl.BlockSpec((nb, 1, LANES), per_g3),
                  pl.BlockSpec((4, D), const2), pl.BlockSpec((1, D), const2),
                  pl.BlockSpec((H, Dh, Dh), const3), pl.BlockSpec((H, Dh, Dh), const3),
                  pl.BlockSpec((1, LANES), const2), pl.BlockSpec((1, D), const2),
                  pl.BlockSpec((L, L), const2)] + extra_specs,
        out_specs=[pl.BlockSpec((nb, rows, D), tok),
                   pl.BlockSpec((nb, H, Dh, Dh), c_out_map),
                   pl.BlockSpec((nb, 1, D), per_g3),
                   pl.BlockSpec((nb, 1, LANES), per_g3)],
        out_shape=[jax.ShapeDtypeStruct((B, nc * rows, D), F32),
                   jax.ShapeDtypeStruct((c_layers * B, H, Dh, Dh), F32),
                   jax.ShapeDtypeStruct((B, 1, D), F32),
                   jax.ShapeDtypeStruct((B, 1, LANES), F32)],
        scratch_shapes=[pltpu.VMEM((nb, SUBLANES, D), F32),
                        pltpu.VMEM((nb, H, Dh, Dh + LANES), F32),
                        pltpu.VMEM((nb, 1, LANES), F32),
                        pltpu.VMEM((nb, L, D), F32), pltpu.VMEM((nb, L, D), F32), pltpu.VMEM((nb, L, LANES), F32)],
        compiler_params=_params(("arbitrary", "arbitrary")),
        name="mlstm",
    )(*operands)
    return hm.reshape(cin.shape), cout, nout, mout


def _deinterleave_t(perm, groups):
    per = LANES // CMP_STRIDE
    ys = [_dot_nt(perm, x_t.astype(BF16)) for x_t in groups]
    return [jnp.concatenate([y[l * per:(l + 1) * per] for y in ys], axis=0) for l in range(CMP_STRIDE)]


def _pe_bias(slot, pe_ref, w1_ref, b1_ref):
    pe = jnp.broadcast_to(pe_ref[slot:slot + 1, :], (SUBLANES, pe_ref.shape[1])).astype(BF16)
    return _dot(pe, w1_ref[slot]) + b1_ref[slot:slot + 1, :]


def _compress_slot(xs, slot, bias, w1_ref, w2_ref, b2_ref):
    dk = NSA_HEAD_DIM
    half = CMP_BLOCK // 2
    nrow = xs[0].shape[0]
    pack = GROUP_LANES // dk
    G = NSA_KV_HEADS
    first = None
    second = None
    for l0 in range(0, half, pack):
        xg = jnp.concatenate(
            [jnp.concatenate([xs[l0 + j][:, g * dk:(g + 1) * dk] for j in range(pack)], axis=1) for g in range(G)],
            axis=0).astype(BF16)
        a = _dot(xg, w1_ref[slot, l0 * dk:(l0 + pack) * dk, :])
        bb = _dot(xg, w1_ref[slot, (half + l0) * dk:(half + l0 + pack) * dk, :])
        first = a if first is None else first + a
        second = bb if second is None else second + bb
    hid = jax.nn.gelu(first + pltpu.roll(second, G * nrow - 1, axis=0) + bias).astype(BF16)
    out = None
    for g in range(G):
        og = _dot(hid[g * nrow:(g + 1) * nrow], w2_ref[slot, g])
        out = og if out is None else out + og
    return out + b2_ref[slot:slot + 1, :]


def _expand_q(q, tq):
    slab = NSA_KV_HEADS * tq
    lane_g = lax.broadcasted_iota(jnp.int32, (slab, GROUP_LANES), 1) // NSA_HEAD_DIM
    row_g = lax.broadcasted_iota(jnp.int32, (slab, GROUP_LANES), 0) // tq
    own = lane_g == row_g
    n_rep = q.shape[1] // GROUP_LANES
    slabs = []
    for r in range(n_rep):
        qr = q[:, r * GROUP_LANES:(r + 1) * GROUP_LANES] * QK_SCALE
        slabs.append(jnp.where(own, jnp.concatenate([qr] * NSA_KV_HEADS, axis=0), 0.0).astype(BF16))
    return slabs, own


def _attend(carry, q, chunks):
    m, l, acc = carry
    scores = []
    for k, _, bias in chunks:
        s = _dot(q, k)
        if bias is not None:
            s = _add_slab_bias(s, bias) if jnp.ndim(bias) == 2 else s + bias
        scores.append(s)
    mx = scores[0]
    for s in scores[1:]:
        mx = jnp.maximum(mx, s)
    m_new = jnp.maximum(m, jnp.max(mx, axis=1, keepdims=True))
    alpha = jnp.exp2(m - m_new)
    ps, pv = None, None
    for s, (_, v, _) in zip(scores, chunks):
        p = jnp.exp2(s - m_new)
        ps = p if ps is None else ps + p
        d = _dot_nt(p.astype(BF16), v)
        pv = d if pv is None else pv + d
    return m_new, alpha * l + jnp.sum(ps, axis=1, keepdims=True), alpha * acc + pv


def _add_slab_bias(s, bias):
    rows, ck = s.shape
    return (s.reshape(rows // bias.shape[0], bias.shape[0], ck) + bias[None]).reshape(rows, ck)


def _softmax_values(scores, values):
    mx = scores[0]
    for s in scores[1:]:
        mx = jnp.maximum(mx, s) if s.shape == mx.shape else mx
    m = mx.max(axis=1, keepdims=True)
    for s in scores[1:]:
        if s.shape != mx.shape:
            m = jnp.maximum(m, s.max(axis=1, keepdims=True))
    l, acc, ps = None, None, None
    for s, (v, v_t) in zip(scores, values):
        p = jnp.exp2(s - m)
        if p.shape == mx.shape:
            ps = p if ps is None else ps + p
        else:
            ls = p.sum(axis=1, keepdims=True)
            l = ls if l is None else l + ls
        pv = _dot_nt(p.astype(BF16), v) if v_t else _dot(p.astype(BF16), v)
        acc = pv if acc is None else acc + pv
    ls = ps.sum(axis=1, keepdims=True)
    return m, (ls if l is None else l + ls), acc


def _attend_init(rows):
    return (jnp.full((rows, 1), NEG, F32), jnp.zeros((rows, 1), F32), jnp.zeros((rows, GROUP_LANES), F32))


def _cmp_rows(q, mask, kcmp, vcmp):
    rows, slab = q.shape[0], mask.shape[0]
    s = _dot_nt(q, kcmp).reshape(rows // slab, slab, mask.shape[1])
    s = jnp.where(mask[None], s, NEG)
    e = jnp.where(mask[None], jnp.exp2(s - jnp.max(s, axis=2, keepdims=True)), 0.0)
    p = e * (1.0 / jnp.maximum(jnp.sum(e, axis=2, keepdims=True), 1e-30))
    p = p.reshape(rows, mask.shape[1]).astype(BF16)
    return _dot(p, vcmp), p


def _cmp_mask(rows, n_rows_cmp, qpos_col, n_cmp):
    lane = lax.broadcasted_iota(jnp.int32, (rows, n_rows_cmp), 1)
    return (lane * CMP_STRIDE + (CMP_BLOCK - 1) <= qpos_col) & (lane < n_cmp)


def _block_scores(imp, blk, cur, n_sel_blocks):
    forced = (blk == 0) | (blk == cur) | (blk == cur - 1)
    score = jnp.where(blk <= cur, jnp.where(forced, FORCE, imp), -1.0)
    return jnp.where(blk < n_sel_blocks, score, -2.0)


def _select_blocks(imp, qpos_col, n_sel_blocks):
    rows = imp.shape[0]
    blk = lax.broadcasted_iota(jnp.int32, (rows, LANES), 1)
    score = _block_scores(imp, blk, qpos_col // SEL_BLOCK, n_sel_blocks)
    rank = jnp.zeros((rows, LANES), F32)
    for j in range(n_sel_blocks):
        col = score[:, j:j + 1]
        ge = jnp.where(col >= score, 1.0, 0.0)
        gt = jnp.where(col > score, 1.0, 0.0)
        rank = rank + jnp.where(blk > j, ge, gt)
    chosen = jnp.where(rank < float(min(N_SEL, n_sel_blocks)), score, -1.0) >= 0.0
    return jnp.where(chosen, 0.0, -MASK_BIG).astype(BF16)


def _select_blocks_t(imp_t, qpos_row, n_sel_blocks):
    nb = -(-n_sel_blocks // SUBLANES) * SUBLANES
    cols = imp_t.shape[1]
    blk = lax.broadcasted_iota(jnp.int32, (nb, cols), 0)
    score = _block_scores(imp_t[:nb], blk, qpos_row // SEL_BLOCK, n_sel_blocks)
    rank = jnp.zeros((nb, cols), F32)
    for j in range(n_sel_blocks):
        row = score[j:j + 1, :]
        ge = jnp.where(row >= score, 1.0, 0.0)
        gt = jnp.where(row > score, 1.0, 0.0)
        rank = rank + jnp.where(blk > j, ge, gt)
    chosen = jnp.where(rank < float(min(N_SEL, n_sel_blocks)), score, -1.0) >= 0.0
    neg_t = jnp.where(chosen, 0.0, -MASK_BIG)
    neg_t = jnp.concatenate([neg_t, jnp.zeros((LANES - nb, cols), F32)], axis=0)
    return neg_t.T.astype(BF16)


def _gate_columns(sg, r, n_rep, tq):
    lane = lambda g, c: 2 * ML_HEADS + (g * n_rep + r) * 3 + c
    return [jnp.concatenate([sg[:, lane(g, c):lane(g, c) + 1] for g in range(NSA_KV_HEADS)], axis=0)
            for c in range(3)]


def _combine_slab(o_cmp, sel, win, gates, own, tq):
    comb = gates[0] * o_cmp
    for col, (_, l, acc) in ((1, sel), (2, win)):
        comb = comb + (gates[col] * (1.0 / jnp.maximum(l, 1e-30))) * acc
    kept = jnp.where(own, comb, 0.0)
    piece = kept[:tq]
    for g in range(1, NSA_KV_HEADS):
        piece = piece + kept[g * tq:(g + 1) * tq]
    return piece


def _compress_prompt_kernel(kc_ref, vc_ref, pe_ref, w1_ref, b1_ref, w2_ref, b2_ref, perm_ref, ko_ref, vo_ref):
    T = kc_ref.shape[2]
    for slot, (src, dst) in enumerate(((kc_ref, ko_ref), (vc_ref, vo_ref))):
        bias = _pe_bias(slot, pe_ref, w1_ref, b1_ref)[0:1]
        xs = _deinterleave_t(perm_ref[...], [src[0, :, t * LANES:(t + 1) * LANES] for t in range(T // LANES)])
        dst[0] = _compress_slot(xs, slot, bias, w1_ref, w2_ref, b2_ref)


def _compress_prompt(kvw_t, cw, *, B, T):
    nrow = T // CMP_STRIDE
    assert (T - CMP_BLOCK) // CMP_STRIDE + 1 <= nrow and T % LANES == 0
    full = lambda a: pl.BlockSpec(a.shape, lambda b: (0,) * a.ndim)
    return pl.pallas_call(
        _compress_prompt_kernel,
        grid=(B,),
        in_specs=[pl.BlockSpec((1, GROUP_LANES, T), lambda b: (b, 0, 0)),
                  pl.BlockSpec((1, GROUP_LANES, T), lambda b: (b, 1, 0))] + [full(a) for a in cw],
        out_specs=[pl.BlockSpec((1, nrow, GROUP_LANES), lambda b: (b, 0, 0))] * 2,
        out_shape=[jax.ShapeDtypeStruct((B, nrow, GROUP_LANES), F32)] * 2,
        compiler_params=_params(("parallel",)),
        name="compress_prompt",
    )(kvw_t, kvw_t, *cw)


def _nsa_prompt_kernel(q_ref, sm_ref, kcmp_ref, vcmp_ref, ks_ref, vs_ref, kw_ref, vw_ref,
                       mmap_t_ref, exp_ref, out_ref,
                       kx_ref, vsb_ref, kwb_ref, vwb_ref, *, tq, ck, n_cmp, n_sel_blocks):
    qt = pl.program_id(1)
    n_rep = q_ref.shape[1] // GROUP_LANES
    slab = NSA_KV_HEADS * tq
    n_chunks = kx_ref.shape[0]

    @pl.when(qt == 0)
    def _():
        for c in range(n_chunks):
            cs = slice(c * ck, (c + 1) * ck)
            kx_ref[c, :GROUP_LANES, :] = ks_ref[0, :, cs].astype(BF16)
            kx_ref[c, GROUP_LANES:, :] = exp_ref[c]
            vsb_ref[c] = vs_ref[0, :, cs].astype(BF16)
            kwb_ref[c] = kw_ref[0, :, cs].astype(BF16)
            vwb_ref[c] = vw_ref[0, :, cs].astype(BF16)

    t0 = qt * tq
    qpos = t0 + lax.broadcasted_iota(jnp.int32, (slab, 1), 0) % tq
    qpos_row = t0 + lax.broadcasted_iota(jnp.int32, (1, slab), 1) % tq
    qs, own = _expand_q(q_ref[...], tq)
    sg = jax.nn.sigmoid(sm_ref[...])
    reps = range(n_rep)

    q_all = jnp.concatenate(qs, axis=0)
    kcmp, vcmp = kcmp_ref[0].astype(BF16), vcmp_ref[0].astype(BF16)
    o_cmp, p_cmp = _cmp_rows(q_all, _cmp_mask(slab, kcmp.shape[0], qpos, n_cmp), kcmp, vcmp)
    imp_rows_t = _dot_nt(mmap_t_ref[...], p_cmp)
    imp_t = imp_rows_t[:, :slab]
    for r in range(1, n_rep):
        imp_t = imp_t + imp_rows_t[:, r * slab:(r + 1) * slab]
    sel_neg = _select_blocks_t(imp_t, qpos_row, n_sel_blocks)
    qx = jnp.concatenate([q_all, jnp.concatenate([sel_neg] * n_rep, axis=0)], axis=1)

    kpos0 = lax.broadcasted_iota(jnp.int32, (1, ck), 1)
    init = _attend_init(n_rep * slab)
    c_diag = t0 // ck

    def sel_body(i, carry):
        return _attend(carry, qx, [(kx_ref[2 * i], vsb_ref[2 * i], None), (kx_ref[2 * i + 1], vsb_ref[2 * i + 1], None)])

    sel = lax.fori_loop(0, c_diag // 2, sel_body, init)
    causal = jnp.where(kpos0 + c_diag * ck <= qpos, 0.0, NEG)
    odd = jnp.maximum(c_diag - 1, 0)
    no_odd = jnp.where(c_diag % 2 == 1, 0.0, NEG)
    sel = _attend(sel, qx, [(kx_ref[odd], vsb_ref[odd], no_odd), (kx_ref[c_diag], vsb_ref[c_diag], causal)])

    n_win = -(-(WINDOW - 1) // ck) + 1
    idx = [jnp.maximum(c_diag - (n_win - 1 - j), 0) for j in range(n_win)]
    gone = [jnp.where(c_diag >= n_win - 1 - j, 0.0, NEG) for j in range(n_win)]
    scores = []
    for j in range(n_win):
        s = _dot(q_all, kwb_ref[idx[j]])
        if j == n_win - 1:
            s = _add_slab_bias(s, causal)
        elif j == 0:
            s = _add_slab_bias(s, jnp.where(kpos0 + idx[0] * ck > qpos - WINDOW, 0.0, NEG) + gone[0])
        else:
            s = s + gone[j]
        scores.append(s)
    win = _softmax_values(scores, [(vwb_ref[idx[j]], True) for j in range(n_win)])

    for r in reps:
        rs = slice(r * slab, (r + 1) * slab)
        piece = _combine_slab(o_cmp[rs], [a[rs] for a in sel], [a[rs] for a in win],
                              _gate_columns(sg, r, n_rep, tq), own, tq)
        out_ref[:, r * GROUP_LANES:(r + 1) * GROUP_LANES] = piece.astype(out_ref.dtype)


def _cmp_to_sel(n_cmp, n_sel_blocks):
    r = SEL_BLOCK // CMP_STRIDE
    c = CMP_BLOCK // CMP_STRIDE
    m = np.zeros((LANES, LANES), np.float32)
    for j in range(n_sel_blocks):
        for a in range(r):
            for b in range(c):
                i = r * j + a - b
                if 0 <= i < n_cmp:
                    m[i, j] += 1.0
    return jnp.asarray(m, BF16)


def _block_expand(n_chunks, ck, n_sel_blocks):
    e = np.zeros((n_chunks, LANES, ck), np.float32)
    key = np.arange(n_chunks * ck).reshape(n_chunks, ck)
    for c in range(n_chunks):
        e[c, key[c] // SEL_BLOCK, np.arange(ck)] = 1.0
    e[:, n_sel_blocks:, :] = 0.0
    return jnp.asarray(e, BF16)


def _nsa_prompt(q, small, kcmp, vcmp, kv_t, win_t, *, B, T, tq, ck):
    M, QW = q.shape
    n_rep = QW // GROUP_LANES
    nqt = T // tq
    n_cmp = (T - CMP_BLOCK) // CMP_STRIDE + 1
    n_sel_blocks = -(-T // SEL_BLOCK)
    assert T % ck == 0 and ck % tq == 0 and n_sel_blocks <= LANES and kcmp.shape[1] <= LANES
    assert WINDOW % ck == 0
    mmap_t = _cmp_to_sel(n_cmp, n_sel_blocks).T[:, :kcmp.shape[1]]
    expand = _block_expand(T // ck, ck, n_sel_blocks)
    const = lambda a: pl.BlockSpec(a.shape, lambda b, t: (0,) * a.ndim)
    kv_spec = lambda slot: pl.BlockSpec((1, GROUP_LANES, T), lambda b, t: (b, slot, 0))
    chunks = lambda kdim: pltpu.VMEM((T // ck, kdim, ck), BF16)
    return pl.pallas_call(
        functools.partial(_nsa_prompt_kernel, tq=tq, ck=ck, n_cmp=n_cmp, n_sel_blocks=n_sel_blocks),
        grid=(B, nqt),
        in_specs=[pl.BlockSpec((tq, QW), lambda b, t: (b * nqt + t, 0)),
                  pl.BlockSpec((tq, LANES), lambda b, t: (b * nqt + t, 0)),
                  pl.BlockSpec((1,) + kcmp.shape[1:], lambda b, t: (b, 0, 0)),
                  pl.BlockSpec((1,) + vcmp.shape[1:], lambda b, t: (b, 0, 0)),
                  kv_spec(2), kv_spec(3), kv_spec(0), kv_spec(1),
                  const(mmap_t), const(expand)],
        out_specs=pl.BlockSpec((tq, QW), lambda b, t: (b * nqt + t, 0)),
        out_shape=jax.ShapeDtypeStruct((M, QW), BF16),
        scratch_shapes=[chunks(GROUP_LANES + LANES), chunks(GROUP_LANES), chunks(GROUP_LANES), chunks(GROUP_LANES)],
        compiler_params=_params(("arbitrary", "arbitrary")),
        name="nsa_prompt",
    )(q, small, kcmp, vcmp, kv_t, kv_t, win_t, win_t, mmap_t, expand)


def _nsa_decode_kernel(pt_ref, *refs, n_pages, page, tq, t_new, past_len, n_cmp, n_sel_blocks, wb):
    del pt_ref
    pages = refs[:n_pages]
    (q_ref, sm_ref, kvn_ref, wn_ref, cw_ref, pe_ref, w1_ref, b1_ref, w2_ref, b2_ref, perm_ref,
     mmap_ref, exp_ref, *outs) = refs[n_pages:]
    out_ref, nw_ref, newk_ref, neww_ref, bias_ref = outs[-5:]
    b = pl.program_id(0)
    n_rep = q_ref.shape[2] // GROUP_LANES
    slab = NSA_KV_HEADS * tq
    reps = range(n_rep)

    @pl.when(b == 0)
    def _():
        newk_ref[...] = jnp.zeros_like(newk_ref)
        neww_ref[...] = jnp.zeros_like(neww_ref)
        for slot in range(2):
            bias_ref[slot] = _pe_bias(slot, pe_ref, w1_ref, b1_ref)

    newk_ref[:tq, :] = kvn_ref[0]
    neww_ref[:tq, :] = wn_ref[0]

    qpos = past_len + lax.broadcasted_iota(jnp.int32, (slab, 1), 0) % tq
    qs, own = _expand_q(q_ref[0], tq)
    sg = jax.nn.sigmoid(sm_ref[0])

    cmp_kv = []
    for slot in range(2):
        xs = _deinterleave_t(perm_ref[...], [pg[0, slot] for pg in pages])
        cmp_kv.append(_compress_slot(xs, slot, bias_ref[slot, 0:1, :], w1_ref, w2_ref, b2_ref).astype(BF16))

    q_all = jnp.concatenate(qs, axis=0)
    o_cmp, p_cmp = _cmp_rows(q_all, _cmp_mask(slab, cmp_kv[0].shape[0], qpos, n_cmp), cmp_kv[0], cmp_kv[1])
    imp_rows = _dot(p_cmp, mmap_ref[...])
    imp = imp_rows[:slab]
    for r in range(1, n_rep):
        imp = imp + imp_rows[r * slab:(r + 1) * slab]
    neg_all = jnp.concatenate([_select_blocks(imp, qpos, n_sel_blocks)] * n_rep, axis=0)

    kpos0 = lax.broadcasted_iota(jnp.int32, (1, page), 1)
    own_rows = jnp.where(kpos0 + past_len <= jnp.concatenate([qpos] * n_rep, axis=0), 0.0, NEG)

    scores = [_dot(q_all, pages[p][0, 2].astype(BF16)) + _dot(neg_all, exp_ref[p]) for p in range(n_pages)]
    values = [(pages[p][0, 3].astype(BF16), True) for p in range(n_pages)]
    k_new = newk_ref[:, 2 * GROUP_LANES:3 * GROUP_LANES].astype(BF16)
    scores.append(_dot_nt(q_all, k_new) + _dot(neg_all, exp_ref[n_pages]) + own_rows)
    values.append((newk_ref[:, 3 * GROUP_LANES:].astype(BF16), False))
    sel = _softmax_values(scores, values)

    kpos = past_len - wb + lax.broadcasted_iota(jnp.int32, (1, wb), 1)
    in_band = jnp.where(kpos > jnp.concatenate([qpos] * n_rep, axis=0) - WINDOW, 0.0, NEG)
    scores = [_dot(q_all, cw_ref[0, 0].astype(BF16)) + in_band,
              _dot_nt(q_all, neww_ref[:, :GROUP_LANES].astype(BF16)) + own_rows]
    values = [(cw_ref[0, 1].astype(BF16), True), (neww_ref[:, GROUP_LANES:].astype(BF16), False)]
    win = _softmax_values(scores, values)

    for r in reps:
        rs = slice(r * slab, (r + 1) * slab)
        piece = _combine_slab(o_cmp[rs], [a[rs] for a in sel], [a[rs] for a in win],
                              _gate_columns(sg, r, n_rep, tq), own, tq)
        out_ref[0, :, r * GROUP_LANES:(r + 1) * GROUP_LANES] = piece

    new_t = neww_ref[...].T
    lane = lax.broadcasted_iota(jnp.int32, (GROUP_LANES, LANES), 1)
    for slot in range(2):
        shifted = pltpu.roll(cw_ref[0, slot], wb - t_new, axis=1)
        fresh = pltpu.roll(new_t[slot * GROUP_LANES:(slot + 1) * GROUP_LANES], LANES - t_new, axis=1)
        nw_ref[0, slot, :, :wb - LANES] = shifted[:, :wb - LANES]
        nw_ref[0, slot, :, wb - LANES:] = jnp.where(lane >= LANES - t_new, fresh, shifted[:, wb - LANES:])


def _nsa_decode(q8, small8, kvnew8, winnew8, cache, cache_win, page_table, cw, *, past_len, t_new, win_off,
                win_stack=None):
    B, tq, QW = q8.shape
    n_pages = page_table.shape[1]
    page = cache.shape[3]
    wb = cache_win.shape[3]
    tk = past_len + t_new
    n_cmp = (tk - CMP_BLOCK) // CMP_STRIDE + 1
    n_sel_blocks = -(-tk // SEL_BLOCK)
    nrow = past_len // CMP_STRIDE
    assert tq == SUBLANES and past_len == n_pages * page and wb % page == 0 and page == LANES
    assert n_cmp <= nrow <= LANES and (n_cmp - 1) * CMP_STRIDE + CMP_BLOCK <= past_len
    assert n_sel_blocks <= LANES and 0 < t_new < tq and wb == WINDOW
    mmap = _cmp_to_sel(n_cmp, n_sel_blocks)[:nrow]
    expand = _block_expand(n_pages + 1, page, n_sel_blocks)
    const = lambda a: pl.BlockSpec(a.shape, lambda b, pt: (0,) * a.ndim)
    page_spec = lambda p: pl.BlockSpec((1,) + cache.shape[1:], lambda b, pt: (pt[b, p], 0, 0, 0))
    per_b = lambda a: pl.BlockSpec((1,) + a.shape[1:], lambda b, pt: (b, 0, 0))
    win_block = (1,) + cache_win.shape[1:]
    consts = list(cw) + [mmap, expand]
    operands = [page_table] + [cache] * n_pages + [q8, small8, kvnew8, winnew8, cache_win] + consts
    extra_specs, aliases = [], {}
    if win_stack is not None:
        extra_specs = [pl.BlockSpec(memory_space=pl.ANY)]
        aliases = {len(operands): 1}
        operands.append(win_stack)
    grid_spec = pltpu.PrefetchScalarGridSpec(
        num_scalar_prefetch=1,
        grid=(B,),
        in_specs=[page_spec(p) for p in range(n_pages)]
        + [per_b(q8), per_b(small8), per_b(kvnew8), per_b(winnew8),
           pl.BlockSpec(win_block, lambda b, pt: (win_off + b, 0, 0, 0))] + [const(a) for a in consts] + extra_specs,
        out_specs=[per_b(q8), pl.BlockSpec(win_block, lambda b, pt: (win_off + b, 0, 0, 0))],
        scratch_shapes=[pltpu.VMEM((page, kvnew8.shape[2]), F32), pltpu.VMEM((page, winnew8.shape[2]), F32),
                        pltpu.VMEM((2, SUBLANES, cw[2].shape[1]), F32)],
    )
    return pl.pallas_call(
        functools.partial(_nsa_decode_kernel, n_pages=n_pages, page=page, tq=tq, t_new=t_new, past_len=past_len,
                          n_cmp=n_cmp, n_sel_blocks=n_sel_blocks, wb=wb),
        grid_spec=grid_spec,
        input_output_aliases=aliases,
        out_shape=[jax.ShapeDtypeStruct(q8.shape, F32), jax.ShapeDtypeStruct(cache_win.shape, F32)],
        compiler_params=_params(("arbitrary",)),
        name="nsa_decode",
    )(*operands)


def _layer_weights(i, prm, d_ml, d_nsa):
    H, G, dk = ML_HEADS, NSA_KV_HEADS, NSA_HEAD_DIM
    R = d_nsa // (G * dk)
    kvw = G * dk
    w_in = prm["w_in"][i]
    o = 0
    c_in, o = w_in[:, o:o + d_ml], o + d_ml
    v_ml, o = w_in[:, o:o + d_ml], o + d_ml
    o_ml, o = w_in[:, o:o + d_ml], o + d_ml
    i_ml, o = w_in[:, o:o + H], o + H
    f_ml, o = w_in[:, o:o + H], o + H
    q_n, o = w_in[:, o:o + d_nsa], o + d_nsa
    kv, o = w_in[:, o:o + 6 * kvw], o + 6 * kvw
    g_n = w_in[:, o:]
    K = w_in.shape[0]
    q_perm = q_n.reshape(K, G, R, dk).transpose(0, 2, 1, 3).reshape(K, d_nsa)
    w_main = jnp.concatenate([c_in, v_ml, o_ml, q_perm, kv], axis=1).astype(BF16)
    w_kvw_t = kv.T.astype(BF16)
    small = jnp.concatenate([i_ml, f_ml, g_n], axis=1)
    w_small = jnp.pad(small, ((0, 0), (0, LANES - small.shape[1]))).astype(BF16)
    w_out = prm["w_out"][i]
    w_out_nsa = w_out[d_ml:].reshape(G, R, dk, -1).transpose(1, 0, 2, 3).reshape(d_nsa, -1)
    w_out_p = jnp.concatenate([w_out[:d_ml], w_out_nsa], axis=0).astype(BF16)
    Dh = d_ml // H
    gate_bias = jnp.pad(jnp.concatenate([prm["b_i"][i], prm["b_f"][i]]), (0, LANES - 2 * H)).reshape(1, LANES)
    w2 = prm["cmp_w2"][i]
    w2_placed = jnp.stack([jnp.stack([jnp.pad(w2[s], ((0, 0), (g * dk, (G - 1 - g) * dk))) for g in range(G)])
                           for s in range(2)]).astype(BF16)
    perm = np.zeros((LANES, LANES), np.float32)
    tok = np.arange(LANES)
    perm[(tok % CMP_STRIDE) * (LANES // CMP_STRIDE) + tok // CMP_STRIDE, tok] = 1.0
    cmp_w = (prm["cmp_pe"][i].reshape(2, -1), prm["cmp_w1"][i].astype(BF16), prm["cmp_b1"][i], w2_placed,
             jnp.tile(prm["cmp_b2"][i], (1, G)), jnp.asarray(perm, BF16))
    return dict(
        w_main=w_main, w_kvw_t=w_kvw_t, w_small=w_small, w_out=w_out_p,
        w_up=prm["w_up_all"], w_down=prm["w_down_all"],
        w_pl=prm["w_pl"][i].astype(BF16), w_pl_gate=prm["w_pl_gate"][i].astype(BF16),
        conv_w=prm["conv_w"][i], conv_b=prm["conv_b"][i].reshape(1, -1),
        wq=prm["w_q_ml"][i].astype(BF16),
        wkt=(jnp.swapaxes(prm["w_k_ml"][i], 1, 2) * (Dh ** -0.5)).astype(BF16),
        gate_bias=gate_bias, g_ml=prm["g_ml"][i].reshape(1, -1), cmp=cmp_w)


def _pad_time(a, B, t, tp):
    return jnp.pad(a.reshape(B, t, -1), ((0, 0), (0, tp - t), (0, 0)))


def _layer(h, pl_e, i, prm, lw, mem, *, B, T):
    M, D = h.shape
    G, dk, H = NSA_KV_HEADS, NSA_HEAD_DIM, ML_HEADS
    d_ml = lw["conv_w"].shape[1]
    d_nsa = lw["w_out"].shape[0] - d_ml
    R = d_nsa // (G * dk)
    Dh = d_ml // H
    kvw = G * dk
    tm = min(M, 512)
    g_pre = prm["g_pre_mix"][i]
    lanes_last = lambda a, lead: a.reshape(lead + (G, dk, a.shape[-1]))

    ml_w = (lw["conv_w"], lw["conv_b"], lw["wq"], lw["wkt"], lw["gate_bias"], lw["g_ml"])
    if mem is None:
        cin, v, o, q, small = _norm_matmul(h, g_pre, lw["w_main"], (d_ml, d_ml, d_ml, d_nsa),
                                           w_small=lw["w_small"], tm=min(M, 1024), tn=512)
        kv_t, win_t = _norm_matmul_t(h, g_pre, lw["w_kvw_t"], (4 * kvw, 2 * kvw), B=B, T=T, tm=min(T, 1024), tn=512)
        zeros = lambda *s: jnp.zeros(s, F32)
        hm, c_new, n_new, m_new = _mlstm(
            cin, v, o, small, zeros(B, SUBLANES, d_ml), zeros(B, H, Dh, Dh), zeros(B, 1, d_ml), zeros(B, 1, LANES),
            *ml_w, B=B, rows=ML_CHUNK, valid=ML_CHUNK)
        kcmp, vcmp = _compress_prompt(kv_t, lw["cmp"], B=B, T=T)
        on = _nsa_prompt(q, small, kcmp, vcmp, kv_t, win_t, B=B, T=T, tq=64, ck=256)
        wlen = min(WINDOW, T)
        new_rows = lanes_last(kv_t, (B, 4)).transpose(0, 4, 1, 2, 3)
        new_win = lanes_last(win_t[:, :, T - wlen:], (B, 2)).transpose(0, 4, 1, 2, 3)
    else:
        cin, v, o, q, kv4, win2, small = _norm_matmul(
            h, g_pre, lw["w_main"], (d_ml, d_ml, d_ml, d_nsa, 4 * kvw, 2 * kvw),
            w_small=lw["w_small"], tm=tm, tn=512)
        tp = SUBLANES
        pad = lambda a: _pad_time(a, B, T, tp)
        prev = jnp.pad(mem["conv"], ((0, 0), (SUBLANES - mem["conv"].shape[1], 0), (0, 0)))
        m0 = jnp.pad(mem["m"], ((0, 0), (0, LANES - H))).reshape(B, 1, LANES)
        hm8, c_new, n_new, m_new = _mlstm(
            pad(cin).reshape(B * tp, d_ml), pad(v).reshape(B * tp, d_ml), pad(o).reshape(B * tp, d_ml),
            pad(small).reshape(B * tp, LANES), prev, mem["C"], mem["n"].reshape(B, 1, d_ml), m0,
            *ml_w, B=B, rows=tp, valid=T, c_off=i * B, c_layers=mem["C"].shape[0] // B, c_stack=mem["c_stack"])
        hm = hm8.reshape(B, tp, d_ml)[:, :T].reshape(M, d_ml)
        page = mem["kv"].shape[3]
        on8, new_win = _nsa_decode(
            pad(q), pad(small), pad(kv4), pad(win2),
            mem["kv"], mem["win"], mem["page_table"], lw["cmp"],
            past_len=mem["page_table"].shape[1] * page, t_new=T, win_off=i * B, win_stack=mem["win_stack"])
        on = on8[:, :T].reshape(M, d_nsa)
        new_rows = kv4.reshape(B, T, 4, G, dk)

    h = _matmul_norm_res([hm, on], lw["w_out"], h, prm["g_post_mix"][i], tm=tm, tk=d_ml + d_nsa)
    u, = _norm_matmul(h, prm["g_pre_mlp"][i], lw["w_up"], (lw["w_up"].shape[1],), tm=min(M, 1024), tn=1024,
                      act="relu2", out_dtype=BF16, layer=i)
    h = _matmul_norm_res([u], lw["w_down"], h, prm["g_post_mlp"][i], tm=tm, tk=2048, layer=i)
    h = _ple(h, pl_e, lw["w_pl_gate"], lw["w_pl"], prm["g_pl"][i], tm=min(M, 256), layer=i)

    state = (new_rows, new_win, c_new, n_new.reshape(B, H, Dh), m_new[:, 0, :H], cin.reshape(B, T, d_ml)[:, T - 3:])
    return h, state


def kernel(x_prompt, x_sample, cache_kv, cache_win, state_C, state_n, state_m, state_conv, page_table,
           p_prompt, p_sample, g_pre_mix, w_in, conv_w, conv_b, w_q_ml, w_k_ml, b_i, b_f, g_ml,
           cmp_pe, cmp_w1, cmp_b1, cmp_w2, cmp_b2, w_out, g_post_mix, g_pre_mlp, w_up, w_down,
           g_post_mlp, w_pl, g_pl, w_pl_gate):
    prm = dict(g_pre_mix=g_pre_mix, w_in=w_in, conv_w=conv_w, conv_b=conv_b, w_q_ml=w_q_ml, w_k_ml=w_k_ml,
               b_i=b_i, b_f=b_f, g_ml=g_ml, cmp_pe=cmp_pe, cmp_w1=cmp_w1, cmp_b1=cmp_b1, cmp_w2=cmp_w2,
               cmp_b2=cmp_b2, w_out=w_out, g_post_mix=g_post_mix, g_pre_mlp=g_pre_mlp, w_up=w_up,
               w_down=w_down, g_post_mlp=g_post_mlp, w_pl=w_pl, g_pl=g_pl, w_pl_gate=w_pl_gate)
    Bp, Tp, D = x_prompt.shape
    Bs, Ts, _ = x_sample.shape
    depth = w_in.shape[0]
    d_ml = conv_w.shape[2]
    d_nsa = w_out.shape[1] - d_ml
    hp = x_prompt.reshape(Bp * Tp, D)
    hs = x_sample.reshape(Bs * Ts, D)
    n_pool, page = cache_kv.shape[1:3]
    kv_t = jnp.transpose(cache_kv, (0, 1, 3, 4, 5, 2)).reshape(depth * n_pool, cache_kv.shape[3], -1, page)
    win_t = jnp.transpose(cache_win, (0, 1, 3, 4, 5, 2)).reshape(depth * Bs, cache_win.shape[3], -1, cache_win.shape[2])
    c_all = state_C.reshape((depth * Bs,) + state_C.shape[2:])
    prm["w_up_all"] = w_up.astype(BF16).reshape(-1, w_up.shape[2])
    prm["w_down_all"] = w_down.astype(BF16).reshape(-1, w_down.shape[2])
    sp, ss = [], []
    c_stack = win_stack = None
    for i in range(depth):
        lw = _layer_weights(i, prm, d_ml, d_nsa)
        mem = dict(kv=kv_t, page_table=page_table + i * n_pool, win=win_t, C=c_all, n=state_n[i],
                   m=state_m[i], conv=state_conv[i], c_stack=c_stack, win_stack=win_stack)
        hp, st_p = _layer(hp, p_prompt.reshape(depth * Bp * Tp, -1), i, prm, lw, None, B=Bp, T=Tp)
        hs, st_s = _layer(hs, p_sample.reshape(depth * Bs * Ts, -1), i, prm, lw, mem, B=Bs, T=Ts)
        win_stack, c_stack = st_s[1], st_s[2]
        sp.append(st_p)
        ss.append(st_s)

    stk = lambda lst, j: jnp.stack([s[j] for s in lst])
    G, dk = NSA_KV_HEADS, NSA_HEAD_DIM
    win_sample = win_stack.reshape(depth, Bs, win_stack.shape[1], G, dk, win_stack.shape[3]).transpose(0, 1, 5, 2, 3, 4)
    return (hp.reshape(Bp, Tp, D), hs.reshape(Bs, Ts, D), stk(sp, 0), stk(ss, 0), stk(sp, 1), win_sample,
            stk(sp, 2), c_stack.reshape(state_C.shape), stk(sp, 3), stk(ss, 3), stk(sp, 4), stk(ss, 4),
            stk(sp, 5), stk(ss, 5))
```

```python
import functools

import numpy as np
import jax
import jax.numpy as jnp
from jax import lax
from jax.experimental import pallas as pl
from jax.experimental.pallas import tpu as pltpu

ML_HEADS = 4
NSA_HEAD_DIM = 64
NSA_KV_HEADS = 4
CMP_BLOCK = 32
CMP_STRIDE = 16
SEL_BLOCK = 64
N_SEL = 16
WINDOW = 512
EPS = 1e-6
NEG = -1e30
FORCE = 1e6

LANES = 128
SUBLANES = 8
VMEM_LIMIT_BYTES = 56 * 1024 * 1024

ML_CHUNK = 128
MLSTM_SEQS_PER_STEP_PROMPT = 2
MLSTM_SEQS_PER_STEP_DECODE = 4
GROUP_LANES = NSA_KV_HEADS * NSA_HEAD_DIM
QK_SCALE = NSA_HEAD_DIM ** -0.5 * 1.4426950408889634
MASK_BIG = 1e30

F32 = jnp.float32
BF16 = jnp.bfloat16


def _dot(a, b):
    return jnp.dot(a, b, preferred_element_type=F32)


def _dot_nt(a, b):
    return lax.dot_general(a, b, (((1,), (1,)), ((), ())), preferred_element_type=F32)


def _params(semantics):
    return pltpu.CompilerParams(dimension_semantics=semantics, vmem_limit_bytes=VMEM_LIMIT_BYTES)


def _rms_rows(x):
    return x * lax.rsqrt(jnp.mean(x * x, axis=-1, keepdims=True) + EPS)


def _norm_matmul_kernel(x_ref, g_ref, w_ref, *rest, seg_tiles, has_small, act):
    if has_small:
        ws_ref, rest = rest[0], rest[1:]
    n_out = len(seg_tiles) + (1 if has_small else 0)
    out_refs, xn_ref = rest[:n_out], rest[n_out]
    j = pl.program_id(1)

    @pl.when(j == 0)
    def _():
        xn = (_rms_rows(x_ref[...]) * g_ref[...]).astype(BF16)
        xn_ref[...] = xn
        if has_small:
            out_refs[-1][...] = _dot(xn, ws_ref[...])

    y = _dot(xn_ref[...], w_ref[...])
    if act == "relu2":
        y = jnp.square(jnp.maximum(y, 0.0))
    start = 0
    for k, n in enumerate(seg_tiles):
        @pl.when((j >= start) & (j < start + n))
        def _(k=k):
            out_refs[k][...] = y.astype(out_refs[k].dtype)
        start += n


def _norm_matmul(x, g, w, seg_cols, *, w_small=None, tm, tn, act=None, out_dtype=F32, layer=0):
    M, K = x.shape
    seg_tiles = tuple(c // tn for c in seg_cols)
    assert all(c % tn == 0 for c in seg_cols) and M % tm == 0
    has_small = w_small is not None
    starts = np.concatenate([[0], np.cumsum(seg_tiles)[:-1]]).tolist()

    def out_map(start, n):
        return lambda i, j: (i, jnp.clip(j - start, 0, n - 1))

    in_specs = [pl.BlockSpec((tm, K), lambda i, j: (i, 0)),
                pl.BlockSpec((1, K), lambda i, j: (0, 0)),
                pl.BlockSpec((K, tn), lambda i, j: (layer, j))]
    args = [x, g.reshape(1, K), w]
    out_specs = [pl.BlockSpec((tm, tn), out_map(s, n)) for s, n in zip(starts, seg_tiles)]
    out_shape = [jax.ShapeDtypeStruct((M, c), out_dtype) for c in seg_cols]
    if has_small:
        in_specs.append(pl.BlockSpec((K, LANES), lambda i, j: (0, 0)))
        args.append(w_small)
        out_specs.append(pl.BlockSpec((tm, LANES), lambda i, j: (i, 0)))
        out_shape.append(jax.ShapeDtypeStruct((M, LANES), F32))
    return pl.pallas_call(
        functools.partial(_norm_matmul_kernel, seg_tiles=seg_tiles, has_small=has_small, act=act),
        grid=(M // tm, sum(seg_tiles)),
        in_specs=in_specs, out_specs=out_specs, out_shape=out_shape,
        scratch_shapes=[pltpu.VMEM((tm, K), BF16)],
        compiler_params=_params(("parallel", "arbitrary")),
        name="norm_matmul",
    )(*args)


def _norm_matmul_t_kernel(x_ref, g_ref, wt_ref, *rest, seg_tiles):
    out_refs, xn_ref = rest[:-1], rest[-1]
    j = pl.program_id(1)

    @pl.when(j == 0)
    def _():
        xn_ref[...] = (_rms_rows(x_ref[...]) * g_ref[...]).astype(BF16)

    y = _dot_nt(wt_ref[...], xn_ref[...])
    start = 0
    for k, n in enumerate(seg_tiles):
        @pl.when((j >= start) & (j < start + n))
        def _(k=k):
            out_refs[k][0] = y
        start += n


def _norm_matmul_t(x, g, wt, seg_rows, *, B, T, tm, tn):
    M, K = x.shape
    seg_tiles = tuple(r // tn for r in seg_rows)
    assert T % tm == 0 and all(r % tn == 0 for r in seg_rows) and M == B * T
    per = T // tm
    starts = np.concatenate([[0], np.cumsum(seg_tiles)[:-1]]).tolist()

    def out_map(start, n):
        return lambda i, j: (i // per, jnp.clip(j - start, 0, n - 1), i % per)

    return pl.pallas_call(
        functools.partial(_norm_matmul_t_kernel, seg_tiles=seg_tiles),
        grid=(M // tm, sum(seg_tiles)),
        in_specs=[pl.BlockSpec((tm, K), lambda i, j: (i, 0)),
                  pl.BlockSpec((1, K), lambda i, j: (0, 0)),
                  pl.BlockSpec((tn, K), lambda i, j: (j, 0))],
        out_specs=[pl.BlockSpec((1, tn, tm), out_map(s, n)) for s, n in zip(starts, seg_tiles)],
        out_shape=[jax.ShapeDtypeStruct((B, r, T), F32) for r in seg_rows],
        scratch_shapes=[pltpu.VMEM((tm, K), BF16)],
        compiler_params=_params(("parallel", "arbitrary")),
        name="norm_matmul_t",
    )(x, g.reshape(1, K), wt)


def _matmul_norm_res_kernel(*refs, n_a):
    a_refs = refs[:n_a]
    w_ref, h_ref, g_ref, o_ref, acc_ref = refs[n_a:]
    k = pl.program_id(1)

    @pl.when(k == 0)
    def _():
        acc_ref[...] = jnp.zeros_like(acc_ref)

    off = 0
    for a_ref in a_refs:
        width = a_ref.shape[1]
        acc_ref[...] += _dot(a_ref[...].astype(BF16), w_ref[off:off + width, :])
        off += width

    @pl.when(k == pl.num_programs(1) - 1)
    def _():
        o_ref[...] = h_ref[...] + _rms_rows(acc_ref[...]) * g_ref[...]


def _matmul_norm_res(a_parts, w, h, g, *, tm, tk, layer=0):
    M = a_parts[0].shape[0]
    K = sum(a.shape[1] for a in a_parts)
    N = w.shape[1]
    assert M % tm == 0 and K % tk == 0 and (len(a_parts) == 1 or tk == K)
    k_steps = K // tk
    a_specs = [pl.BlockSpec((tm, tk if len(a_parts) == 1 else a.shape[1]), lambda i, k: (i, k)) for a in a_parts]
    return pl.pallas_call(
        functools.partial(_matmul_norm_res_kernel, n_a=len(a_parts)),
        grid=(M // tm, k_steps),
        in_specs=a_specs + [pl.BlockSpec((tk, N), lambda i, k: (layer * k_steps + k, 0)),
                            pl.BlockSpec((tm, N), lambda i, k: (i, 0)),
                            pl.BlockSpec((1, N), lambda i, k: (0, 0))],
        out_specs=pl.BlockSpec((tm, N), lambda i, k: (i, 0)),
        out_shape=jax.ShapeDtypeStruct((M, N), F32),
        scratch_shapes=[pltpu.VMEM((tm, N), F32)],
        compiler_params=_params(("parallel", "arbitrary")),
        name="matmul_norm_res",
    )(*a_parts, w, h, g.reshape(1, N))


def _ple_kernel(h_ref, p_ref, wg_ref, wp_ref, g_ref, o_ref):
    h = h_ref[...]
    gate = jax.nn.sigmoid(_dot(_rms_rows(h).astype(BF16), wg_ref[...]))
    e = _dot(p_ref[...].astype(BF16), wp_ref[...])
    o_ref[...] = h + gate * (_rms_rows(e) * g_ref[...])


def _ple(h, p, wg, wp, g, *, tm, layer=0):
    M, N = h.shape
    P = p.shape[1]
    assert M % tm == 0
    return pl.pallas_call(
        _ple_kernel,
        grid=(M // tm,),
        in_specs=[pl.BlockSpec((tm, N), lambda i: (i, 0)),
                  pl.BlockSpec((tm, P), lambda i: (layer * (M // tm) + i, 0)),
                  pl.BlockSpec((N, N), lambda i: (0, 0)),
                  pl.BlockSpec((P, N), lambda i: (0, 0)),
                  pl.BlockSpec((1, N), lambda i: (0, 0))],
        out_specs=pl.BlockSpec((tm, N), lambda i: (i, 0)),
        out_shape=jax.ShapeDtypeStruct((M, N), F32),
        compiler_params=_params(("parallel",)),
        name="ple",
    )(h, p, wg, wp, g.reshape(1, N))


def _shift_rows(x, prev8, s):
    rows = x.shape[0]
    xs = pltpu.roll(x, s, axis=0)
    rid = lax.broadcasted_iota(jnp.int32, (SUBLANES, x.shape[1]), 0)
    head = jnp.where(rid < s, pltpu.roll(prev8, s, axis=0), xs[:SUBLANES])
    if rows == SUBLANES:
        return head
    return jnp.concatenate([head, xs[SUBLANES:]], axis=0)


def _mlstm_kernel(*refs, nb, aliased, **static):
    n_seq, n_shared = 8, 7
    seq_in, shared = refs[:n_seq], refs[n_seq:n_seq + n_shared]
    rest = refs[n_seq + n_shared + (1 if aliased else 0):]
    per_seq = [[r.at[j] for r in seq_in] + list(shared) + [r.at[j] for r in rest] for j in range(nb)]
    for args in per_seq:
        _mlstm_load_state(*args, **static)
    m_rows = [None] * nb
    running = {j: _mlstm_seq(*args, **static) for j, args in enumerate(per_seq)}
    while running:
        for j in list(running):
            try:
                next(running[j])
            except StopIteration as done:
                m_rows[j] = done.value
                del running[j]
    for args, m_row in zip(per_seq, m_rows):
        _mlstm_store_state(*args, m_row, **static)


def _mlstm_load_state(cin_ref, v_ref, o_ref, sm_ref, prev_ref, c0_ref, n0_ref, m0_ref,
                      cw_ref, cb_ref, wq_ref, wkt_ref, gb_ref, gml_ref, ltri_ref,
                      hm_ref, cout_ref, nout_ref, mout_ref,
                      carry_ref, cext_ref, m_ref, chpad_ref, vpad_ref, gpad_ref,
                      *, L, rows, valid, H, Dh):
    @pl.when(pl.program_id(1) == 0)
    def _():
        carry_ref[...] = prev_ref[...]
        m_ref[...] = m0_ref[...]
        for h in range(H):
            cext_ref[h, :, :Dh] = c0_ref[h]
            n_row = n0_ref[:, h * Dh:(h + 1) * Dh]
            cext_ref[h, :, Dh:] = jnp.broadcast_to(n_row, (LANES, Dh)).T

    if rows < L:
        @pl.when((pl.program_id(0) == 0) & (pl.program_id(1) == 0))
        def _():
            chpad_ref[...] = jnp.zeros_like(chpad_ref)
            vpad_ref[...] = jnp.zeros_like(vpad_ref)
            gpad_ref[...] = jnp.zeros_like(gpad_ref)


def _mlstm_store_state(cin_ref, v_ref, o_ref, sm_ref, prev_ref, c0_ref, n0_ref, m0_ref,
                       cw_ref, cb_ref, wq_ref, wkt_ref, gb_ref, gml_ref, ltri_ref,
                       hm_ref, cout_ref, nout_ref, mout_ref,
                       carry_ref, cext_ref, m_ref, chpad_ref, vpad_ref, gpad_ref, m_new_row,
                       *, L, rows, valid, H, Dh):
    @pl.when(pl.program_id(1) == pl.num_programs(1) - 1)
    def _():
        mout_ref[...] = m_new_row
        for h in range(H):
            cout_ref[h] = cext_ref[h, :, :Dh]
            nout_ref[:, h * Dh:(h + 1) * Dh] = cext_ref[h, :, Dh:].T[0:1, :]


def _mlstm_seq(cin_ref, v_ref, o_ref, sm_ref, prev_ref, c0_ref, n0_ref, m0_ref,
               cw_ref, cb_ref, wq_ref, wkt_ref, gb_ref, gml_ref, ltri_ref,
               hm_ref, cout_ref, nout_ref, mout_ref,
               carry_ref, cext_ref, m_ref, chpad_ref, vpad_ref, gpad_ref,
               *, L, rows, valid, H, Dh):
    x = cin_ref[...]
    prev8 = carry_ref[...]
    conv = cb_ref[...] + cw_ref[3:4, :] * x
    for s in (1, 2, 3):
        conv = conv + cw_ref[3 - s:4 - s, :] * _shift_rows(x, prev8, s)
    carry_ref[...] = x[rows - SUBLANES:, :]
    ch = conv * jax.nn.sigmoid(conv)

    if rows < L:
        chpad_ref[:rows, :] = ch
        vpad_ref[:rows, :] = v_ref[...]
        gpad_ref[:rows, :] = sm_ref[...]
        ch, v_all, sm = chpad_ref[...], vpad_ref[...], gpad_ref[...]
    else:
        v_all, sm = v_ref[...], sm_ref[...]
    ch = ch.astype(BF16)

    lane = lax.broadcasted_iota(jnp.int32, (L, LANES), 1)
    rid = lax.broadcasted_iota(jnp.int32, (L, LANES), 0)
    pre = sm + gb_ref[...]
    gates = jnp.where(lane < H, pre, jax.nn.log_sigmoid(pre))
    gates = jnp.where(rid < valid, gates, jnp.where(lane < H, NEG, 0.0))
    hi = gates.astype(BF16)
    r1 = gates - hi.astype(F32)
    mid = r1.astype(BF16)
    lo = (r1 - mid.astype(F32)).astype(BF16)
    ltri = ltri_ref[...]
    bc_all = _dot(ltri, hi) + _dot(ltri, mid) + _dot(ltri, lo)
    colform = jnp.where(lane < H, gates, bc_all)
    rowform = colform.T

    r_i = lax.broadcasted_iota(jnp.int32, (L, L), 0)
    c_i = lax.broadcasted_iota(jnp.int32, (L, L), 1)
    causal = c_i <= r_i
    lane1 = lax.broadcasted_iota(jnp.int32, (1, LANES), 1)
    m_row = m_ref[...]
    m_new_row = m_row
    ones = jnp.ones((L, LANES), F32)

    yield
    heads = range(H)
    hsl = [slice(h * Dh, (h + 1) * Dh) for h in heads]
    q = [_dot(ch[:, hsl[h]], wq_ref[h]).astype(BF16) for h in heads]
    kt = [_dot_nt(wkt_ref[h], ch[:, hsl[h]]).astype(BF16) for h in heads]
    cext = [cext_ref[h] for h in heads]
    qc = [_dot(q[h], cext[h].astype(BF16)) for h in heads]
    qk = [_dot(q[h], kt[h]) for h in heads]
    yield

    bc_col = [colform[:, H + h:H + h + 1] for h in heads]
    m_h = [m_row[:, h:h + 1] for h in heads]
    mt, s_w, a_int = [], [], []
    for h in heads:
        logd = jnp.where(causal, bc_col[h] - rowform[H + h:H + h + 1, :] + rowform[h:h + 1, :], NEG)
        inter = m_h[h] + bc_col[h]
        mt.append(jnp.maximum(jnp.max(logd, axis=-1, keepdims=True), inter))
        s_w.append((qk[h] * jnp.exp(logd - mt[h])).astype(BF16))
        a_int.append(jnp.exp(inter - mt[h]))
    v_ext = [jnp.concatenate([v_all[:, hsl[h]], ones], axis=1) for h in heads]
    sv = [_dot(s_w[h], v_ext[h].astype(BF16)) for h in heads]
    yield

    for h in heads:
        m_last = mt[h][L - 1:L, :]
        bc_last = bc_col[h][L - 1:L, :]
        w_col = jnp.exp(bc_last - bc_col[h] + colform[:, h:h + 1] - m_last)
        dec = jnp.exp(m_h[h] + bc_last - m_last)
        cext_ref[h] = dec * cext[h] + _dot(kt[h], (w_col * v_ext[h]).astype(BF16))
        m_new_row = jnp.where(lane1 == h, m_last, m_new_row)
    yield

    for h in heads:
        tot = sv[h] + a_int[h] * qc[h]
        hh = tot[:, :Dh] / jnp.maximum(jnp.abs(tot[:, Dh:Dh + 1]), jnp.exp(-mt[h]))
        y = _rms_rows(hh) * gml_ref[:, hsl[h]]
        hm_ref[:, hsl[h]] = y[:rows] * jax.nn.sigmoid(o_ref[:, hsl[h]])

    m_ref[...] = m_new_row
    return m_new_row


def _mlstm(cin, v, o, small, prev, c0, n0, m0, cw, cb, wq, wkt, gb, gml, *, B, rows, valid, c_off=0,
           c_layers=1, c_stack=None):
    L = ML_CHUNK
    H = ML_HEADS
    D = cin.shape[1]
    Dh = D // H
    nc = cin.shape[0] // (B * rows)
    nb = min(B, MLSTM_SEQS_PER_STEP_PROMPT if rows == L else MLSTM_SEQS_PER_STEP_DECODE)
    assert rows == L or (nc == 1 and rows == SUBLANES)
    assert B % nb == 0 and c_off % nb == 0
    ltri = jnp.asarray(np.tril(np.ones((L, L), np.float32)), BF16)
    seq3 = lambda a: a.reshape(B, nc * rows, a.shape[1])
    tok = lambda g, c: (g, c, 0)
    per_g3 = lambda g, c: (g, 0, 0)
    per_g4 = lambda g, c: (g, 0, 0, 0)
    const2 = lambda g, c: (0, 0)
    const3 = lambda g, c: (0, 0, 0)
    operands = [seq3(cin), seq3(v), seq3(o), seq3(small), prev, c0, n0, m0, cw, cb, wq, wkt, gb, gml, ltri]
    stacked = c_layers > 1
    aliased = stacked and c_stack is not None
    extra_specs, aliases = [], {}
    if aliased:
        extra_specs = [pl.BlockSpec(memory_space=pl.ANY)]
        aliases = {len(operands): 1}
        operands.append(c_stack)
    c_out_map = (lambda g, c: (c_off // nb + g, 0, 0, 0)) if stacked else per_g4
    hm, cout, nout, mout = pl.pallas_call(
        functools.partial(_mlstm_kernel, nb=nb, aliased=aliased, L=L, rows=rows, valid=valid, H=H, Dh=Dh),
        grid=(B // nb, nc),
        input_output_aliases=aliases,
        in_specs=[pl.BlockSpec((nb, rows, D), tok), pl.BlockSpec((nb, rows, D), tok), pl.BlockSpec((nb, rows, D), tok),
                  pl.BlockSpec((nb, rows, LANES), tok),
                  pl.BlockSpec((nb, SUBLANES, D), per_g3),
                  pl.BlockSpec((nb, H, Dh, Dh), lambda g, c: (c_off // nb + g, 0, 0, 0)),
                  pl.BlockSpec((nb, 1, D), per_g3),
                  pl.BlockSpec((nb, 1, LANES), per_g3),
                  pl.BlockSpec((4, D), const2), pl.BlockSpec((1, D), const2),
                  pl.BlockSpec((H, Dh, Dh), const3), pl.BlockSpec((H, Dh, Dh), const3),
                  pl.BlockSpec((1, LANES), const2), pl.BlockSpec((1, D), const2),
                  pl.BlockSpec((L, L), const2)] + extra_specs,
        out_specs=[pl.BlockSpec((nb, rows, D), tok),
                   pl.BlockSpec((nb, H, Dh, Dh), c_out_map),
                   pl.BlockSpec((nb, 1, D), per_g3),
                   pl.BlockSpec((nb, 1, LANES), per_g3)],
        out_shape=[jax.ShapeDtypeStruct((B, nc * rows, D), F32),
                   jax.ShapeDtypeStruct((c_layers * B, H, Dh, Dh), F32),
                   jax.ShapeDtypeStruct((B, 1, D), F32),
                   jax.ShapeDtypeStruct((B, 1, LANES), F32)],
        scratch_shapes=[pltpu.VMEM((nb, SUBLANES, D), F32),
                        pltpu.VMEM((nb, H, Dh, Dh + LANES), F32),
                        pltpu.VMEM((nb, 1, LANES), F32),
                        pltpu.VMEM((nb, L, D), F32), pltpu.VMEM((nb, L, D), F32), pltpu.VMEM((nb, L, LANES), F32)],
        compiler_params=_params(("arbitrary", "arbitrary")),
        name="mlstm",
    )(*operands)
    return hm.reshape(cin.shape), cout, nout, mout


def _deinterleave_t(perm, groups):
    per = LANES // CMP_STRIDE
    ys = [_dot_nt(perm, x_t.astype(BF16)) for x_t in groups]
    return [jnp.concatenate([y[l * per:(l + 1) * per] for y in ys], axis=0) for l in range(CMP_STRIDE)]


def _pe_bias(slot, pe_ref, w1_ref, b1_ref):
    pe = jnp.broadcast_to(pe_ref[slot:slot + 1, :], (SUBLANES, pe_ref.shape[1])).astype(BF16)
    return _dot(pe, w1_ref[slot]) + b1_ref[slot:slot + 1, :]


def _compress_slot(xs, slot, bias, w1_ref, w2_ref, b2_ref):
    dk = NSA_HEAD_DIM
    half = CMP_BLOCK // 2
    nrow = xs[0].shape[0]
    pack = GROUP_LANES // dk
    G = NSA_KV_HEADS
    first = None
    second = None
    for l0 in range(0, half, pack):
        xg = jnp.concatenate(
            [jnp.concatenate([xs[l0 + j][:, g * dk:(g + 1) * dk] for j in range(pack)], axis=1) for g in range(G)],
            axis=0).astype(BF16)
        a = _dot(xg, w1_ref[slot, l0 * dk:(l0 + pack) * dk, :])
        bb = _dot(xg, w1_ref[slot, (half + l0) * dk:(half + l0 + pack) * dk, :])
        first = a if first is None else first + a
        second = bb if second is None else second + bb
    hid = jax.nn.gelu(first + pltpu.roll(second, G * nrow - 1, axis=0) + bias).astype(BF16)
    out = None
    for g in range(G):
        og = _dot(hid[g * nrow:(g + 1) * nrow], w2_ref[slot, g])
        out = og if out is None else out + og
    return out + b2_ref[slot:slot + 1, :]


def _expand_q(q, tq):
    slab = NSA_KV_HEADS * tq
    lane_g = lax.broadcasted_iota(jnp.int32, (slab, GROUP_LANES), 1) // NSA_HEAD_DIM
    row_g = lax.broadcasted_iota(jnp.int32, (slab, GROUP_LANES), 0) // tq
    own = lane_g == row_g
    n_rep = q.shape[1] // GROUP_LANES
    slabs = []
    for r in range(n_rep):
        qr = q[:, r * GROUP_LANES:(r + 1) * GROUP_LANES] * QK_SCALE
        slabs.append(jnp.where(own, jnp.concatenate([qr] * NSA_KV_HEADS, axis=0), 0.0).astype(BF16))
    return slabs, own


def _attend(carry, q, chunks):
    m, l, acc = carry
    scores = []
    for k, _, bias in chunks:
        s = _dot(q, k)
        if bias is not None:
            s = _add_slab_bias(s, bias) if jnp.ndim(bias) == 2 else s + bias
        scores.append(s)
    mx = scores[0]
    for s in scores[1:]:
        mx = jnp.maximum(mx, s)
    m_new = jnp.maximum(m, jnp.max(mx, axis=1, keepdims=True))
    alpha = jnp.exp2(m - m_new)
    ps, pv = None, None
    for s, (_, v, _) in zip(scores, chunks):
        p = jnp.exp2(s - m_new)
        ps = p if ps is None else ps + p
        d = _dot_nt(p.astype(BF16), v)
        pv = d if pv is None else pv + d
    return m_new, alpha * l + jnp.sum(ps, axis=1, keepdims=True), alpha * acc + pv


def _add_slab_bias(s, bias):
    rows, ck = s.shape
    return (s.reshape(rows // bias.shape[0], bias.shape[0], ck) + bias[None]).reshape(rows, ck)


def _softmax_values(scores, values):
    mx = scores[0]
    for s in scores[1:]:
        mx = jnp.maximum(mx, s) if s.shape == mx.shape else mx
    m = mx.max(axis=1, keepdims=True)
    for s in scores[1:]:
        if s.shape != mx.shape:
            m = jnp.maximum(m, s.max(axis=1, keepdims=True))
    l, acc, ps = None, None, None
    for s, (v, v_t) in zip(scores, values):
        p = jnp.exp2(s - m)
        if p.shape == mx.shape:
            ps = p if ps is None else ps + p
        else:
            ls = p.sum(axis=1, keepdims=True)
            l = ls if l is None else l + ls
        pv = _dot_nt(p.astype(BF16), v) if v_t else _dot(p.astype(BF16), v)
        acc = pv if acc is None else acc + pv
    ls = ps.sum(axis=1, keepdims=True)
    return m, (ls if l is None else l + ls), acc


def _attend_init(rows):
    return (jnp.full((rows, 1), NEG, F32), jnp.zeros((rows, 1), F32), jnp.zeros((rows, GROUP_LANES), F32))


def _cmp_rows(q, mask, kcmp, vcmp):
    rows, slab = q.shape[0], mask.shape[0]
    s = _dot_nt(q, kcmp).reshape(rows // slab, slab, mask.shape[1])
    s = jnp.where(mask[None], s, NEG)
    e = jnp.where(mask[None], jnp.exp2(s - jnp.max(s, axis=2, keepdims=True)), 0.0)
    p = e * (1.0 / jnp.maximum(jnp.sum(e, axis=2, keepdims=True), 1e-30))
    p = p.reshape(rows, mask.shape[1]).astype(BF16)
    return _dot(p, vcmp), p


def _cmp_mask(rows, n_rows_cmp, qpos_col, n_cmp):
    lane = lax.broadcasted_iota(jnp.int32, (rows, n_rows_cmp), 1)
    return (lane * CMP_STRIDE + (CMP_BLOCK - 1) <= qpos_col) & (lane < n_cmp)


def _block_scores(imp, blk, cur, n_sel_blocks):
    forced = (blk == 0) | (blk == cur) | (blk == cur - 1)
    score = jnp.where(blk <= cur, jnp.where(forced, FORCE, imp), -1.0)
    return jnp.where(blk < n_sel_blocks, score, -2.0)


def _select_blocks(imp, qpos_col, n_sel_blocks):
    rows = imp.shape[0]
    blk = lax.broadcasted_iota(jnp.int32, (rows, LANES), 1)
    score = _block_scores(imp, blk, qpos_col // SEL_BLOCK, n_sel_blocks)
    rank = jnp.zeros((rows, LANES), F32)
    for j in range(n_sel_blocks):
        col = score[:, j:j + 1]
        ge = jnp.where(col >= score, 1.0, 0.0)
        gt = jnp.where(col > score, 1.0, 0.0)
        rank = rank + jnp.where(blk > j, ge, gt)
    chosen = jnp.where(rank < float(min(N_SEL, n_sel_blocks)), score, -1.0) >= 0.0
    return jnp.where(chosen, 0.0, -MASK_BIG).astype(BF16)


def _select_blocks_t(imp_t, qpos_row, n_sel_blocks):
    nb = -(-n_sel_blocks // SUBLANES) * SUBLANES
    cols = imp_t.shape[1]
    blk = lax.broadcasted_iota(jnp.int32, (nb, cols), 0)
    score = _block_scores(imp_t[:nb], blk, qpos_row // SEL_BLOCK, n_sel_blocks)
    rank = jnp.zeros((nb, cols), F32)
    for j in range(n_sel_blocks):
        row = score[j:j + 1, :]
        ge = jnp.where(row >= score, 1.0, 0.0)
        gt = jnp.where(row > score, 1.0, 0.0)
        rank = rank + jnp.where(blk > j, ge, gt)
    chosen = jnp.where(rank < float(min(N_SEL, n_sel_blocks)), score, -1.0) >= 0.0
    neg_t = jnp.where(chosen, 0.0, -MASK_BIG)
    neg_t = jnp.concatenate([neg_t, jnp.zeros((LANES - nb, cols), F32)], axis=0)
    return neg_t.T.astype(BF16)


def _gate_columns(sg, r, n_rep, tq):
    lane = lambda g, c: 2 * ML_HEADS + (g * n_rep + r) * 3 + c
    return [jnp.concatenate([sg[:, lane(g, c):lane(g, c) + 1] for g in range(NSA_KV_HEADS)], axis=0)
            for c in range(3)]


def _gated_fold(terms, own, tq):
    comb = None
    for gate, branch in terms:
        if isinstance(branch, (tuple, list)):
            _, l, acc = branch
            term = (gate * (1.0 / jnp.maximum(l, 1e-30))) * acc
        else:
            term = gate * branch
        comb = term if comb is None else comb + term
    kept = jnp.where(own, comb, 0.0)
    piece = kept[:tq]
    for g in range(1, NSA_KV_HEADS):
        piece = piece + kept[g * tq:(g + 1) * tq]
    return piece


def _combine_slab(o_cmp, sel, win, gates, own, tq):
    return _gated_fold([(gates[0], o_cmp), (gates[1], sel), (gates[2], win)], own, tq)


def _compress_prompt_kernel(kc_ref, vc_ref, pe_ref, w1_ref, b1_ref, w2_ref, b2_ref, perm_ref, ko_ref, vo_ref):
    T = kc_ref.shape[2]
    for slot, (src, dst) in enumerate(((kc_ref, ko_ref), (vc_ref, vo_ref))):
        bias = _pe_bias(slot, pe_ref, w1_ref, b1_ref)[0:1]
        xs = _deinterleave_t(perm_ref[...], [src[0, :, t * LANES:(t + 1) * LANES] for t in range(T // LANES)])
        dst[0] = _compress_slot(xs, slot, bias, w1_ref, w2_ref, b2_ref)


def _compress_prompt(kvw_t, cw, *, B, T):
    nrow = T // CMP_STRIDE
    assert (T - CMP_BLOCK) // CMP_STRIDE + 1 <= nrow and T % LANES == 0
    full = lambda a: pl.BlockSpec(a.shape, lambda b: (0,) * a.ndim)
    return pl.pallas_call(
        _compress_prompt_kernel,
        grid=(B,),
        in_specs=[pl.BlockSpec((1, GROUP_LANES, T), lambda b: (b, 0, 0)),
                  pl.BlockSpec((1, GROUP_LANES, T), lambda b: (b, 1, 0))] + [full(a) for a in cw],
        out_specs=[pl.BlockSpec((1, nrow, GROUP_LANES), lambda b: (b, 0, 0))] * 2,
        out_shape=[jax.ShapeDtypeStruct((B, nrow, GROUP_LANES), F32)] * 2,
        compiler_params=_params(("parallel",)),
        name="compress_prompt",
    )(kvw_t, kvw_t, *cw)


def _nsa_prompt_kernel(q_ref, sm_ref, kcmp_ref, vcmp_ref, ks_ref, vs_ref, kw_ref, vw_ref,
                       mmap_t_ref, exp_ref, out_ref,
                       kx_ref, vsb_ref, kwb_ref, vwb_ref, *, tq, ck, n_cmp, n_sel_blocks):
    qt = pl.program_id(1)
    n_rep = q_ref.shape[1] // GROUP_LANES
    slab = NSA_KV_HEADS * tq
    n_chunks = kx_ref.shape[0]

    @pl.when(qt == 0)
    def _():
        for c in range(n_chunks):
            cs = slice(c * ck, (c + 1) * ck)
            kx_ref[c, :GROUP_LANES, :] = ks_ref[0, :, cs].astype(BF16)
            kx_ref[c, GROUP_LANES:, :] = exp_ref[c]
            vsb_ref[c] = vs_ref[0, :, cs].astype(BF16)
            kwb_ref[c] = kw_ref[0, :, cs].astype(BF16)
            vwb_ref[c] = vw_ref[0, :, cs].astype(BF16)

    t0 = qt * tq
    qpos = t0 + lax.broadcasted_iota(jnp.int32, (slab, 1), 0) % tq
    qpos_row = t0 + lax.broadcasted_iota(jnp.int32, (1, slab), 1) % tq
    qs, own = _expand_q(q_ref[...], tq)
    sg = jax.nn.sigmoid(sm_ref[...])
    reps = range(n_rep)

    q_all = jnp.concatenate(qs, axis=0)
    kpos0 = lax.broadcasted_iota(jnp.int32, (1, ck), 1)
    c_diag = t0 // ck
    causal = jnp.where(kpos0 + c_diag * ck <= qpos, 0.0, NEG)

    kcmp, vcmp = kcmp_ref[0].astype(BF16), vcmp_ref[0].astype(BF16)
    o_cmp, p_cmp = _cmp_rows(q_all, _cmp_mask(slab, kcmp.shape[0], qpos, n_cmp), kcmp, vcmp)
    imp_rows_t = _dot_nt(mmap_t_ref[...], p_cmp)

    n_win = -(-(WINDOW - 1) // ck) + 1
    idx = [jnp.maximum(c_diag - (n_win - 1 - j), 0) for j in range(n_win)]
    gone = [jnp.where(c_diag >= n_win - 1 - j, 0.0, NEG) for j in range(n_win)]
    scores = []
    for j in range(n_win):
        s = _dot(q_all, kwb_ref[idx[j]])
        if j == n_win - 1:
            s = _add_slab_bias(s, causal)
        elif j == 0:
            s = _add_slab_bias(s, jnp.where(kpos0 + idx[0] * ck > qpos - WINDOW, 0.0, NEG) + gone[0])
        else:
            s = s + gone[j]
        scores.append(s)
    win = _softmax_values(scores, [(vwb_ref[idx[j]], True) for j in range(n_win)])

    imp_t = imp_rows_t[:, :slab]
    for r in range(1, n_rep):
        imp_t = imp_t + imp_rows_t[:, r * slab:(r + 1) * slab]
    sel_neg = _select_blocks_t(imp_t, qpos_row, n_sel_blocks)
    qx = jnp.concatenate([q_all, jnp.concatenate([sel_neg] * n_rep, axis=0)], axis=1)

    def sel_body(i, carry):
        return _attend(carry, qx, [(kx_ref[2 * i], vsb_ref[2 * i], None), (kx_ref[2 * i + 1], vsb_ref[2 * i + 1], None)])

    sel = lax.fori_loop(0, c_diag // 2, sel_body, _attend_init(n_rep * slab))
    odd = jnp.maximum(c_diag - 1, 0)
    no_odd = jnp.where(c_diag % 2 == 1, 0.0, NEG)
    sel = _attend(sel, qx, [(kx_ref[odd], vsb_ref[odd], no_odd), (kx_ref[c_diag], vsb_ref[c_diag], causal)])

    for r in reps:
        rs = slice(r * slab, (r + 1) * slab)
        piece = _combine_slab(o_cmp[rs], [a[rs] for a in sel], [a[rs] for a in win],
                              _gate_columns(sg, r, n_rep, tq), own, tq)
        out_ref[:, r * GROUP_LANES:(r + 1) * GROUP_LANES] = piece.astype(out_ref.dtype)


def _cmp_to_sel(n_cmp, n_sel_blocks):
    r = SEL_BLOCK // CMP_STRIDE
    c = CMP_BLOCK // CMP_STRIDE
    m = np.zeros((LANES, LANES), np.float32)
    for j in range(n_sel_blocks):
        for a in range(r):
            for b in range(c):
                i = r * j + a - b
                if 0 <= i < n_cmp:
                    m[i, j] += 1.0
    return jnp.asarray(m, BF16)


def _block_expand(n_chunks, ck, n_sel_blocks):
    e = np.zeros((n_chunks, LANES, ck), np.float32)
    key = np.arange(n_chunks * ck).reshape(n_chunks, ck)
    for c in range(n_chunks):
        e[c, key[c] // SEL_BLOCK, np.arange(ck)] = 1.0
    e[:, n_sel_blocks:, :] = 0.0
    return jnp.asarray(e, BF16)


def _nsa_prompt(q, small, kcmp, vcmp, kv_t, win_t, *, B, T, tq, ck):
    M, QW = q.shape
    n_rep = QW // GROUP_LANES
    nqt = T // tq
    n_cmp = (T - CMP_BLOCK) // CMP_STRIDE + 1
    n_sel_blocks = -(-T // SEL_BLOCK)
    assert T % ck == 0 and ck % tq == 0 and n_sel_blocks <= LANES and kcmp.shape[1] <= LANES
    assert WINDOW % ck == 0
    mmap_t = _cmp_to_sel(n_cmp, n_sel_blocks).T[:, :kcmp.shape[1]]
    expand = _block_expand(T // ck, ck, n_sel_blocks)
    const = lambda a: pl.BlockSpec(a.shape, lambda b, t: (0,) * a.ndim)
    kv_spec = lambda slot: pl.BlockSpec((1, GROUP_LANES, T), lambda b, t: (b, slot, 0))
    chunks = lambda kdim: pltpu.VMEM((T // ck, kdim, ck), BF16)
    return pl.pallas_call(
        functools.partial(_nsa_prompt_kernel, tq=tq, ck=ck, n_cmp=n_cmp, n_sel_blocks=n_sel_blocks),
        grid=(B, nqt),
        in_specs=[pl.BlockSpec((tq, QW), lambda b, t: (b * nqt + t, 0)),
                  pl.BlockSpec((tq, LANES), lambda b, t: (b * nqt + t, 0)),
                  pl.BlockSpec((1,) + kcmp.shape[1:], lambda b, t: (b, 0, 0)),
                  pl.BlockSpec((1,) + vcmp.shape[1:], lambda b, t: (b, 0, 0)),
                  kv_spec(2), kv_spec(3), kv_spec(0), kv_spec(1),
                  const(mmap_t), const(expand)],
        out_specs=pl.BlockSpec((tq, QW), lambda b, t: (b * nqt + t, 0)),
        out_shape=jax.ShapeDtypeStruct((M, QW), BF16),
        scratch_shapes=[chunks(GROUP_LANES + LANES), chunks(GROUP_LANES), chunks(GROUP_LANES), chunks(GROUP_LANES)],
        compiler_params=_params(("arbitrary", "arbitrary")),
        name="nsa_prompt",
    )(q, small, kcmp, vcmp, kv_t, kv_t, win_t, win_t, mmap_t, expand)


def _nsa_decode_kernel(pt_ref, *refs, n_pages, page, tq, t_new, past_len, n_cmp, n_sel_blocks, wb):
    del pt_ref
    pages = refs[:n_pages]
    (q_ref, sm_ref, kvn_ref, wn_ref, cw_ref, pe_ref, w1_ref, b1_ref, w2_ref, b2_ref, perm_ref,
     mmap_ref, exp_ref, *outs) = refs[n_pages:]
    out_ref, nw_ref, newk_ref, neww_ref, bias_ref = outs[-5:]
    b = pl.program_id(0)
    n_rep = q_ref.shape[2] // GROUP_LANES
    slab = NSA_KV_HEADS * tq
    reps = range(n_rep)

    @pl.when(b == 0)
    def _():
        newk_ref[...] = jnp.zeros_like(newk_ref)
        neww_ref[...] = jnp.zeros_like(neww_ref)
        for slot in range(2):
            bias_ref[slot] = _pe_bias(slot, pe_ref, w1_ref, b1_ref)

    newk_ref[:tq, :] = kvn_ref[0]
    neww_ref[:tq, :] = wn_ref[0]

    qpos = past_len + lax.broadcasted_iota(jnp.int32, (slab, 1), 0) % tq
    qs, own = _expand_q(q_ref[0], tq)
    sg = jax.nn.sigmoid(sm_ref[0])

    cmp_kv = []
    for slot in range(2):
        xs = _deinterleave_t(perm_ref[...], [pg[0, slot] for pg in pages])
        cmp_kv.append(_compress_slot(xs, slot, bias_ref[slot, 0:1, :], w1_ref, w2_ref, b2_ref).astype(BF16))

    q_all = jnp.concatenate(qs, axis=0)
    o_cmp, p_cmp = _cmp_rows(q_all, _cmp_mask(slab, cmp_kv[0].shape[0], qpos, n_cmp), cmp_kv[0], cmp_kv[1])
    imp_rows = _dot(p_cmp, mmap_ref[...])
    imp = imp_rows[:slab]
    for r in range(1, n_rep):
        imp = imp + imp_rows[r * slab:(r + 1) * slab]
    neg_all = jnp.concatenate([_select_blocks(imp, qpos, n_sel_blocks)] * n_rep, axis=0)

    kpos0 = lax.broadcasted_iota(jnp.int32, (1, page), 1)
    own_rows = jnp.where(kpos0 + past_len <= jnp.concatenate([qpos] * n_rep, axis=0), 0.0, NEG)

    scores = [_dot(q_all, pages[p][0, 2].astype(BF16)) + _dot(neg_all, exp_ref[p]) for p in range(n_pages)]
    values = [(pages[p][0, 3].astype(BF16), True) for p in range(n_pages)]
    k_new = newk_ref[:, 2 * GROUP_LANES:3 * GROUP_LANES].astype(BF16)
    scores.append(_dot_nt(q_all, k_new) + _dot(neg_all, exp_ref[n_pages]) + own_rows)
    values.append((newk_ref[:, 3 * GROUP_LANES:].astype(BF16), False))
    sel = _softmax_values(scores, values)

    kpos = past_len - wb + lax.broadcasted_iota(jnp.int32, (1, wb), 1)
    in_band = jnp.where(kpos > jnp.concatenate([qpos] * n_rep, axis=0) - WINDOW, 0.0, NEG)
    scores = [_dot(q_all, cw_ref[0, 0].astype(BF16)) + in_band,
              _dot_nt(q_all, neww_ref[:, :GROUP_LANES].astype(BF16)) + own_rows]
    values = [(cw_ref[0, 1].astype(BF16), True), (neww_ref[:, GROUP_LANES:].astype(BF16), False)]
    win = _softmax_values(scores, values)

    for r in reps:
        rs = slice(r * slab, (r + 1) * slab)
        piece = _combine_slab(o_cmp[rs], [a[rs] for a in sel], [a[rs] for a in win],
                              _gate_columns(sg, r, n_rep, tq), own, tq)
        out_ref[0, :, r * GROUP_LANES:(r + 1) * GROUP_LANES] = piece

    new_t = neww_ref[...].T
    lane = lax.broadcasted_iota(jnp.int32, (GROUP_LANES, LANES), 1)
    for slot in range(2):
        shifted = pltpu.roll(cw_ref[0, slot], wb - t_new, axis=1)
        fresh = pltpu.roll(new_t[slot * GROUP_LANES:(slot + 1) * GROUP_LANES], LANES - t_new, axis=1)
        nw_ref[0, slot, :, :wb - LANES] = shifted[:, :wb - LANES]
        nw_ref[0, slot, :, wb - LANES:] = jnp.where(lane >= LANES - t_new, fresh, shifted[:, wb - LANES:])


def _nsa_decode(q8, small8, kvnew8, winnew8, cache, cache_win, page_table, cw, *, past_len, t_new, win_off,
                win_stack=None):
    B, tq, QW = q8.shape
    n_pages = page_table.shape[1]
    page = cache.shape[3]
    wb = cache_win.shape[3]
    tk = past_len + t_new
    n_cmp = (tk - CMP_BLOCK) // CMP_STRIDE + 1
    n_sel_blocks = -(-tk // SEL_BLOCK)
    nrow = past_len // CMP_STRIDE
    assert tq == SUBLANES and past_len == n_pages * page and wb % page == 0 and page == LANES
    assert n_cmp <= nrow <= LANES and (n_cmp - 1) * CMP_STRIDE + CMP_BLOCK <= past_len
    assert n_sel_blocks <= LANES and 0 < t_new < tq and wb == WINDOW
    mmap = _cmp_to_sel(n_cmp, n_sel_blocks)[:nrow]
    expand = _block_expand(n_pages + 1, page, n_sel_blocks)
    const = lambda a: pl.BlockSpec(a.shape, lambda b, pt: (0,) * a.ndim)
    page_spec = lambda p: pl.BlockSpec((1,) + cache.shape[1:], lambda b, pt: (pt[b, p], 0, 0, 0))
    per_b = lambda a: pl.BlockSpec((1,) + a.shape[1:], lambda b, pt: (b, 0, 0))
    win_block = (1,) + cache_win.shape[1:]
    consts = list(cw) + [mmap, expand]
    operands = [page_table] + [cache] * n_pages + [q8, small8, kvnew8, winnew8, cache_win] + consts
    extra_specs, aliases = [], {}
    if win_stack is not None:
        extra_specs = [pl.BlockSpec(memory_space=pl.ANY)]
        aliases = {len(operands): 1}
        operands.append(win_stack)
    grid_spec = pltpu.PrefetchScalarGridSpec(
        num_scalar_prefetch=1,
        grid=(B,),
        in_specs=[page_spec(p) for p in range(n_pages)]
        + [per_b(q8), per_b(small8), per_b(kvnew8), per_b(winnew8),
           pl.BlockSpec(win_block, lambda b, pt: (win_off + b, 0, 0, 0))] + [const(a) for a in consts] + extra_specs,
        out_specs=[per_b(q8), pl.BlockSpec(win_block, lambda b, pt: (win_off + b, 0, 0, 0))],
        scratch_shapes=[pltpu.VMEM((page, kvnew8.shape[2]), F32), pltpu.VMEM((page, winnew8.shape[2]), F32),
                        pltpu.VMEM((2, SUBLANES, cw[2].shape[1]), F32)],
    )
    return pl.pallas_call(
        functools.partial(_nsa_decode_kernel, n_pages=n_pages, page=page, tq=tq, t_new=t_new, past_len=past_len,
                          n_cmp=n_cmp, n_sel_blocks=n_sel_blocks, wb=wb),
        grid_spec=grid_spec,
        input_output_aliases=aliases,
        out_shape=[jax.ShapeDtypeStruct(q8.shape, F32), jax.ShapeDtypeStruct(cache_win.shape, F32)],
        compiler_params=_params(("arbitrary",)),
        name="nsa_decode",
    )(*operands)


def _layer_weights(i, prm, d_ml, d_nsa):
    H, G, dk = ML_HEADS, NSA_KV_HEADS, NSA_HEAD_DIM
    R = d_nsa // (G * dk)
    kvw = G * dk
    w_in = prm["w_in"][i]
    o = 0
    c_in, o = w_in[:, o:o + d_ml], o + d_ml
    v_ml, o = w_in[:, o:o + d_ml], o + d_ml
    o_ml, o = w_in[:, o:o + d_ml], o + d_ml
    i_ml, o = w_in[:, o:o + H], o + H
    f_ml, o = w_in[:, o:o + H], o + H
    q_n, o = w_in[:, o:o + d_nsa], o + d_nsa
    kv, o = w_in[:, o:o + 6 * kvw], o + 6 * kvw
    g_n = w_in[:, o:]
    K = w_in.shape[0]
    q_perm = q_n.reshape(K, G, R, dk).transpose(0, 2, 1, 3).reshape(K, d_nsa)
    w_main = jnp.concatenate([c_in, v_ml, o_ml, q_perm, kv], axis=1).astype(BF16)
    w_kvw_t = kv.T.astype(BF16)
    small = jnp.concatenate([i_ml, f_ml, g_n], axis=1)
    w_small = jnp.pad(small, ((0, 0), (0, LANES - small.shape[1]))).astype(BF16)
    w_out = prm["w_out"][i]
    w_out_nsa = w_out[d_ml:].reshape(G, R, dk, -1).transpose(1, 0, 2, 3).reshape(d_nsa, -1)
    w_out_p = jnp.concatenate([w_out[:d_ml], w_out_nsa], axis=0).astype(BF16)
    Dh = d_ml // H
    gate_bias = jnp.pad(jnp.concatenate([prm["b_i"][i], prm["b_f"][i]]), (0, LANES - 2 * H)).reshape(1, LANES)
    w2 = prm["cmp_w2"][i]
    w2_placed = jnp.stack([jnp.stack([jnp.pad(w2[s], ((0, 0), (g * dk, (G - 1 - g) * dk))) for g in range(G)])
                           for s in range(2)]).astype(BF16)
    perm = np.zeros((LANES, LANES), np.float32)
    tok = np.arange(LANES)
    perm[(tok % CMP_STRIDE) * (LANES // CMP_STRIDE) + tok // CMP_STRIDE, tok] = 1.0
    cmp_w = (prm["cmp_pe"][i].reshape(2, -1), prm["cmp_w1"][i].astype(BF16), prm["cmp_b1"][i], w2_placed,
             jnp.tile(prm["cmp_b2"][i], (1, G)), jnp.asarray(perm, BF16))
    return dict(
        w_main=w_main, w_kvw_t=w_kvw_t, w_small=w_small, w_out=w_out_p,
        w_up=prm["w_up_all"], w_down=prm["w_down_all"],
        w_pl=prm["w_pl"][i].astype(BF16), w_pl_gate=prm["w_pl_gate"][i].astype(BF16),
        conv_w=prm["conv_w"][i], conv_b=prm["conv_b"][i].reshape(1, -1),
        wq=prm["w_q_ml"][i].astype(BF16),
        wkt=(jnp.swapaxes(prm["w_k_ml"][i], 1, 2) * (Dh ** -0.5)).astype(BF16),
        gate_bias=gate_bias, g_ml=prm["g_ml"][i].reshape(1, -1), cmp=cmp_w)


def _pad_time(a, B, t, tp):
    return jnp.pad(a.reshape(B, t, -1), ((0, 0), (0, tp - t), (0, 0)))


def _layer(h, pl_e, i, prm, lw, mem, *, B, T):
    M, D = h.shape
    G, dk, H = NSA_KV_HEADS, NSA_HEAD_DIM, ML_HEADS
    d_ml = lw["conv_w"].shape[1]
    d_nsa = lw["w_out"].shape[0] - d_ml
    R = d_nsa // (G * dk)
    Dh = d_ml // H
    kvw = G * dk
    tm = min(M, 512)
    g_pre = prm["g_pre_mix"][i]
    lanes_last = lambda a, lead: a.reshape(lead + (G, dk, a.shape[-1]))

    ml_w = (lw["conv_w"], lw["conv_b"], lw["wq"], lw["wkt"], lw["gate_bias"], lw["g_ml"])
    if mem is None:
        cin, v, o, q, small = _norm_matmul(h, g_pre, lw["w_main"], (d_ml, d_ml, d_ml, d_nsa),
                                           w_small=lw["w_small"], tm=min(M, 1024), tn=512)
        kv_t, win_t = _norm_matmul_t(h, g_pre, lw["w_kvw_t"], (4 * kvw, 2 * kvw), B=B, T=T, tm=min(T, 1024), tn=512)
        zeros = lambda *s: jnp.zeros(s, F32)
        hm, c_new, n_new, m_new = _mlstm(
            cin, v, o, small, zeros(B, SUBLANES, d_ml), zeros(B, H, Dh, Dh), zeros(B, 1, d_ml), zeros(B, 1, LANES),
            *ml_w, B=B, rows=ML_CHUNK, valid=ML_CHUNK)
        kcmp, vcmp = _compress_prompt(kv_t, lw["cmp"], B=B, T=T)
        on = _nsa_prompt(q, small, kcmp, vcmp, kv_t, win_t, B=B, T=T, tq=64, ck=256)
        wlen = min(WINDOW, T)
        new_rows = lanes_last(kv_t, (B, 4)).transpose(0, 4, 1, 2, 3)
        new_win = lanes_last(win_t[:, :, T - wlen:], (B, 2)).transpose(0, 4, 1, 2, 3)
    else:
        cin, v, o, q, kv4, win2, small = _norm_matmul(
            h, g_pre, lw["w_main"], (d_ml, d_ml, d_ml, d_nsa, 4 * kvw, 2 * kvw),
            w_small=lw["w_small"], tm=tm, tn=512)
        tp = SUBLANES
        pad = lambda a: _pad_time(a, B, T, tp)
        prev = jnp.pad(mem["conv"], ((0, 0), (SUBLANES - mem["conv"].shape[1], 0), (0, 0)))
        m0 = jnp.pad(mem["m"], ((0, 0), (0, LANES - H))).reshape(B, 1, LANES)
        hm8, c_new, n_new, m_new = _mlstm(
            pad(cin).reshape(B * tp, d_ml), pad(v).reshape(B * tp, d_ml), pad(o).reshape(B * tp, d_ml),
            pad(small).reshape(B * tp, LANES), prev, mem["C"], mem["n"].reshape(B, 1, d_ml), m0,
            *ml_w, B=B, rows=tp, valid=T, c_off=i * B, c_layers=mem["C"].shape[0] // B, c_stack=mem["c_stack"])
        hm = hm8.reshape(B, tp, d_ml)[:, :T].reshape(M, d_ml)
        page = mem["kv"].shape[3]
        on8, new_win = _nsa_decode(
            pad(q), pad(small), pad(kv4), pad(win2),
            mem["kv"], mem["win"], mem["page_table"], lw["cmp"],
            past_len=mem["page_table"].shape[1] * page, t_new=T, win_off=i * B, win_stack=mem["win_stack"])
        on = on8[:, :T].reshape(M, d_nsa)
        new_rows = kv4.reshape(B, T, 4, G, dk)

    h = _matmul_norm_res([hm, on], lw["w_out"], h, prm["g_post_mix"][i], tm=tm, tk=d_ml + d_nsa)
    u, = _norm_matmul(h, prm["g_pre_mlp"][i], lw["w_up"], (lw["w_up"].shape[1],), tm=min(M, 1024), tn=1024,
                      act="relu2", out_dtype=BF16, layer=i)
    h = _matmul_norm_res([u], lw["w_down"], h, prm["g_post_mlp"][i], tm=tm, tk=2048, layer=i)
    h = _ple(h, pl_e, lw["w_pl_gate"], lw["w_pl"], prm["g_pl"][i], tm=min(M, 256), layer=i)

    state = (new_rows, new_win, c_new, n_new.reshape(B, H, Dh), m_new[:, 0, :H], cin.reshape(B, T, d_ml)[:, T - 3:])
    return h, state


def kernel(x_prompt, x_sample, cache_kv, cache_win, state_C, state_n, state_m, state_conv, page_table,
           p_prompt, p_sample, g_pre_mix, w_in, conv_w, conv_b, w_q_ml, w_k_ml, b_i, b_f, g_ml,
           cmp_pe, cmp_w1, cmp_b1, cmp_w2, cmp_b2, w_out, g_post_mix, g_pre_mlp, w_up, w_down,
           g_post_mlp, w_pl, g_pl, w_pl_gate):
    prm = dict(g_pre_mix=g_pre_mix, w_in=w_in, conv_w=conv_w, conv_b=conv_b, w_q_ml=w_q_ml, w_k_ml=w_k_ml,
               b_i=b_i, b_f=b_f, g_ml=g_ml, cmp_pe=cmp_pe, cmp_w1=cmp_w1, cmp_b1=cmp_b1, cmp_w2=cmp_w2,
               cmp_b2=cmp_b2, w_out=w_out, g_post_mix=g_post_mix, g_pre_mlp=g_pre_mlp, w_up=w_up,
               w_down=w_down, g_post_mlp=g_post_mlp, w_pl=w_pl, g_pl=g_pl, w_pl_gate=w_pl_gate)
    Bp, Tp, D = x_prompt.shape
    Bs, Ts, _ = x_sample.shape
    depth = w_in.shape[0]
    d_ml = conv_w.shape[2]
    d_nsa = w_out.shape[1] - d_ml
    hp = x_prompt.reshape(Bp * Tp, D)
    hs = x_sample.reshape(Bs * Ts, D)
    n_pool, page = cache_kv.shape[1:3]
    kv_t = jnp.transpose(cache_kv, (0, 1, 3, 4, 5, 2)).reshape(depth * n_pool, cache_kv.shape[3], -1, page)
    win_t = jnp.transpose(cache_win, (0, 1, 3, 4, 5, 2)).reshape(depth * Bs, cache_win.shape[3], -1, cache_win.shape[2])
    c_all = state_C.reshape((depth * Bs,) + state_C.shape[2:])
    prm["w_up_all"] = w_up.astype(BF16).reshape(-1, w_up.shape[2])
    prm["w_down_all"] = w_down.astype(BF16).reshape(-1, w_down.shape[2])
    sp, ss = [], []
    c_stack = win_stack = None
    for i in range(depth):
        lw = _layer_weights(i, prm, d_ml, d_nsa)
        mem = dict(kv=kv_t, page_table=page_table + i * n_pool, win=win_t, C=c_all, n=state_n[i],
                   m=state_m[i], conv=state_conv[i], c_stack=c_stack, win_stack=win_stack)
        hp, st_p = _layer(hp, p_prompt.reshape(depth * Bp * Tp, -1), i, prm, lw, None, B=Bp, T=Tp)
        hs, st_s = _layer(hs, p_sample.reshape(depth * Bs * Ts, -1), i, prm, lw, mem, B=Bs, T=Ts)
        win_stack, c_stack = st_s[1], st_s[2]
        sp.append(st_p)
        ss.append(st_s)

    stk = lambda lst, j: jnp.stack([s[j] for s in lst])
    G, dk = NSA_KV_HEADS, NSA_HEAD_DIM
    win_sample = win_stack.reshape(depth, Bs, win_stack.shape[1], G, dk, win_stack.shape[3]).transpose(0, 1, 5, 2, 3, 4)
    return (hp.reshape(Bp, Tp, D), hs.reshape(Bs, Ts, D), stk(sp, 0), stk(ss, 0), stk(sp, 1), win_sample,
            stk(sp, 2), c_stack.reshape(state_C.shape), stk(sp, 3), stk(ss, 3), stk(sp, 4), stk(ss, 4),
            stk(sp, 5), stk(ss, 5))
```

```python
import functools

import numpy as np
import jax
import jax.numpy as jnp
from jax import lax
from jax.experimental import pallas as pl
from jax.experimental.pallas import tpu as pltpu

ML_HEADS = 4
NSA_HEAD_DIM = 64
NSA_KV_HEADS = 4
CMP_BLOCK = 32
CMP_STRIDE = 16
SEL_BLOCK = 64
N_SEL = 16
WINDOW = 512
EPS = 1e-6
NEG = -1e30
FORCE = 1e6

LANES = 128
SUBLANES = 8
VMEM_LIMIT_BYTES = 56 * 1024 * 1024

ML_CHUNK = 128
NSA_DECODE_SEQS_PER_STEP = 2
MLSTM_SEQS_PER_STEP_PROMPT = 2
MLSTM_SEQS_PER_STEP_DECODE = 4
GROUP_LANES = NSA_KV_HEADS * NSA_HEAD_DIM
QK_SCALE = NSA_HEAD_DIM ** -0.5 * 1.4426950408889634
MASK_BIG = 1e30

F32 = jnp.float32
BF16 = jnp.bfloat16


def _dot(a, b):
    return jnp.dot(a, b, preferred_element_type=F32)


def _dot_nt(a, b):
    return lax.dot_general(a, b, (((1,), (1,)), ((), ())), preferred_element_type=F32)


def _params(semantics):
    return pltpu.CompilerParams(dimension_semantics=semantics, vmem_limit_bytes=VMEM_LIMIT_BYTES)


def _rms_rows(x):
    return x * lax.rsqrt(jnp.mean(x * x, axis=-1, keepdims=True) + EPS)


def _norm_matmul_kernel(x_ref, g_ref, w_ref, *rest, seg_tiles, has_small, act):
    if has_small:
        ws_ref, rest = rest[0], rest[1:]
    n_out = len(seg_tiles) + (1 if has_small else 0)
    out_refs, xn_ref = rest[:n_out], rest[n_out]
    j = pl.program_id(1)

    @pl.when(j == 0)
    def _():
        xn = (_rms_rows(x_ref[...]) * g_ref[...]).astype(BF16)
        xn_ref[...] = xn
        if has_small:
            out_refs[-1][...] = _dot(xn, ws_ref[...])

    y = _dot(xn_ref[...], w_ref[...])
    if act == "relu2":
        y = jnp.square(jnp.maximum(y, 0.0))
    start = 0
    for k, n in enumerate(seg_tiles):
        @pl.when((j >= start) & (j < start + n))
        def _(k=k):
            out_refs[k][...] = y.astype(out_refs[k].dtype)
        start += n


def _norm_matmul(x, g, w, seg_cols, *, w_small=None, tm, tn, act=None, out_dtype=F32, layer=0):
    M, K = x.shape
    seg_tiles = tuple(c // tn for c in seg_cols)
    assert all(c % tn == 0 for c in seg_cols) and M % tm == 0
    has_small = w_small is not None
    starts = np.concatenate([[0], np.cumsum(seg_tiles)[:-1]]).tolist()

    def out_map(start, n):
        return lambda i, j: (i, jnp.clip(j - start, 0, n - 1))

    in_specs = [pl.BlockSpec((tm, K), lambda i, j: (i, 0)),
                pl.BlockSpec((1, K), lambda i, j: (0, 0)),
                pl.BlockSpec((K, tn), lambda i, j: (layer, j))]
    args = [x, g.reshape(1, K), w]
    out_specs = [pl.BlockSpec((tm, tn), out_map(s, n)) for s, n in zip(starts, seg_tiles)]
    out_shape = [jax.ShapeDtypeStruct((M, c), out_dtype) for c in seg_cols]
    if has_small:
        in_specs.append(pl.BlockSpec((K, LANES), lambda i, j: (0, 0)))
        args.append(w_small)
        out_specs.append(pl.BlockSpec((tm, LANES), lambda i, j: (i, 0)))
        out_shape.append(jax.ShapeDtypeStruct((M, LANES), F32))
    return pl.pallas_call(
        functools.partial(_norm_matmul_kernel, seg_tiles=seg_tiles, has_small=has_small, act=act),
        grid=(M // tm, sum(seg_tiles)),
        in_specs=in_specs, out_specs=out_specs, out_shape=out_shape,
        scratch_shapes=[pltpu.VMEM((tm, K), BF16)],
        compiler_params=_params(("parallel", "arbitrary")),
        name="norm_matmul",
    )(*args)


def _norm_matmul_t_kernel(x_ref, g_ref, wt_ref, *rest, seg_tiles):
    out_refs, xn_ref = rest[:-1], rest[-1]
    j = pl.program_id(1)

    @pl.when(j == 0)
    def _():
        xn_ref[...] = (_rms_rows(x_ref[...]) * g_ref[...]).astype(BF16)

    y = _dot_nt(wt_ref[...], xn_ref[...])
    start = 0
    for k, n in enumerate(seg_tiles):
        @pl.when((j >= start) & (j < start + n))
        def _(k=k):
            out_refs[k][0] = y
        start += n


def _norm_matmul_t(x, g, wt, seg_rows, *, B, T, tm, tn):
    M, K = x.shape
    seg_tiles = tuple(r // tn for r in seg_rows)
    assert T % tm == 0 and all(r % tn == 0 for r in seg_rows) and M == B * T
    per = T // tm
    starts = np.concatenate([[0], np.cumsum(seg_tiles)[:-1]]).tolist()

    def out_map(start, n):
        return lambda i, j: (i // per, jnp.clip(j - start, 0, n - 1), i % per)

    return pl.pallas_call(
        functools.partial(_norm_matmul_t_kernel, seg_tiles=seg_tiles),
        grid=(M // tm, sum(seg_tiles)),
        in_specs=[pl.BlockSpec((tm, K), lambda i, j: (i, 0)),
                  pl.BlockSpec((1, K), lambda i, j: (0, 0)),
                  pl.BlockSpec((tn, K), lambda i, j: (j, 0))],
        out_specs=[pl.BlockSpec((1, tn, tm), out_map(s, n)) for s, n in zip(starts, seg_tiles)],
        out_shape=[jax.ShapeDtypeStruct((B, r, T), F32) for r in seg_rows],
        scratch_shapes=[pltpu.VMEM((tm, K), BF16)],
        compiler_params=_params(("parallel", "arbitrary")),
        name="norm_matmul_t",
    )(x, g.reshape(1, K), wt)


def _matmul_norm_res_kernel(*refs, n_a):
    a_refs = refs[:n_a]
    w_ref, h_ref, g_ref, o_ref, acc_ref = refs[n_a:]
    k = pl.program_id(1)

    @pl.when(k == 0)
    def _():
        acc_ref[...] = jnp.zeros_like(acc_ref)

    off = 0
    for a_ref in a_refs:
        width = a_ref.shape[1]
        acc_ref[...] += _dot(a_ref[...].astype(BF16), w_ref[off:off + width, :])
        off += width

    @pl.when(k == pl.num_programs(1) - 1)
    def _():
        o_ref[...] = h_ref[...] + _rms_rows(acc_ref[...]) * g_ref[...]


def _matmul_norm_res(a_parts, w, h, g, *, tm, tk, layer=0):
    M = a_parts[0].shape[0]
    K = sum(a.shape[1] for a in a_parts)
    N = w.shape[1]
    assert M % tm == 0 and K % tk == 0 and (len(a_parts) == 1 or tk == K)
    k_steps = K // tk
    a_specs = [pl.BlockSpec((tm, tk if len(a_parts) == 1 else a.shape[1]), lambda i, k: (i, k)) for a in a_parts]
    return pl.pallas_call(
        functools.partial(_matmul_norm_res_kernel, n_a=len(a_parts)),
        grid=(M // tm, k_steps),
        in_specs=a_specs + [pl.BlockSpec((tk, N), lambda i, k: (layer * k_steps + k, 0)),
                            pl.BlockSpec((tm, N), lambda i, k: (i, 0)),
                            pl.BlockSpec((1, N), lambda i, k: (0, 0))],
        out_specs=pl.BlockSpec((tm, N), lambda i, k: (i, 0)),
        out_shape=jax.ShapeDtypeStruct((M, N), F32),
        scratch_shapes=[pltpu.VMEM((tm, N), F32)],
        compiler_params=_params(("parallel", "arbitrary")),
        name="matmul_norm_res",
    )(*a_parts, w, h, g.reshape(1, N))


def _ple_kernel(h_ref, p_ref, wg_ref, wp_ref, g_ref, o_ref):
    h = h_ref[...]
    gate = jax.nn.sigmoid(_dot(_rms_rows(h).astype(BF16), wg_ref[...]))
    e = _dot(p_ref[...].astype(BF16), wp_ref[...])
    o_ref[...] = h + gate * (_rms_rows(e) * g_ref[...])


def _ple(h, p, wg, wp, g, *, tm, layer=0):
    M, N = h.shape
    P = p.shape[1]
    assert M % tm == 0
    return pl.pallas_call(
        _ple_kernel,
        grid=(M // tm,),
        in_specs=[pl.BlockSpec((tm, N), lambda i: (i, 0)),
                  pl.BlockSpec((tm, P), lambda i: (layer * (M // tm) + i, 0)),
                  pl.BlockSpec((N, N), lambda i: (0, 0)),
                  pl.BlockSpec((P, N), lambda i: (0, 0)),
                  pl.BlockSpec((1, N), lambda i: (0, 0))],
        out_specs=pl.BlockSpec((tm, N), lambda i: (i, 0)),
        out_shape=jax.ShapeDtypeStruct((M, N), F32),
        compiler_params=_params(("parallel",)),
        name="ple",
    )(h, p, wg, wp, g.reshape(1, N))


def _shift_rows(x, prev8, s):
    rows = x.shape[0]
    xs = pltpu.roll(x, s, axis=0)
    rid = lax.broadcasted_iota(jnp.int32, (SUBLANES, x.shape[1]), 0)
    head = jnp.where(rid < s, pltpu.roll(prev8, s, axis=0), xs[:SUBLANES])
    if rows == SUBLANES:
        return head
    return jnp.concatenate([head, xs[SUBLANES:]], axis=0)


def _mlstm_kernel(*refs, nb, aliased, **static):
    n_seq, n_shared = 8, 7
    seq_in, shared = refs[:n_seq], refs[n_seq:n_seq + n_shared]
    rest = refs[n_seq + n_shared + (1 if aliased else 0):]
    per_seq = [[r.at[j] for r in seq_in] + list(shared) + [r.at[j] for r in rest] for j in range(nb)]
    for args in per_seq:
        _mlstm_load_state(*args, **static)
    m_rows = [None] * nb
    running = {j: _mlstm_seq(*args, **static) for j, args in enumerate(per_seq)}
    while running:
        for j in list(running):
            try:
                next(running[j])
            except StopIteration as done:
                m_rows[j] = done.value
                del running[j]
    for args, m_row in zip(per_seq, m_rows):
        _mlstm_store_state(*args, m_row, **static)


def _mlstm_load_state(cin_ref, v_ref, o_ref, sm_ref, prev_ref, c0_ref, n0_ref, m0_ref,
                      cw_ref, cb_ref, wq_ref, wkt_ref, gb_ref, gml_ref, ltri_ref,
                      hm_ref, cout_ref, nout_ref, mout_ref,
                      carry_ref, cext_ref, m_ref, chpad_ref, vpad_ref, gpad_ref,
                      *, L, rows, valid, H, Dh):
    @pl.when(pl.program_id(1) == 0)
    def _():
        carry_ref[...] = prev_ref[...]
        m_ref[...] = m0_ref[...]
        for h in range(H):
            cext_ref[h, :, :Dh] = c0_ref[h]
            n_row = n0_ref[:, h * Dh:(h + 1) * Dh]
            cext_ref[h, :, Dh:] = jnp.broadcast_to(n_row, (LANES, Dh)).T

    if rows < L:
        @pl.when((pl.program_id(0) == 0) & (pl.program_id(1) == 0))
        def _():
            chpad_ref[...] = jnp.zeros_like(chpad_ref)
            vpad_ref[...] = jnp.zeros_like(vpad_ref)
            gpad_ref[...] = jnp.zeros_like(gpad_ref)


def _mlstm_store_state(cin_ref, v_ref, o_ref, sm_ref, prev_ref, c0_ref, n0_ref, m0_ref,
                       cw_ref, cb_ref, wq_ref, wkt_ref, gb_ref, gml_ref, ltri_ref,
                       hm_ref, cout_ref, nout_ref, mout_ref,
                       carry_ref, cext_ref, m_ref, chpad_ref, vpad_ref, gpad_ref, m_new_row,
                       *, L, rows, valid, H, Dh):
    @pl.when(pl.program_id(1) == pl.num_programs(1) - 1)
    def _():
        mout_ref[...] = m_new_row
        for h in range(H):
            cout_ref[h] = cext_ref[h, :, :Dh]
            nout_ref[:, h * Dh:(h + 1) * Dh] = cext_ref[h, :, Dh:].T[0:1, :]


def _mlstm_seq(cin_ref, v_ref, o_ref, sm_ref, prev_ref, c0_ref, n0_ref, m0_ref,
               cw_ref, cb_ref, wq_ref, wkt_ref, gb_ref, gml_ref, ltri_ref,
               hm_ref, cout_ref, nout_ref, mout_ref,
               carry_ref, cext_ref, m_ref, chpad_ref, vpad_ref, gpad_ref,
               *, L, rows, valid, H, Dh):
    x = cin_ref[...]
    prev8 = carry_ref[...]
    conv = cb_ref[...] + cw_ref[3:4, :] * x
    for s in (1, 2, 3):
        conv = conv + cw_ref[3 - s:4 - s, :] * _shift_rows(x, prev8, s)
    carry_ref[...] = x[rows - SUBLANES:, :]
    ch = conv * jax.nn.sigmoid(conv)

    if rows < L:
        chpad_ref[:rows, :] = ch
        vpad_ref[:rows, :] = v_ref[...]
        gpad_ref[:rows, :] = sm_ref[...]
        ch, v_all, sm = chpad_ref[...], vpad_ref[...], gpad_ref[...]
    else:
        v_all, sm = v_ref[...], sm_ref[...]
    ch = ch.astype(BF16)

    lane = lax.broadcasted_iota(jnp.int32, (L, LANES), 1)
    rid = lax.broadcasted_iota(jnp.int32, (L, LANES), 0)
    pre = sm + gb_ref[...]
    gates = jnp.where(lane < H, pre, jax.nn.log_sigmoid(pre))
    gates = jnp.where(rid < valid, gates, jnp.where(lane < H, NEG, 0.0))
    hi = gates.astype(BF16)
    r1 = gates - hi.astype(F32)
    mid = r1.astype(BF16)
    lo = (r1 - mid.astype(F32)).astype(BF16)
    ltri = ltri_ref[...]
    bc_all = _dot(ltri, hi) + _dot(ltri, mid) + _dot(ltri, lo)
    colform = jnp.where(lane < H, gates, bc_all)
    rowform = colform.T

    r_i = lax.broadcasted_iota(jnp.int32, (L, L), 0)
    c_i = lax.broadcasted_iota(jnp.int32, (L, L), 1)
    causal = c_i <= r_i
    lane1 = lax.broadcasted_iota(jnp.int32, (1, LANES), 1)
    m_row = m_ref[...]
    m_new_row = m_row
    ones = jnp.ones((L, LANES), F32)

    yield
    heads = range(H)
    hsl = [slice(h * Dh, (h + 1) * Dh) for h in heads]
    q = [_dot(ch[:, hsl[h]], wq_ref[h]).astype(BF16) for h in heads]
    kt = [_dot_nt(wkt_ref[h], ch[:, hsl[h]]).astype(BF16) for h in heads]
    cext = [cext_ref[h] for h in heads]
    qc = [_dot(q[h], cext[h].astype(BF16)) for h in heads]
    qk = [_dot(q[h], kt[h]) for h in heads]
    yield

    bc_col = [colform[:, H + h:H + h + 1] for h in heads]
    m_h = [m_row[:, h:h + 1] for h in heads]
    mt, s_w, a_int = [], [], []
    for h in heads:
        logd = jnp.where(causal, bc_col[h] - rowform[H + h:H + h + 1, :] + rowform[h:h + 1, :], NEG)
        inter = m_h[h] + bc_col[h]
        mt.append(jnp.maximum(jnp.max(logd, axis=-1, keepdims=True), inter))
        s_w.append((qk[h] * jnp.exp(logd - mt[h])).astype(BF16))
        a_int.append(jnp.exp(inter - mt[h]))
    v_ext = [jnp.concatenate([v_all[:, hsl[h]], ones], axis=1) for h in heads]
    sv = [_dot(s_w[h], v_ext[h].astype(BF16)) for h in heads]
    yield

    for h in heads:
        m_last = mt[h][L - 1:L, :]
        bc_last = bc_col[h][L - 1:L, :]
        w_col = jnp.exp(bc_last - bc_col[h] + colform[:, h:h + 1] - m_last)
        dec = jnp.exp(m_h[h] + bc_last - m_last)
        cext_ref[h] = dec * cext[h] + _dot(kt[h], (w_col * v_ext[h]).astype(BF16))
        m_new_row = jnp.where(lane1 == h, m_last, m_new_row)
    yield

    for h in heads:
        tot = sv[h] + a_int[h] * qc[h]
        hh = tot[:, :Dh] / jnp.maximum(jnp.abs(tot[:, Dh:Dh + 1]), jnp.exp(-mt[h]))
        y = _rms_rows(hh) * gml_ref[:, hsl[h]]
        hm_ref[:, hsl[h]] = y[:rows] * jax.nn.sigmoid(o_ref[:, hsl[h]])

    m_ref[...] = m_new_row
    return m_new_row


def _mlstm(cin, v, o, small, prev, c0, n0, m0, cw, cb, wq, wkt, gb, gml, *, B, rows, valid, c_off=0,
           c_layers=1, c_stack=None):
    L = ML_CHUNK
    H = ML_HEADS
    D = cin.shape[1]
    Dh = D // H
    nc = cin.shape[0] // (B * rows)
    nb = min(B, MLSTM_SEQS_PER_STEP_PROMPT if rows == L else MLSTM_SEQS_PER_STEP_DECODE)
    assert rows == L or (nc == 1 and rows == SUBLANES)
    assert B % nb == 0 and c_off % nb == 0
    ltri = jnp.asarray(np.tril(np.ones((L, L), np.float32)), BF16)
    seq3 = lambda a: a.reshape(B, nc * rows, a.shape[1])
    tok = lambda g, c: (g, c, 0)
    per_g3 = lambda g, c: (g, 0, 0)
    per_g4 = lambda g, c: (g, 0, 0, 0)
    const2 = lambda g, c: (0, 0)
    const3 = lambda g, c: (0, 0, 0)
    operands = [seq3(cin), seq3(v), seq3(o), seq3(small), prev, c0, n0, m0, cw, cb, wq, wkt, gb, gml, ltri]
    stacked = c_layers > 1
    aliased = stacked and c_stack is not None
    extra_specs, aliases = [], {}
    if aliased:
        extra_specs = [pl.BlockSpec(memory_space=pl.ANY)]
        aliases = {len(operands): 1}
        operands.append(c_stack)
    c_out_map = (lambda g, c: (c_off // nb + g, 0, 0, 0)) if stacked else per_g4
    hm, cout, nout, mout = pl.pallas_call(
        functools.partial(_mlstm_kernel, nb=nb, aliased=aliased, L=L, rows=rows, valid=valid, H=H, Dh=Dh),
        grid=(B // nb, nc),
        input_output_aliases=aliases,
        in_specs=[pl.BlockSpec((nb, rows, D), tok), pl.BlockSpec((nb, rows, D), tok), pl.BlockSpec((nb, rows, D), tok),
                  pl.BlockSpec((nb, rows, LANES), tok),
                  pl.BlockSpec((nb, SUBLANES, D), per_g3),
                  pl.BlockSpec((nb, H, Dh, Dh), lambda g, c: (c_off // nb + g, 0, 0, 0)),
                  pl.BlockSpec((nb, 1, D), per_g3),
                  pl.BlockSpec((nb, 1, LANES), per_g3),
                  pl.BlockSpec((4, D), const2), pl.BlockSpec((1, D), const2),
                  pl.BlockSpec((H, Dh, Dh), const3), pl.BlockSpec((H, Dh, Dh), const3),
                  pl.BlockSpec((1, LANES), const2), pl.BlockSpec((1, D), const2),
                  pl.BlockSpec((L, L), const2)] + extra_specs,
        out_specs=[pl.BlockSpec((nb, rows, D), tok),
                   pl.BlockSpec((nb, H, Dh, Dh), c_out_map),
                   pl.BlockSpec((nb, 1, D), per_g3),
                   pl.BlockSpec((nb, 1, LANES), per_g3)],
        out_shape=[jax.ShapeDtypeStruct((B, nc * rows, D), F32),
                   jax.ShapeDtypeStruct((c_layers * B, H, Dh, Dh), F32),
                   jax.ShapeDtypeStruct((B, 1, D), F32),
                   jax.ShapeDtypeStruct((B, 1, LANES), F32)],
        scratch_shapes=[pltpu.VMEM((nb, SUBLANES, D), F32),
                        pltpu.VMEM((nb, H, Dh, Dh + LANES), F32),
                        pltpu.VMEM((nb, 1, LANES), F32),
                        pltpu.VMEM((nb, L, D), F32), pltpu.VMEM((nb, L, D), F32), pltpu.VMEM((nb, L, LANES), F32)],
        compiler_params=_params(("arbitrary", "arbitrary")),
        name="mlstm",
    )(*operands)
    return hm.reshape(cin.shape), cout, nout, mout


def _deinterleave_t(perm, groups):
    per = LANES // CMP_STRIDE
    ys = [_dot_nt(perm, x_t.astype(BF16)) for x_t in groups]
    return [jnp.concatenate([y[l * per:(l + 1) * per] for y in ys], axis=0) for l in range(CMP_STRIDE)]


def _pe_bias(slot, pe_ref, w1_ref, b1_ref):
    pe = jnp.broadcast_to(pe_ref[slot:slot + 1, :], (SUBLANES, pe_ref.shape[1])).astype(BF16)
    return _dot(pe, w1_ref[slot]) + b1_ref[slot:slot + 1, :]


def _compress_slots(perm, groups, biases, w1_ref, w2_ref, b2_ref):
    dk = NSA_HEAD_DIM
    half = CMP_BLOCK // 2
    pack = GROUP_LANES // dk
    G = NSA_KV_HEADS
    slots = range(len(groups))
    xs = [_deinterleave_t(perm, groups[s]) for s in slots]
    nrow = xs[0][0].shape[0]
    first = [None for _ in slots]
    second = [None for _ in slots]
    for l0 in range(0, half, pack):
        for s in slots:
            xg = jnp.concatenate(
                [jnp.concatenate([xs[s][l0 + j][:, g * dk:(g + 1) * dk] for j in range(pack)], axis=1)
                 for g in range(G)], axis=0).astype(BF16)
            a = _dot(xg, w1_ref[s, l0 * dk:(l0 + pack) * dk, :])
            bb = _dot(xg, w1_ref[s, (half + l0) * dk:(half + l0 + pack) * dk, :])
            first[s] = a if first[s] is None else first[s] + a
            second[s] = bb if second[s] is None else second[s] + bb
    hid = [jax.nn.gelu(first[s] + pltpu.roll(second[s], G * nrow - 1, axis=0) + biases[s]).astype(BF16)
           for s in slots]
    outs = []
    for s in slots:
        out = None
        for g in range(G):
            og = _dot(hid[s][g * nrow:(g + 1) * nrow], w2_ref[s, g])
            out = og if out is None else out + og
        outs.append(out + b2_ref[s:s + 1, :])
    return outs


def _expand_q(q, tq):
    slab = NSA_KV_HEADS * tq
    lane_g = lax.broadcasted_iota(jnp.int32, (slab, GROUP_LANES), 1) // NSA_HEAD_DIM
    row_g = lax.broadcasted_iota(jnp.int32, (slab, GROUP_LANES), 0) // tq
    own = lane_g == row_g
    n_rep = q.shape[1] // GROUP_LANES
    slabs = []
    for r in range(n_rep):
        qr = q[:, r * GROUP_LANES:(r + 1) * GROUP_LANES] * QK_SCALE
        slabs.append(jnp.where(own, jnp.concatenate([qr] * NSA_KV_HEADS, axis=0), 0.0).astype(BF16))
    return slabs, own


def _attend(carry, q, chunks):
    m, l, acc = carry
    scores = []
    for k, _, bias in chunks:
        s = _dot(q, k)
        if bias is not None:
            s = _add_slab_bias(s, bias) if jnp.ndim(bias) == 2 else s + bias
        scores.append(s)
    mx = scores[0]
    for s in scores[1:]:
        mx = jnp.maximum(mx, s)
    m_new = jnp.maximum(m, jnp.max(mx, axis=1, keepdims=True))
    alpha = jnp.exp2(m - m_new)
    ps, pv = None, None
    for s, (_, v, _) in zip(scores, chunks):
        p = jnp.exp2(s - m_new)
        ps = p if ps is None else ps + p
        d = _dot_nt(p.astype(BF16), v)
        pv = d if pv is None else pv + d
    return m_new, alpha * l + jnp.sum(ps, axis=1, keepdims=True), alpha * acc + pv


def _add_slab_bias(s, bias):
    rows, ck = s.shape
    return (s.reshape(rows // bias.shape[0], bias.shape[0], ck) + bias[None]).reshape(rows, ck)


def _softmax_values(scores, values):
    mx = scores[0]
    for s in scores[1:]:
        mx = jnp.maximum(mx, s) if s.shape == mx.shape else mx
    m = mx.max(axis=1, keepdims=True)
    for s in scores[1:]:
        if s.shape != mx.shape:
            m = jnp.maximum(m, s.max(axis=1, keepdims=True))
    l, acc, ps = None, None, None
    for s, (v, v_t) in zip(scores, values):
        p = jnp.exp2(s - m)
        if p.shape == mx.shape:
            ps = p if ps is None else ps + p
        else:
            ls = p.sum(axis=1, keepdims=True)
            l = ls if l is None else l + ls
        pv = _dot_nt(p.astype(BF16), v) if v_t else _dot(p.astype(BF16), v)
        acc = pv if acc is None else acc + pv
    ls = ps.sum(axis=1, keepdims=True)
    return m, (ls if l is None else l + ls), acc


def _attend_init(rows):
    return (jnp.full((rows, 1), NEG, F32), jnp.zeros((rows, 1), F32), jnp.zeros((rows, GROUP_LANES), F32))


def _cmp_rows(q, mask, kcmp, vcmp):
    rows, slab = q.shape[0], mask.shape[0]
    s = _dot_nt(q, kcmp).reshape(rows // slab, slab, mask.shape[1])
    s = jnp.where(mask[None], s, NEG)
    e = jnp.where(mask[None], jnp.exp2(s - jnp.max(s, axis=2, keepdims=True)), 0.0)
    p = e * (1.0 / jnp.maximum(jnp.sum(e, axis=2, keepdims=True), 1e-30))
    p = p.reshape(rows, mask.shape[1]).astype(BF16)
    return _dot(p, vcmp), p


def _cmp_mask(rows, n_rows_cmp, qpos_col, n_cmp):
    lane = lax.broadcasted_iota(jnp.int32, (rows, n_rows_cmp), 1)
    return (lane * CMP_STRIDE + (CMP_BLOCK - 1) <= qpos_col) & (lane < n_cmp)


def _block_scores(imp, blk, cur, n_sel_blocks):
    forced = (blk == 0) | (blk == cur) | (blk == cur - 1)
    score = jnp.where(blk <= cur, jnp.where(forced, FORCE, imp), -1.0)
    return jnp.where(blk < n_sel_blocks, score, -2.0)


def _select_blocks(imp, qpos_col, n_sel_blocks):
    rows = imp.shape[0]
    blk = lax.broadcasted_iota(jnp.int32, (rows, LANES), 1)
    score = _block_scores(imp, blk, qpos_col // SEL_BLOCK, n_sel_blocks)
    rank = jnp.zeros((rows, LANES), F32)
    for j in range(n_sel_blocks):
        col = score[:, j:j + 1]
        ge = jnp.where(col >= score, 1.0, 0.0)
        gt = jnp.where(col > score, 1.0, 0.0)
        rank = rank + jnp.where(blk > j, ge, gt)
    chosen = jnp.where(rank < float(min(N_SEL, n_sel_blocks)), score, -1.0) >= 0.0
    return jnp.where(chosen, 0.0, -MASK_BIG).astype(BF16)


def _select_blocks_t(imp_t, qpos_row, n_sel_blocks):
    nb = -(-n_sel_blocks // SUBLANES) * SUBLANES
    cols = imp_t.shape[1]
    blk = lax.broadcasted_iota(jnp.int32, (nb, cols), 0)
    score = _block_scores(imp_t[:nb], blk, qpos_row // SEL_BLOCK, n_sel_blocks)
    rank = jnp.zeros((nb, cols), F32)
    for j in range(n_sel_blocks):
        row = score[j:j + 1, :]
        ge = jnp.where(row >= score, 1.0, 0.0)
        gt = jnp.where(row > score, 1.0, 0.0)
        rank = rank + jnp.where(blk > j, ge, gt)
    chosen = jnp.where(rank < float(min(N_SEL, n_sel_blocks)), score, -1.0) >= 0.0
    neg_t = jnp.where(chosen, 0.0, -MASK_BIG)
    neg_t = jnp.concatenate([neg_t, jnp.zeros((LANES - nb, cols), F32)], axis=0)
    return neg_t.T.astype(BF16)


def _gate_columns(sg, r, n_rep, tq):
    lane = lambda g, c: 2 * ML_HEADS + (g * n_rep + r) * 3 + c
    return [jnp.concatenate([sg[:, lane(g, c):lane(g, c) + 1] for g in range(NSA_KV_HEADS)], axis=0)
            for c in range(3)]


def _gated_fold(terms, own, tq):
    comb = None
    for gate, branch in terms:
        if isinstance(branch, (tuple, list)):
            _, l, acc = branch
            term = (gate * (1.0 / jnp.maximum(l, 1e-30))) * acc
        else:
            term = gate * branch
        comb = term if comb is None else comb + term
    kept = jnp.where(own, comb, 0.0)
    piece = kept[:tq]
    for g in range(1, NSA_KV_HEADS):
        piece = piece + kept[g * tq:(g + 1) * tq]
    return piece


def _combine_slab(o_cmp, sel, win, gates, own, tq):
    return _gated_fold([(gates[0], o_cmp), (gates[1], sel), (gates[2], win)], own, tq)


def _compress_prompt_kernel(kc_ref, vc_ref, pe_ref, w1_ref, b1_ref, w2_ref, b2_ref, perm_ref, ko_ref, vo_ref):
    T = kc_ref.shape[2]
    groups = [[src[0, :, t * LANES:(t + 1) * LANES] for t in range(T // LANES)] for src in (kc_ref, vc_ref)]
    biases = [_pe_bias(slot, pe_ref, w1_ref, b1_ref)[0:1] for slot in range(2)]
    ko_ref[0], vo_ref[0] = _compress_slots(perm_ref[...], groups, biases, w1_ref, w2_ref, b2_ref)


def _compress_prompt(kvw_t, cw, *, B, T):
    nrow = T // CMP_STRIDE
    assert (T - CMP_BLOCK) // CMP_STRIDE + 1 <= nrow and T % LANES == 0
    full = lambda a: pl.BlockSpec(a.shape, lambda b: (0,) * a.ndim)
    return pl.pallas_call(
        _compress_prompt_kernel,
        grid=(B,),
        in_specs=[pl.BlockSpec((1, GROUP_LANES, T), lambda b: (b, 0, 0)),
                  pl.BlockSpec((1, GROUP_LANES, T), lambda b: (b, 1, 0))] + [full(a) for a in cw],
        out_specs=[pl.BlockSpec((1, nrow, GROUP_LANES), lambda b: (b, 0, 0))] * 2,
        out_shape=[jax.ShapeDtypeStruct((B, nrow, GROUP_LANES), F32)] * 2,
        compiler_params=_params(("parallel",)),
        name="compress_prompt",
    )(kvw_t, kvw_t, *cw)


def _nsa_prompt_kernel(q_ref, sm_ref, kcmp_ref, vcmp_ref, ks_ref, vs_ref, kw_ref, vw_ref,
                       mmap_t_ref, exp_ref, out_ref,
                       kx_ref, vsb_ref, kwb_ref, vwb_ref, *, tq, ck, n_cmp, n_sel_blocks):
    qt = pl.program_id(1)
    n_rep = q_ref.shape[1] // GROUP_LANES
    slab = NSA_KV_HEADS * tq
    n_chunks = kx_ref.shape[0]

    @pl.when(qt == 0)
    def _():
        for c in range(n_chunks):
            cs = slice(c * ck, (c + 1) * ck)
            kx_ref[c, :GROUP_LANES, :] = ks_ref[0, :, cs].astype(BF16)
            kx_ref[c, GROUP_LANES:, :] = exp_ref[c]
            vsb_ref[c] = vs_ref[0, :, cs].astype(BF16)
            kwb_ref[c] = kw_ref[0, :, cs].astype(BF16)
            vwb_ref[c] = vw_ref[0, :, cs].astype(BF16)

    t0 = qt * tq
    qpos = t0 + lax.broadcasted_iota(jnp.int32, (slab, 1), 0) % tq
    qpos_row = t0 + lax.broadcasted_iota(jnp.int32, (1, slab), 1) % tq
    qs, own = _expand_q(q_ref[...], tq)
    sg = jax.nn.sigmoid(sm_ref[...])
    reps = range(n_rep)

    q_all = jnp.concatenate(qs, axis=0)
    kpos0 = lax.broadcasted_iota(jnp.int32, (1, ck), 1)
    c_diag = t0 // ck
    causal = jnp.where(kpos0 + c_diag * ck <= qpos, 0.0, NEG)

    kcmp, vcmp = kcmp_ref[0].astype(BF16), vcmp_ref[0].astype(BF16)
    o_cmp, p_cmp = _cmp_rows(q_all, _cmp_mask(slab, kcmp.shape[0], qpos, n_cmp), kcmp, vcmp)
    imp_rows_t = _dot_nt(mmap_t_ref[...], p_cmp)

    n_win = -(-(WINDOW - 1) // ck) + 1
    idx = [jnp.maximum(c_diag - (n_win - 1 - j), 0) for j in range(n_win)]
    gone = [jnp.where(c_diag >= n_win - 1 - j, 0.0, NEG) for j in range(n_win)]
    scores = []
    for j in range(n_win):
        s = _dot(q_all, kwb_ref[idx[j]])
        if j == n_win - 1:
            s = _add_slab_bias(s, causal)
        elif j == 0:
            s = _add_slab_bias(s, jnp.where(kpos0 + idx[0] * ck > qpos - WINDOW, 0.0, NEG) + gone[0])
        else:
            s = s + gone[j]
        scores.append(s)
    win = _softmax_values(scores, [(vwb_ref[idx[j]], True) for j in range(n_win)])

    imp_t = imp_rows_t[:, :slab]
    for r in range(1, n_rep):
        imp_t = imp_t + imp_rows_t[:, r * slab:(r + 1) * slab]
    sel_neg = _select_blocks_t(imp_t, qpos_row, n_sel_blocks)
    qx = jnp.concatenate([q_all, jnp.concatenate([sel_neg] * n_rep, axis=0)], axis=1)

    def sel_body(i, carry):
        return _attend(carry, qx, [(kx_ref[2 * i], vsb_ref[2 * i], None), (kx_ref[2 * i + 1], vsb_ref[2 * i + 1], None)])

    sel = lax.fori_loop(0, c_diag // 2, sel_body, _attend_init(n_rep * slab))
    odd = jnp.maximum(c_diag - 1, 0)
    no_odd = jnp.where(c_diag % 2 == 1, 0.0, NEG)
    sel = _attend(sel, qx, [(kx_ref[odd], vsb_ref[odd], no_odd), (kx_ref[c_diag], vsb_ref[c_diag], causal)])

    for r in reps:
        rs = slice(r * slab, (r + 1) * slab)
        piece = _combine_slab(o_cmp[rs], [a[rs] for a in sel], [a[rs] for a in win],
                              _gate_columns(sg, r, n_rep, tq), own, tq)
        out_ref[:, r * GROUP_LANES:(r + 1) * GROUP_LANES] = piece.astype(out_ref.dtype)


def _cmp_to_sel(n_cmp, n_sel_blocks):
    r = SEL_BLOCK // CMP_STRIDE
    c = CMP_BLOCK // CMP_STRIDE
    m = np.zeros((LANES, LANES), np.float32)
    for j in range(n_sel_blocks):
        for a in range(r):
            for b in range(c):
                i = r * j + a - b
                if 0 <= i < n_cmp:
                    m[i, j] += 1.0
    return jnp.asarray(m, BF16)


def _block_expand(n_chunks, ck, n_sel_blocks):
    e = np.zeros((n_chunks, LANES, ck), np.float32)
    key = np.arange(n_chunks * ck).reshape(n_chunks, ck)
    for c in range(n_chunks):
        e[c, key[c] // SEL_BLOCK, np.arange(ck)] = 1.0
    e[:, n_sel_blocks:, :] = 0.0
    return jnp.asarray(e, BF16)


def _nsa_prompt(q, small, kcmp, vcmp, kv_t, win_t, *, B, T, tq, ck):
    M, QW = q.shape
    n_rep = QW // GROUP_LANES
    nqt = T // tq
    n_cmp = (T - CMP_BLOCK) // CMP_STRIDE + 1
    n_sel_blocks = -(-T // SEL_BLOCK)
    assert T % ck == 0 and ck % tq == 0 and n_sel_blocks <= LANES and kcmp.shape[1] <= LANES
    assert WINDOW % ck == 0
    mmap_t = _cmp_to_sel(n_cmp, n_sel_blocks).T[:, :kcmp.shape[1]]
    expand = _block_expand(T // ck, ck, n_sel_blocks)
    const = lambda a: pl.BlockSpec(a.shape, lambda b, t: (0,) * a.ndim)
    kv_spec = lambda slot: pl.BlockSpec((1, GROUP_LANES, T), lambda b, t: (b, slot, 0))
    chunks = lambda kdim: pltpu.VMEM((T // ck, kdim, ck), BF16)
    return pl.pallas_call(
        functools.partial(_nsa_prompt_kernel, tq=tq, ck=ck, n_cmp=n_cmp, n_sel_blocks=n_sel_blocks),
        grid=(B, nqt),
        in_specs=[pl.BlockSpec((tq, QW), lambda b, t: (b * nqt + t, 0)),
                  pl.BlockSpec((tq, LANES), lambda b, t: (b * nqt + t, 0)),
                  pl.BlockSpec((1,) + kcmp.shape[1:], lambda b, t: (b, 0, 0)),
                  pl.BlockSpec((1,) + vcmp.shape[1:], lambda b, t: (b, 0, 0)),
                  kv_spec(2), kv_spec(3), kv_spec(0), kv_spec(1),
                  const(mmap_t), const(expand)],
        out_specs=pl.BlockSpec((tq, QW), lambda b, t: (b * nqt + t, 0)),
        out_shape=jax.ShapeDtypeStruct((M, QW), BF16),
        scratch_shapes=[chunks(GROUP_LANES + LANES), chunks(GROUP_LANES), chunks(GROUP_LANES), chunks(GROUP_LANES)],
        compiler_params=_params(("arbitrary", "arbitrary")),
        name="nsa_prompt",
    )(q, small, kcmp, vcmp, kv_t, kv_t, win_t, win_t, mmap_t, expand)


def _nsa_decode_kernel(pt_ref, *refs, nb, n_pages, **static):
    del pt_ref
    pages = refs[:nb * n_pages]
    (q_ref, sm_ref, kvn_ref, wn_ref, cw_ref, pe_ref, w1_ref, b1_ref, w2_ref, b2_ref, perm_ref,
     mmap_ref, exp_ref, *outs) = refs[nb * n_pages:]
    out_ref, nw_ref, newk_ref, neww_ref, bias_ref = outs[-5:]

    @pl.when(pl.program_id(0) == 0)
    def _():
        newk_ref[...] = jnp.zeros_like(newk_ref)
        neww_ref[...] = jnp.zeros_like(neww_ref)
        for slot in range(2):
            bias_ref[slot] = _pe_bias(slot, pe_ref, w1_ref, b1_ref)

    shared = (w1_ref, w2_ref, b2_ref, perm_ref, mmap_ref, exp_ref, bias_ref)
    running = [_nsa_decode_seq(pages[j * n_pages:(j + 1) * n_pages],
                               *[r.at[j] for r in (q_ref, sm_ref, kvn_ref, wn_ref, cw_ref)], *shared,
                               *[r.at[j] for r in (out_ref, nw_ref, newk_ref, neww_ref)], **static)
               for j in range(nb)]
    while running:
        for seq in list(running):
            if next(seq, True):
                running.remove(seq)


def _nsa_decode_seq(pages, q_ref, sm_ref, kvn_ref, wn_ref, cw_ref, w1_ref, w2_ref, b2_ref, perm_ref,
                    mmap_ref, exp_ref, bias_ref, out_ref, nw_ref, newk_ref, neww_ref,
                    *, page, tq, t_new, past_len, n_cmp, n_sel_blocks, wb):
    n_pages = len(pages)
    n_rep = q_ref.shape[1] // GROUP_LANES
    slab = NSA_KV_HEADS * tq
    reps = range(n_rep)

    newk_ref[:tq, :] = kvn_ref[...]
    neww_ref[:tq, :] = wn_ref[...]

    qpos = past_len + lax.broadcasted_iota(jnp.int32, (slab, 1), 0) % tq
    qs, own = _expand_q(q_ref[...], tq)
    sg = jax.nn.sigmoid(sm_ref[...])

    cmp_kv = _compress_slots(perm_ref[...], [[pg[0, slot] for pg in pages] for slot in range(2)],
                             [bias_ref[slot, 0:1, :] for slot in range(2)], w1_ref, w2_ref, b2_ref)
    cmp_kv = [a.astype(BF16) for a in cmp_kv]
    yield

    q_all = jnp.concatenate(qs, axis=0)
    o_cmp, p_cmp = _cmp_rows(q_all, _cmp_mask(slab, cmp_kv[0].shape[0], qpos, n_cmp), cmp_kv[0], cmp_kv[1])
    imp_rows = _dot(p_cmp, mmap_ref[...])
    imp = imp_rows[:slab]
    for r in range(1, n_rep):
        imp = imp + imp_rows[r * slab:(r + 1) * slab]
    neg_all = jnp.concatenate([_select_blocks(imp, qpos, n_sel_blocks)] * n_rep, axis=0)
    yield

    kpos0 = lax.broadcasted_iota(jnp.int32, (1, page), 1)
    own_rows = jnp.where(kpos0 + past_len <= jnp.concatenate([qpos] * n_rep, axis=0), 0.0, NEG)

    scores = [_dot(q_all, pages[p][0, 2].astype(BF16)) + _dot(neg_all, exp_ref[p]) for p in range(n_pages)]
    values = [(pages[p][0, 3].astype(BF16), True) for p in range(n_pages)]
    k_new = newk_ref[:, 2 * GROUP_LANES:3 * GROUP_LANES].astype(BF16)
    scores.append(_dot_nt(q_all, k_new) + _dot(neg_all, exp_ref[n_pages]) + own_rows)
    values.append((newk_ref[:, 3 * GROUP_LANES:].astype(BF16), False))
    sel = _softmax_values(scores, values)
    yield

    kpos = past_len - wb + lax.broadcasted_iota(jnp.int32, (1, wb), 1)
    in_band = jnp.where(kpos > jnp.concatenate([qpos] * n_rep, axis=0) - WINDOW, 0.0, NEG)
    scores = [_dot(q_all, cw_ref[0].astype(BF16)) + in_band,
              _dot_nt(q_all, neww_ref[:, :GROUP_LANES].astype(BF16)) + own_rows]
    values = [(cw_ref[1].astype(BF16), True), (neww_ref[:, GROUP_LANES:].astype(BF16), False)]
    win = _softmax_values(scores, values)
    yield

    for r in reps:
        rs = slice(r * slab, (r + 1) * slab)
        piece = _combine_slab(o_cmp[rs], [a[rs] for a in sel], [a[rs] for a in win],
                              _gate_columns(sg, r, n_rep, tq), own, tq)
        out_ref[:, r * GROUP_LANES:(r + 1) * GROUP_LANES] = piece

    new_t = neww_ref[...].T
    lane = lax.broadcasted_iota(jnp.int32, (GROUP_LANES, LANES), 1)
    for slot in range(2):
        shifted = pltpu.roll(cw_ref[slot], wb - t_new, axis=1)
        fresh = pltpu.roll(new_t[slot * GROUP_LANES:(slot + 1) * GROUP_LANES], LANES - t_new, axis=1)
        nw_ref[slot, :, :wb - LANES] = shifted[:, :wb - LANES]
        nw_ref[slot, :, wb - LANES:] = jnp.where(lane >= LANES - t_new, fresh, shifted[:, wb - LANES:])


def _nsa_decode(q8, small8, kvnew8, winnew8, cache, cache_win, page_table, cw, *, past_len, t_new, win_off,
                win_stack=None):
    B, tq, QW = q8.shape
    n_pages = page_table.shape[1]
    page = cache.shape[3]
    wb = cache_win.shape[3]
    tk = past_len + t_new
    n_cmp = (tk - CMP_BLOCK) // CMP_STRIDE + 1
    n_sel_blocks = -(-tk // SEL_BLOCK)
    nrow = past_len // CMP_STRIDE
    assert tq == SUBLANES and past_len == n_pages * page and wb % page == 0 and page == LANES
    assert n_cmp <= nrow <= LANES and (n_cmp - 1) * CMP_STRIDE + CMP_BLOCK <= past_len
    assert n_sel_blocks <= LANES and 0 < t_new < tq and wb == WINDOW
    mmap = _cmp_to_sel(n_cmp, n_sel_blocks)[:nrow]
    expand = _block_expand(n_pages + 1, page, n_sel_blocks)
    nb = min(B, NSA_DECODE_SEQS_PER_STEP)
    assert B % nb == 0 and win_off % nb == 0
    const = lambda a: pl.BlockSpec(a.shape, lambda b, pt: (0,) * a.ndim)
    page_spec = lambda j, p: pl.BlockSpec((1,) + cache.shape[1:], lambda b, pt: (pt[b * nb + j, p], 0, 0, 0))
    per_b = lambda a: pl.BlockSpec((nb,) + a.shape[1:], lambda b, pt: (b, 0, 0))
    win_block = (nb,) + cache_win.shape[1:]
    win_map = lambda b, pt: (win_off // nb + b, 0, 0, 0)
    consts = list(cw) + [mmap, expand]
    operands = [page_table] + [cache] * (nb * n_pages) + [q8, small8, kvnew8, winnew8, cache_win] + consts
    extra_specs, aliases = [], {}
    if win_stack is not None:
        extra_specs = [pl.BlockSpec(memory_space=pl.ANY)]
        aliases = {len(operands): 1}
        operands.append(win_stack)
    grid_spec = pltpu.PrefetchScalarGridSpec(
        num_scalar_prefetch=1,
        grid=(B // nb,),
        in_specs=[page_spec(j, p) for j in range(nb) for p in range(n_pages)]
        + [per_b(q8), per_b(small8), per_b(kvnew8), per_b(winnew8), pl.BlockSpec(win_block, win_map)]
        + [const(a) for a in consts] + extra_specs,
        out_specs=[per_b(q8), pl.BlockSpec(win_block, win_map)],
        scratch_shapes=[pltpu.VMEM((nb, page, kvnew8.shape[2]), F32), pltpu.VMEM((nb, page, winnew8.shape[2]), F32),
                        pltpu.VMEM((2, SUBLANES, cw[2].shape[1]), F32)],
    )
    return pl.pallas_call(
        functools.partial(_nsa_decode_kernel, nb=nb, n_pages=n_pages, page=page, tq=tq, t_new=t_new,
                          past_len=past_len, n_cmp=n_cmp, n_sel_blocks=n_sel_blocks, wb=wb),
        grid_spec=grid_spec,
        input_output_aliases=aliases,
        out_shape=[jax.ShapeDtypeStruct(q8.shape, F32), jax.ShapeDtypeStruct(cache_win.shape, F32)],
        compiler_params=_params(("arbitrary",)),
        name="nsa_decode",
    )(*operands)


def _layer_weights(i, prm, d_ml, d_nsa):
    H, G, dk = ML_HEADS, NSA_KV_HEADS, NSA_HEAD_DIM
    R = d_nsa // (G * dk)
    kvw = G * dk
    w_in = prm["w_in"][i]
    o = 0
    c_in, o = w_in[:, o:o + d_ml], o + d_ml
    v_ml, o = w_in[:, o:o + d_ml], o + d_ml
    o_ml, o = w_in[:, o:o + d_ml], o + d_ml
    i_ml, o = w_in[:, o:o + H], o + H
    f_ml, o = w_in[:, o:o + H], o + H
    q_n, o = w_in[:, o:o + d_nsa], o + d_nsa
    kv, o = w_in[:, o:o + 6 * kvw], o + 6 * kvw
    g_n = w_in[:, o:]
    K = w_in.shape[0]
    q_perm = q_n.reshape(K, G, R, dk).transpose(0, 2, 1, 3).reshape(K, d_nsa)
    w_main = jnp.concatenate([c_in, v_ml, o_ml, q_perm, kv], axis=1).astype(BF16)
    w_kvw_t = kv.T.astype(BF16)
    small = jnp.concatenate([i_ml, f_ml, g_n], axis=1)
    w_small = jnp.pad(small, ((0, 0), (0, LANES - small.shape[1]))).astype(BF16)
    w_out = prm["w_out"][i]
    w_out_nsa = w_out[d_ml:].reshape(G, R, dk, -1).transpose(1, 0, 2, 3).reshape(d_nsa, -1)
    w_out_p = jnp.concatenate([w_out[:d_ml], w_out_nsa], axis=0).astype(BF16)
    Dh = d_ml // H
    gate_bias = jnp.pad(jnp.concatenate([prm["b_i"][i], prm["b_f"][i]]), (0, LANES - 2 * H)).reshape(1, LANES)
    w2 = prm["cmp_w2"][i]
    w2_placed = jnp.stack([jnp.stack([jnp.pad(w2[s], ((0, 0), (g * dk, (G - 1 - g) * dk))) for g in range(G)])
                           for s in range(2)]).astype(BF16)
    perm = np.zeros((LANES, LANES), np.float32)
    tok = np.arange(LANES)
    perm[(tok % CMP_STRIDE) * (LANES // CMP_STRIDE) + tok // CMP_STRIDE, tok] = 1.0
    cmp_w = (prm["cmp_pe"][i].reshape(2, -1), prm["cmp_w1"][i].astype(BF16), prm["cmp_b1"][i], w2_placed,
             jnp.tile(prm["cmp_b2"][i], (1, G)), jnp.asarray(perm, BF16))
    return dict(
        w_main=w_main, w_kvw_t=w_kvw_t, w_small=w_small, w_out=w_out_p,
        w_up=prm["w_up_all"], w_down=prm["w_down_all"],
        w_pl=prm["w_pl"][i].astype(BF16), w_pl_gate=prm["w_pl_gate"][i].astype(BF16),
        conv_w=prm["conv_w"][i], conv_b=prm["conv_b"][i].reshape(1, -1),
        wq=prm["w_q_ml"][i].astype(BF16),
        wkt=(jnp.swapaxes(prm["w_k_ml"][i], 1, 2) * (Dh ** -0.5)).astype(BF16),
        gate_bias=gate_bias, g_ml=prm["g_ml"][i].reshape(1, -1), cmp=cmp_w)


def _pad_time(a, B, t, tp):
    return jnp.pad(a.reshape(B, t, -1), ((0, 0), (0, tp - t), (0, 0)))


def _layer(h, pl_e, i, prm, lw, mem, *, B, T):
    M, D = h.shape
    G, dk, H = NSA_KV_HEADS, NSA_HEAD_DIM, ML_HEADS
    d_ml = lw["conv_w"].shape[1]
    d_nsa = lw["w_out"].shape[0] - d_ml
    R = d_nsa // (G * dk)
    Dh = d_ml // H
    kvw = G * dk
    tm = min(M, 512)
    g_pre = prm["g_pre_mix"][i]
    lanes_last = lambda a, lead: a.reshape(lead + (G, dk, a.shape[-1]))

    ml_w = (lw["conv_w"], lw["conv_b"], lw["wq"], lw["wkt"], lw["gate_bias"], lw["g_ml"])
    if mem is None:
        cin, v, o, q, small = _norm_matmul(h, g_pre, lw["w_main"], (d_ml, d_ml, d_ml, d_nsa),
                                           w_small=lw["w_small"], tm=min(M, 1024), tn=512)
        kv_t, win_t = _norm_matmul_t(h, g_pre, lw["w_kvw_t"], (4 * kvw, 2 * kvw), B=B, T=T, tm=min(T, 1024), tn=512)
        zeros = lambda *s: jnp.zeros(s, F32)
        hm, c_new, n_new, m_new = _mlstm(
            cin, v, o, small, zeros(B, SUBLANES, d_ml), zeros(B, H, Dh, Dh), zeros(B, 1, d_ml), zeros(B, 1, LANES),
            *ml_w, B=B, rows=ML_CHUNK, valid=ML_CHUNK)
        kcmp, vcmp = _compress_prompt(kv_t, lw["cmp"], B=B, T=T)
        on = _nsa_prompt(q, small, kcmp, vcmp, kv_t, win_t, B=B, T=T, tq=64, ck=256)
        wlen = min(WINDOW, T)
        new_rows = lanes_last(kv_t, (B, 4)).transpose(0, 4, 1, 2, 3)
        new_win = lanes_last(win_t[:, :, T - wlen:], (B, 2)).transpose(0, 4, 1, 2, 3)
    else:
        cin, v, o, q, kv4, win2, small = _norm_matmul(
            h, g_pre, lw["w_main"], (d_ml, d_ml, d_ml, d_nsa, 4 * kvw, 2 * kvw),
            w_small=lw["w_small"], tm=tm, tn=512)
        tp = SUBLANES
        pad = lambda a: _pad_time(a, B, T, tp)
        prev = jnp.pad(mem["conv"], ((0, 0), (SUBLANES - mem["conv"].shape[1], 0), (0, 0)))
        m0 = jnp.pad(mem["m"], ((0, 0), (0, LANES - H))).reshape(B, 1, LANES)
        hm8, c_new, n_new, m_new = _mlstm(
            pad(cin).reshape(B * tp, d_ml), pad(v).reshape(B * tp, d_ml), pad(o).reshape(B * tp, d_ml),
            pad(small).reshape(B * tp, LANES), prev, mem["C"], mem["n"].reshape(B, 1, d_ml), m0,
            *ml_w, B=B, rows=tp, valid=T, c_off=i * B, c_layers=mem["C"].shape[0] // B, c_stack=mem["c_stack"])
        hm = hm8.reshape(B, tp, d_ml)[:, :T].reshape(M, d_ml)
        page = mem["kv"].shape[3]
        on8, new_win = _nsa_decode(
            pad(q), pad(small), pad(kv4), pad(win2),
            mem["kv"], mem["win"], mem["page_table"], lw["cmp"],
            past_len=mem["page_table"].shape[1] * page, t_new=T, win_off=i * B, win_stack=mem["win_stack"])
        on = on8[:, :T].reshape(M, d_nsa)
        new_rows = kv4.reshape(B, T, 4, G, dk)

    h = _matmul_norm_res([hm, on], lw["w_out"], h, prm["g_post_mix"][i], tm=tm, tk=d_ml + d_nsa)
    u, = _norm_matmul(h, prm["g_pre_mlp"][i], lw["w_up"], (lw["w_up"].shape[1],), tm=min(M, 1024), tn=1024,
                      act="relu2", out_dtype=BF16, layer=i)
    h = _matmul_norm_res([u], lw["w_down"], h, prm["g_post_mlp"][i], tm=tm, tk=2048, layer=i)
    h = _ple(h, pl_e, lw["w_pl_gate"], lw["w_pl"], prm["g_pl"][i], tm=min(M, 256), layer=i)

    state = (new_rows, new_win, c_new, n_new.reshape(B, H, Dh), m_new[:, 0, :H], cin.reshape(B, T, d_ml)[:, T - 3:])
    return h, state


def kernel(x_prompt, x_sample, cache_kv, cache_win, state_C, state_n, state_m, state_conv, page_table,
           p_prompt, p_sample, g_pre_mix, w_in, conv_w, conv_b, w_q_ml, w_k_ml, b_i, b_f, g_ml,
           cmp_pe, cmp_w1, cmp_b1, cmp_w2, cmp_b2, w_out, g_post_mix, g_pre_mlp, w_up, w_down,
           g_post_mlp, w_pl, g_pl, w_pl_gate):
    prm = dict(g_pre_mix=g_pre_mix, w_in=w_in, conv_w=conv_w, conv_b=conv_b, w_q_ml=w_q_ml, w_k_ml=w_k_ml,
               b_i=b_i, b_f=b_f, g_ml=g_ml, cmp_pe=cmp_pe, cmp_w1=cmp_w1, cmp_b1=cmp_b1, cmp_w2=cmp_w2,
               cmp_b2=cmp_b2, w_out=w_out, g_post_mix=g_post_mix, g_pre_mlp=g_pre_mlp, w_up=w_up,
               w_down=w_down, g_post_mlp=g_post_mlp, w_pl=w_pl, g_pl=g_pl, w_pl_gate=w_pl_gate)
    Bp, Tp, D = x_prompt.shape
    Bs, Ts, _ = x_sample.shape
    depth = w_in.shape[0]
    d_ml = conv_w.shape[2]
    d_nsa = w_out.shape[1] - d_ml
    hp = x_prompt.reshape(Bp * Tp, D)
    hs = x_sample.reshape(Bs * Ts, D)
    n_pool, page = cache_kv.shape[1:3]
    kv_t = jnp.transpose(cache_kv, (0, 1, 3, 4, 5, 2)).reshape(depth * n_pool, cache_kv.shape[3], -1, page)
    win_t = jnp.transpose(cache_win, (0, 1, 3, 4, 5, 2)).reshape(depth * Bs, cache_win.shape[3], -1, cache_win.shape[2])
    c_all = state_C.reshape((depth * Bs,) + state_C.shape[2:])
    prm["w_up_all"] = w_up.astype(BF16).reshape(-1, w_up.shape[2])
    prm["w_down_all"] = w_down.astype(BF16).reshape(-1, w_down.shape[2])
    sp, ss = [], []
    c_stack = win_stack = None
    for i in range(depth):
        lw = _layer_weights(i, prm, d_ml, d_nsa)
        mem = dict(kv=kv_t, page_table=page_table + i * n_pool, win=win_t, C=c_all, n=state_n[i],
                   m=state_m[i], conv=state_conv[i], c_stack=c_stack, win_stack=win_stack)
        hp, st_p = _layer(hp, p_prompt.reshape(depth * Bp * Tp, -1), i, prm, lw, None, B=Bp, T=Tp)
        hs, st_s = _layer(hs, p_sample.reshape(depth * Bs * Ts, -1), i, prm, lw, mem, B=Bs, T=Ts)
        win_stack, c_stack = st_s[1], st_s[2]
        sp.append(st_p)
        ss.append(st_s)

    stk = lambda lst, j: jnp.stack([s[j] for s in lst])
    G, dk = NSA_KV_HEADS, NSA_HEAD_DIM
    win_sample = win_stack.reshape(depth, Bs, win_stack.shape[1], G, dk, win_stack.shape[3]).transpose(0, 1, 5, 2, 3, 4)
    return (hp.reshape(Bp, Tp, D), hs.reshape(Bs, Ts, D), stk(sp, 0), stk(ss, 0), stk(sp, 1), win_sample,
            stk(sp, 2), c_stack.reshape(state_C.shape), stk(sp, 3), stk(ss, 3), stk(sp, 4), stk(ss, 4),
            stk(sp, 5), stk(ss, 5))
```

```python
import functools

import numpy as np
import jax
import jax.numpy as jnp
from jax import lax
from jax.experimental import pallas as pl
from jax.experimental.pallas import tpu as pltpu

ML_HEADS = 4
NSA_HEAD_DIM = 64
NSA_KV_HEADS = 4
CMP_BLOCK = 32
CMP_STRIDE = 16
SEL_BLOCK = 64
N_SEL = 16
WINDOW = 512
EPS = 1e-6
NEG = -1e30
FORCE = 1e6

LANES = 128
SUBLANES = 8
VMEM_LIMIT_BYTES = 56 * 1024 * 1024

ML_CHUNK = 128
NSA_DECODE_SEQS_PER_STEP = 2
NSA_PROMPT_TILES_PER_STEP = 2
MLSTM_SEQS_PER_STEP_PROMPT = 2
MLSTM_SEQS_PER_STEP_DECODE = 4
GROUP_LANES = NSA_KV_HEADS * NSA_HEAD_DIM
QK_SCALE = NSA_HEAD_DIM ** -0.5 * 1.4426950408889634
MASK_BIG = 1e30

F32 = jnp.float32
BF16 = jnp.bfloat16


def _dot(a, b):
    return jnp.dot(a, b, preferred_element_type=F32)


def _dot_nt(a, b):
    return lax.dot_general(a, b, (((1,), (1,)), ((), ())), preferred_element_type=F32)


def _params(semantics):
    return pltpu.CompilerParams(dimension_semantics=semantics, vmem_limit_bytes=VMEM_LIMIT_BYTES)


def _rms_rows(x):
    return x * lax.rsqrt(jnp.mean(x * x, axis=-1, keepdims=True) + EPS)


def _norm_matmul_kernel(x_ref, g_ref, w_ref, *rest, seg_tiles, has_small, act):
    if has_small:
        ws_ref, rest = rest[0], rest[1:]
    n_out = len(seg_tiles) + (1 if has_small else 0)
    out_refs, xn_ref = rest[:n_out], rest[n_out]
    j = pl.program_id(1)

    @pl.when(j == 0)
    def _():
        xn = (_rms_rows(x_ref[...]) * g_ref[...]).astype(BF16)
        xn_ref[...] = xn
        if has_small:
            out_refs[-1][...] = _dot(xn, ws_ref[...])

    y = _dot(xn_ref[...], w_ref[...])
    if act == "relu2":
        y = jnp.square(jnp.maximum(y, 0.0))
    start = 0
    for k, n in enumerate(seg_tiles):
        @pl.when((j >= start) & (j < start + n))
        def _(k=k):
            out_refs[k][...] = y.astype(out_refs[k].dtype)
        start += n


def _norm_matmul(x, g, w, seg_cols, *, w_small=None, tm, tn, act=None, out_dtype=F32, layer=0):
    M, K = x.shape
    seg_tiles = tuple(c // tn for c in seg_cols)
    assert all(c % tn == 0 for c in seg_cols) and M % tm == 0
    has_small = w_small is not None
    starts = np.concatenate([[0], np.cumsum(seg_tiles)[:-1]]).tolist()

    def out_map(start, n):
        return lambda i, j: (i, jnp.clip(j - start, 0, n - 1))

    in_specs = [pl.BlockSpec((tm, K), lambda i, j: (i, 0)),
                pl.BlockSpec((1, K), lambda i, j: (0, 0)),
                pl.BlockSpec((K, tn), lambda i, j: (layer, j))]
    args = [x, g.reshape(1, K), w]
    out_specs = [pl.BlockSpec((tm, tn), out_map(s, n)) for s, n in zip(starts, seg_tiles)]
    out_shape = [jax.ShapeDtypeStruct((M, c), out_dtype) for c in seg_cols]
    if has_small:
        in_specs.append(pl.BlockSpec((K, LANES), lambda i, j: (0, 0)))
        args.append(w_small)
        out_specs.append(pl.BlockSpec((tm, LANES), lambda i, j: (i, 0)))
        out_shape.append(jax.ShapeDtypeStruct((M, LANES), F32))
    return pl.pallas_call(
        functools.partial(_norm_matmul_kernel, seg_tiles=seg_tiles, has_small=has_small, act=act),
        grid=(M // tm, sum(seg_tiles)),
        in_specs=in_specs, out_specs=out_specs, out_shape=out_shape,
        scratch_shapes=[pltpu.VMEM((tm, K), BF16)],
        compiler_params=_params(("parallel", "arbitrary")),
        name="norm_matmul",
    )(*args)


def _norm_matmul_t_kernel(x_ref, g_ref, wt_ref, *rest, seg_tiles):
    out_refs, xn_ref = rest[:-1], rest[-1]
    j = pl.program_id(1)

    @pl.when(j == 0)
    def _():
        xn_ref[...] = (_rms_rows(x_ref[...]) * g_ref[...]).astype(BF16)

    y = _dot_nt(wt_ref[...], xn_ref[...])
    start = 0
    for k, n in enumerate(seg_tiles):
        @pl.when((j >= start) & (j < start + n))
        def _(k=k):
            out_refs[k][0] = y
        start += n


def _norm_matmul_t(x, g, wt, seg_rows, *, B, T, tm, tn):
    M, K = x.shape
    seg_tiles = tuple(r // tn for r in seg_rows)
    assert T % tm == 0 and all(r % tn == 0 for r in seg_rows) and M == B * T
    per = T // tm
    starts = np.concatenate([[0], np.cumsum(seg_tiles)[:-1]]).tolist()

    def out_map(start, n):
        return lambda i, j: (i // per, jnp.clip(j - start, 0, n - 1), i % per)

    return pl.pallas_call(
        functools.partial(_norm_matmul_t_kernel, seg_tiles=seg_tiles),
        grid=(M // tm, sum(seg_tiles)),
        in_specs=[pl.BlockSpec((tm, K), lambda i, j: (i, 0)),
                  pl.BlockSpec((1, K), lambda i, j: (0, 0)),
                  pl.BlockSpec((tn, K), lambda i, j: (j, 0))],
        out_specs=[pl.BlockSpec((1, tn, tm), out_map(s, n)) for s, n in zip(starts, seg_tiles)],
        out_shape=[jax.ShapeDtypeStruct((B, r, T), F32) for r in seg_rows],
        scratch_shapes=[pltpu.VMEM((tm, K), BF16)],
        compiler_params=_params(("parallel", "arbitrary")),
        name="norm_matmul_t",
    )(x, g.reshape(1, K), wt)


def _matmul_norm_res_kernel(*refs, n_a):
    a_refs = refs[:n_a]
    w_ref, h_ref, g_ref, o_ref, acc_ref = refs[n_a:]
    k = pl.program_id(1)

    @pl.when(k == 0)
    def _():
        acc_ref[...] = jnp.zeros_like(acc_ref)

    off = 0
    for a_ref in a_refs:
        width = a_ref.shape[1]
        acc_ref[...] += _dot(a_ref[...].astype(BF16), w_ref[off:off + width, :])
        off += width

    @pl.when(k == pl.num_programs(1) - 1)
    def _():
        o_ref[...] = h_ref[...] + _rms_rows(acc_ref[...]) * g_ref[...]


def _matmul_norm_res(a_parts, w, h, g, *, tm, tk, layer=0):
    M = a_parts[0].shape[0]
    K = sum(a.shape[1] for a in a_parts)
    N = w.shape[1]
    assert M % tm == 0 and K % tk == 0 and (len(a_parts) == 1 or tk == K)
    k_steps = K // tk
    a_specs = [pl.BlockSpec((tm, tk if len(a_parts) == 1 else a.shape[1]), lambda i, k: (i, k)) for a in a_parts]
    return pl.pallas_call(
        functools.partial(_matmul_norm_res_kernel, n_a=len(a_parts)),
        grid=(M // tm, k_steps),
        in_specs=a_specs + [pl.BlockSpec((tk, N), lambda i, k: (layer * k_steps + k, 0)),
                            pl.BlockSpec((tm, N), lambda i, k: (i, 0)),
                            pl.BlockSpec((1, N), lambda i, k: (0, 0))],
        out_specs=pl.BlockSpec((tm, N), lambda i, k: (i, 0)),
        out_shape=jax.ShapeDtypeStruct((M, N), F32),
        scratch_shapes=[pltpu.VMEM((tm, N), F32)],
        compiler_params=_params(("parallel", "arbitrary")),
        name="matmul_norm_res",
    )(*a_parts, w, h, g.reshape(1, N))


def _ple_kernel(h_ref, p_ref, wg_ref, wp_ref, g_ref, o_ref):
    h = h_ref[...]
    gate = jax.nn.sigmoid(_dot(_rms_rows(h).astype(BF16), wg_ref[...]))
    e = _dot(p_ref[...].astype(BF16), wp_ref[...])
    o_ref[...] = h + gate * (_rms_rows(e) * g_ref[...])


def _ple(h, p, wg, wp, g, *, tm, layer=0):
    M, N = h.shape
    P = p.shape[1]
    assert M % tm == 0
    return pl.pallas_call(
        _ple_kernel,
        grid=(M // tm,),
        in_specs=[pl.BlockSpec((tm, N), lambda i: (i, 0)),
                  pl.BlockSpec((tm, P), lambda i: (layer * (M // tm) + i, 0)),
                  pl.BlockSpec((N, N), lambda i: (0, 0)),
                  pl.BlockSpec((P, N), lambda i: (0, 0)),
                  pl.BlockSpec((1, N), lambda i: (0, 0))],
        out_specs=pl.BlockSpec((tm, N), lambda i: (i, 0)),
        out_shape=jax.ShapeDtypeStruct((M, N), F32),
        compiler_params=_params(("parallel",)),
        name="ple",
    )(h, p, wg, wp, g.reshape(1, N))


def _shift_rows(x, prev8, s):
    rows = x.shape[0]
    xs = pltpu.roll(x, s, axis=0)
    rid = lax.broadcasted_iota(jnp.int32, (SUBLANES, x.shape[1]), 0)
    head = jnp.where(rid < s, pltpu.roll(prev8, s, axis=0), xs[:SUBLANES])
    if rows == SUBLANES:
        return head
    return jnp.concatenate([head, xs[SUBLANES:]], axis=0)


def _mlstm_kernel(*refs, nb, aliased, **static):
    n_seq, n_shared = 8, 7
    seq_in, shared = refs[:n_seq], refs[n_seq:n_seq + n_shared]
    rest = refs[n_seq + n_shared + (1 if aliased else 0):]
    per_seq = [[r.at[j] for r in seq_in] + list(shared) + [r.at[j] for r in rest] for j in range(nb)]
    for args in per_seq:
        _mlstm_load_state(*args, **static)
    m_rows = [None] * nb
    running = {j: _mlstm_seq(*args, **static) for j, args in enumerate(per_seq)}
    while running:
        for j in list(running):
            try:
                next(running[j])
            except StopIteration as done:
                m_rows[j] = done.value
                del running[j]
    for args, m_row in zip(per_seq, m_rows):
        _mlstm_store_state(*args, m_row, **static)


def _mlstm_load_state(cin_ref, v_ref, o_ref, sm_ref, prev_ref, c0_ref, n0_ref, m0_ref,
                      cw_ref, cb_ref, wq_ref, wkt_ref, gb_ref, gml_ref, ltri_ref,
                      hm_ref, cout_ref, nout_ref, mout_ref,
                      carry_ref, cext_ref, m_ref, chpad_ref, vpad_ref, gpad_ref,
                      *, L, rows, valid, H, Dh):
    @pl.when(pl.program_id(1) == 0)
    def _():
        carry_ref[...] = prev_ref[...]
        m_ref[...] = m0_ref[...]
        for h in range(H):
            cext_ref[h, :, :Dh] = c0_ref[h]
            n_row = n0_ref[:, h * Dh:(h + 1) * Dh]
            cext_ref[h, :, Dh:] = jnp.broadcast_to(n_row, (LANES, Dh)).T

    if rows < L:
        @pl.when((pl.program_id(0) == 0) & (pl.program_id(1) == 0))
        def _():
            chpad_ref[...] = jnp.zeros_like(chpad_ref)
            vpad_ref[...] = jnp.zeros_like(vpad_ref)
            gpad_ref[...] = jnp.zeros_like(gpad_ref)


def _mlstm_store_state(cin_ref, v_ref, o_ref, sm_ref, prev_ref, c0_ref, n0_ref, m0_ref,
                       cw_ref, cb_ref, wq_ref, wkt_ref, gb_ref, gml_ref, ltri_ref,
                       hm_ref, cout_ref, nout_ref, mout_ref,
                       carry_ref, cext_ref, m_ref, chpad_ref, vpad_ref, gpad_ref, m_new_row,
                       *, L, rows, valid, H, Dh):
    @pl.when(pl.program_id(1) == pl.num_programs(1) - 1)
    def _():
        mout_ref[...] = m_new_row
        for h in range(H):
            cout_ref[h] = cext_ref[h, :, :Dh]
            nout_ref[:, h * Dh:(h + 1) * Dh] = cext_ref[h, :, Dh:].T[0:1, :]


def _mlstm_seq(cin_ref, v_ref, o_ref, sm_ref, prev_ref, c0_ref, n0_ref, m0_ref,
               cw_ref, cb_ref, wq_ref, wkt_ref, gb_ref, gml_ref, ltri_ref,
               hm_ref, cout_ref, nout_ref, mout_ref,
               carry_ref, cext_ref, m_ref, chpad_ref, vpad_ref, gpad_ref,
               *, L, rows, valid, H, Dh):
    x = cin_ref[...]
    prev8 = carry_ref[...]
    conv = cb_ref[...] + cw_ref[3:4, :] * x
    for s in (1, 2, 3):
        conv = conv + cw_ref[3 - s:4 - s, :] * _shift_rows(x, prev8, s)
    carry_ref[...] = x[rows - SUBLANES:, :]
    ch = conv * jax.nn.sigmoid(conv)

    if rows < L:
        chpad_ref[:rows, :] = ch
        vpad_ref[:rows, :] = v_ref[...]
        gpad_ref[:rows, :] = sm_ref[...]
        ch, v_all, sm = chpad_ref[...], vpad_ref[...], gpad_ref[...]
    else:
        v_all, sm = v_ref[...], sm_ref[...]
    ch = ch.astype(BF16)

    lane = lax.broadcasted_iota(jnp.int32, (L, LANES), 1)
    rid = lax.broadcasted_iota(jnp.int32, (L, LANES), 0)
    pre = sm + gb_ref[...]
    gates = jnp.where(lane < H, pre, jax.nn.log_sigmoid(pre))
    gates = jnp.where(rid < valid, gates, jnp.where(lane < H, NEG, 0.0))
    hi = gates.astype(BF16)
    r1 = gates - hi.astype(F32)
    mid = r1.astype(BF16)
    lo = (r1 - mid.astype(F32)).astype(BF16)
    ltri = ltri_ref[...]
    bc_all = _dot(ltri, hi) + _dot(ltri, mid) + _dot(ltri, lo)
    colform = jnp.where(lane < H, gates, bc_all)
    rowform = colform.T

    r_i = lax.broadcasted_iota(jnp.int32, (L, L), 0)
    c_i = lax.broadcasted_iota(jnp.int32, (L, L), 1)
    causal = c_i <= r_i
    lane1 = lax.broadcasted_iota(jnp.int32, (1, LANES), 1)
    m_row = m_ref[...]
    m_new_row = m_row
    ones = jnp.ones((L, LANES), F32)

    yield
    heads = range(H)
    hsl = [slice(h * Dh, (h + 1) * Dh) for h in heads]
    q = [_dot(ch[:, hsl[h]], wq_ref[h]).astype(BF16) for h in heads]
    kt = [_dot_nt(wkt_ref[h], ch[:, hsl[h]]).astype(BF16) for h in heads]
    cext = [cext_ref[h] for h in heads]
    qc = [_dot(q[h], cext[h].astype(BF16)) for h in heads]
    qk = [_dot(q[h], kt[h]) for h in heads]
    yield

    bc_col = [colform[:, H + h:H + h + 1] for h in heads]
    m_h = [m_row[:, h:h + 1] for h in heads]
    mt, s_w, a_int = [], [], []
    for h in heads:
        logd = jnp.where(causal, bc_col[h] - rowform[H + h:H + h + 1, :] + rowform[h:h + 1, :], NEG)
        inter = m_h[h] + bc_col[h]
        mt.append(jnp.maximum(jnp.max(logd, axis=-1, keepdims=True), inter))
        s_w.append((qk[h] * jnp.exp(logd - mt[h])).astype(BF16))
        a_int.append(jnp.exp(inter - mt[h]))
    v_ext = [jnp.concatenate([v_all[:, hsl[h]], ones], axis=1) for h in heads]
    sv = [_dot(s_w[h], v_ext[h].astype(BF16)) for h in heads]
    yield

    for h in heads:
        m_last = mt[h][L - 1:L, :]
        bc_last = bc_col[h][L - 1:L, :]
        w_col = jnp.exp(bc_last - bc_col[h] + colform[:, h:h + 1] - m_last)
        dec = jnp.exp(m_h[h] + bc_last - m_last)
        cext_ref[h] = dec * cext[h] + _dot(kt[h], (w_col * v_ext[h]).astype(BF16))
        m_new_row = jnp.where(lane1 == h, m_last, m_new_row)
    yield

    for h in heads:
        tot = sv[h] + a_int[h] * qc[h]
        hh = tot[:, :Dh] / jnp.maximum(jnp.abs(tot[:, Dh:Dh + 1]), jnp.exp(-mt[h]))
        y = _rms_rows(hh) * gml_ref[:, hsl[h]]
        hm_ref[:, hsl[h]] = y[:rows] * jax.nn.sigmoid(o_ref[:, hsl[h]])

    m_ref[...] = m_new_row
    return m_new_row


def _mlstm(cin, v, o, small, prev, c0, n0, m0, cw, cb, wq, wkt, gb, gml, *, B, rows, valid, c_off=0,
           c_layers=1, c_stack=None):
    L = ML_CHUNK
    H = ML_HEADS
    D = cin.shape[1]
    Dh = D // H
    nc = cin.shape[0] // (B * rows)
    nb = min(B, MLSTM_SEQS_PER_STEP_PROMPT if rows == L else MLSTM_SEQS_PER_STEP_DECODE)
    assert rows == L or (nc == 1 and rows == SUBLANES)
    assert B % nb == 0 and c_off % nb == 0
    ltri = jnp.asarray(np.tril(np.ones((L, L), np.float32)), BF16)
    seq3 = lambda a: a.reshape(B, nc * rows, a.shape[1])
    tok = lambda g, c: (g, c, 0)
    per_g3 = lambda g, c: (g, 0, 0)
    per_g4 = lambda g, c: (g, 0, 0, 0)
    const2 = lambda g, c: (0, 0)
    const3 = lambda g, c: (0, 0, 0)
    operands = [seq3(cin), seq3(v), seq3(o), seq3(small), prev, c0, n0, m0, cw, cb, wq, wkt, gb, gml, ltri]
    stacked = c_layers > 1
    aliased = stacked and c_stack is not None
    extra_specs, aliases = [], {}
    if aliased:
        extra_specs = [pl.BlockSpec(memory_space=pl.ANY)]
        aliases = {len(operands): 1}
        operands.append(c_stack)
    c_out_map = (lambda g, c: (c_off // nb + g, 0, 0, 0)) if stacked else per_g4
    hm, cout, nout, mout = pl.pallas_call(
        functools.partial(_mlstm_kernel, nb=nb, aliased=aliased, L=L, rows=rows, valid=valid, H=H, Dh=Dh),
        grid=(B // nb, nc),
        input_output_aliases=aliases,
        in_specs=[pl.BlockSpec((nb, rows, D), tok), pl.BlockSpec((nb, rows, D), tok), pl.BlockSpec((nb, rows, D), tok),
                  pl.BlockSpec((nb, rows, LANES), tok),
                  pl.BlockSpec((nb, SUBLANES, D), per_g3),
                  pl.BlockSpec((nb, H, Dh, Dh), lambda g, c: (c_off // nb + g, 0, 0, 0)),
                  pl.BlockSpec((nb, 1, D), per_g3),
                  pl.BlockSpec((nb, 1, LANES), per_g3),
                  pl.BlockSpec((4, D), const2), pl.BlockSpec((1, D), const2),
                  pl.BlockSpec((H, Dh, Dh), const3), pl.BlockSpec((H, Dh, Dh), const3),
                  pl.BlockSpec((1, LANES), const2), pl.BlockSpec((1, D), const2),
                  pl.BlockSpec((L, L), const2)] + extra_specs,
        out_specs=[pl.BlockSpec((nb, rows, D), tok),
                   pl.BlockSpec((nb, H, Dh, Dh), c_out_map),
                   pl.BlockSpec((nb, 1, D), per_g3),
                   pl.BlockSpec((nb, 1, LANES), per_g3)],
        out_shape=[jax.ShapeDtypeStruct((B, nc * rows, D), F32),
                   jax.ShapeDtypeStruct((c_layers * B, H, Dh, Dh), F32),
                   jax.ShapeDtypeStruct((B, 1, D), F32),
                   jax.ShapeDtypeStruct((B, 1, LANES), F32)],
        scratch_shapes=[pltpu.VMEM((nb, SUBLANES, D), F32),
                        pltpu.VMEM((nb, H, Dh, Dh + LANES), F32),
                        pltpu.VMEM((nb, 1, LANES), F32),
                        pltpu.VMEM((nb, L, D), F32), pltpu.VMEM((nb, L, D), F32), pltpu.VMEM((nb, L, LANES), F32)],
        compiler_params=_params(("arbitrary", "arbitrary")),
        name="mlstm",
    )(*operands)
    return hm.reshape(cin.shape), cout, nout, mout


def _deinterleave_t(perm, groups):
    per = LANES // CMP_STRIDE
    ys = [_dot_nt(perm, x_t.astype(BF16)) for x_t in groups]
    return [jnp.concatenate([y[l * per:(l + 1) * per] for y in ys], axis=0) for l in range(CMP_STRIDE)]


def _pe_bias(slot, pe_ref, w1_ref, b1_ref):
    pe = jnp.broadcast_to(pe_ref[slot:slot + 1, :], (SUBLANES, pe_ref.shape[1])).astype(BF16)
    return _dot(pe, w1_ref[slot]) + b1_ref[slot:slot + 1, :]


def _compress_slots(perm, groups, biases, w1_ref, w2_ref, b2_ref):
    dk = NSA_HEAD_DIM
    half = CMP_BLOCK // 2
    pack = GROUP_LANES // dk
    G = NSA_KV_HEADS
    slots = range(len(groups))
    xs = [_deinterleave_t(perm, groups[s]) for s in slots]
    nrow = xs[0][0].shape[0]
    first = [None for _ in slots]
    second = [None for _ in slots]
    for l0 in range(0, half, pack):
        for s in slots:
            xg = jnp.concatenate(
                [jnp.concatenate([xs[s][l0 + j][:, g * dk:(g + 1) * dk] for j in range(pack)], axis=1)
                 for g in range(G)], axis=0).astype(BF16)
            a = _dot(xg, w1_ref[s, l0 * dk:(l0 + pack) * dk, :])
            bb = _dot(xg, w1_ref[s, (half + l0) * dk:(half + l0 + pack) * dk, :])
            first[s] = a if first[s] is None else first[s] + a
            second[s] = bb if second[s] is None else second[s] + bb
    hid = [jax.nn.gelu(first[s] + pltpu.roll(second[s], G * nrow - 1, axis=0) + biases[s]).astype(BF16)
           for s in slots]
    outs = []
    for s in slots:
        out = None
        for g in range(G):
            og = _dot(hid[s][g * nrow:(g + 1) * nrow], w2_ref[s, g])
            out = og if out is None else out + og
        outs.append(out + b2_ref[s:s + 1, :])
    return outs


def _expand_q(q, tq):
    slab = NSA_KV_HEADS * tq
    lane_g = lax.broadcasted_iota(jnp.int32, (slab, GROUP_LANES), 1) // NSA_HEAD_DIM
    row_g = lax.broadcasted_iota(jnp.int32, (slab, GROUP_LANES), 0) // tq
    own = lane_g == row_g
    n_rep = q.shape[1] // GROUP_LANES
    slabs = []
    for r in range(n_rep):
        qr = q[:, r * GROUP_LANES:(r + 1) * GROUP_LANES] * QK_SCALE
        slabs.append(jnp.where(own, jnp.concatenate([qr] * NSA_KV_HEADS, axis=0), 0.0).astype(BF16))
    return slabs, own


def _attend(carry, q, chunks):
    m, l, acc = carry
    scores = []
    for k, _, bias in chunks:
        s = _dot(q, k)
        if bias is not None:
            s = _add_slab_bias(s, bias) if jnp.ndim(bias) == 2 else s + bias
        scores.append(s)
    mx = scores[0]
    for s in scores[1:]:
        mx = jnp.maximum(mx, s)
    m_new = jnp.maximum(m, jnp.max(mx, axis=1, keepdims=True))
    alpha = jnp.exp2(m - m_new)
    ps, pv = None, None
    for s, (_, v, _) in zip(scores, chunks):
        p = jnp.exp2(s - m_new)
        ps = p if ps is None else ps + p
        d = _dot_nt(p.astype(BF16), v)
        pv = d if pv is None else pv + d
    return m_new, alpha * l + jnp.sum(ps, axis=1, keepdims=True), alpha * acc + pv


def _add_slab_bias(s, bias):
    rows, ck = s.shape
    return (s.reshape(rows // bias.shape[0], bias.shape[0], ck) + bias[None]).reshape(rows, ck)


def _softmax_values(scores, values):
    mx = scores[0]
    for s in scores[1:]:
        mx = jnp.maximum(mx, s) if s.shape == mx.shape else mx
    m = mx.max(axis=1, keepdims=True)
    for s in scores[1:]:
        if s.shape != mx.shape:
            m = jnp.maximum(m, s.max(axis=1, keepdims=True))
    l, acc, ps = None, None, None
    for s, (v, v_t) in zip(scores, values):
        p = jnp.exp2(s - m)
        if p.shape == mx.shape:
            ps = p if ps is None else ps + p
        else:
            ls = p.sum(axis=1, keepdims=True)
            l = ls if l is None else l + ls
        pv = _dot_nt(p.astype(BF16), v) if v_t else _dot(p.astype(BF16), v)
        acc = pv if acc is None else acc + pv
    ls = ps.sum(axis=1, keepdims=True)
    return m, (ls if l is None else l + ls), acc


def _attend_init(rows):
    return (jnp.full((rows, 1), NEG, F32), jnp.zeros((rows, 1), F32), jnp.zeros((rows, GROUP_LANES), F32))


def _cmp_rows(q, mask, kcmp, vcmp):
    rows, slab = q.shape[0], mask.shape[0]
    s = _dot_nt(q, kcmp).reshape(rows // slab, slab, mask.shape[1])
    s = jnp.where(mask[None], s, NEG)
    e = jnp.where(mask[None], jnp.exp2(s - jnp.max(s, axis=2, keepdims=True)), 0.0)
    p = e * (1.0 / jnp.maximum(jnp.sum(e, axis=2, keepdims=True), 1e-30))
    p = p.reshape(rows, mask.shape[1]).astype(BF16)
    return _dot(p, vcmp), p


def _cmp_mask(rows, n_rows_cmp, qpos_col, n_cmp):
    lane = lax.broadcasted_iota(jnp.int32, (rows, n_rows_cmp), 1)
    return (lane * CMP_STRIDE + (CMP_BLOCK - 1) <= qpos_col) & (lane < n_cmp)


def _block_scores(imp, blk, cur, n_sel_blocks):
    forced = (blk == 0) | (blk == cur) | (blk == cur - 1)
    score = jnp.where(blk <= cur, jnp.where(forced, FORCE, imp), -1.0)
    return jnp.where(blk < n_sel_blocks, score, -2.0)


def _select_blocks(imp, qpos_col, n_sel_blocks):
    rows = imp.shape[0]
    blk = lax.broadcasted_iota(jnp.int32, (rows, LANES), 1)
    score = _block_scores(imp, blk, qpos_col // SEL_BLOCK, n_sel_blocks)
    rank = jnp.zeros((rows, LANES), F32)
    for j in range(n_sel_blocks):
        col = score[:, j:j + 1]
        ge = jnp.where(col >= score, 1.0, 0.0)
        gt = jnp.where(col > score, 1.0, 0.0)
        rank = rank + jnp.where(blk > j, ge, gt)
    chosen = jnp.where(rank < float(min(N_SEL, n_sel_blocks)), score, -1.0) >= 0.0
    return jnp.where(chosen, 0.0, -MASK_BIG).astype(BF16)


def _select_blocks_t(imp_t, qpos_row, n_sel_blocks):
    nb = -(-n_sel_blocks // SUBLANES) * SUBLANES
    cols = imp_t.shape[1]
    blk = lax.broadcasted_iota(jnp.int32, (nb, cols), 0)
    score = _block_scores(imp_t[:nb], blk, qpos_row // SEL_BLOCK, n_sel_blocks)
    rank = jnp.zeros((nb, cols), F32)
    for j in range(n_sel_blocks):
        row = score[j:j + 1, :]
        ge = jnp.where(row >= score, 1.0, 0.0)
        gt = jnp.where(row > score, 1.0, 0.0)
        rank = rank + jnp.where(blk > j, ge, gt)
    chosen = jnp.where(rank < float(min(N_SEL, n_sel_blocks)), score, -1.0) >= 0.0
    neg_t = jnp.where(chosen, 0.0, -MASK_BIG)
    neg_t = jnp.concatenate([neg_t, jnp.zeros((LANES - nb, cols), F32)], axis=0)
    return neg_t.T.astype(BF16)


def _gate_columns(sg, r, n_rep, tq):
    lane = lambda g, c: 2 * ML_HEADS + (g * n_rep + r) * 3 + c
    return [jnp.concatenate([sg[:, lane(g, c):lane(g, c) + 1] for g in range(NSA_KV_HEADS)], axis=0)
            for c in range(3)]


def _gated_fold(terms, own, tq):
    comb = None
    for gate, branch in terms:
        if isinstance(branch, (tuple, list)):
            _, l, acc = branch
            term = (gate * (1.0 / jnp.maximum(l, 1e-30))) * acc
        else:
            term = gate * branch
        comb = term if comb is None else comb + term
    kept = jnp.where(own, comb, 0.0)
    piece = kept[:tq]
    for g in range(1, NSA_KV_HEADS):
        piece = piece + kept[g * tq:(g + 1) * tq]
    return piece


def _combine_slab(o_cmp, sel, win, gates, own, tq):
    return _gated_fold([(gates[0], o_cmp), (gates[1], sel), (gates[2], win)], own, tq)


def _compress_prompt_kernel(kc_ref, vc_ref, pe_ref, w1_ref, b1_ref, w2_ref, b2_ref, perm_ref, ko_ref, vo_ref):
    T = kc_ref.shape[2]
    groups = [[src[0, :, t * LANES:(t + 1) * LANES] for t in range(T // LANES)] for src in (kc_ref, vc_ref)]
    biases = [_pe_bias(slot, pe_ref, w1_ref, b1_ref)[0:1] for slot in range(2)]
    ko_ref[0], vo_ref[0] = _compress_slots(perm_ref[...], groups, biases, w1_ref, w2_ref, b2_ref)


def _compress_prompt(kvw_t, cw, *, B, T):
    nrow = T // CMP_STRIDE
    assert (T - CMP_BLOCK) // CMP_STRIDE + 1 <= nrow and T % LANES == 0
    full = lambda a: pl.BlockSpec(a.shape, lambda b: (0,) * a.ndim)
    return pl.pallas_call(
        _compress_prompt_kernel,
        grid=(B,),
        in_specs=[pl.BlockSpec((1, GROUP_LANES, T), lambda b: (b, 0, 0)),
                  pl.BlockSpec((1, GROUP_LANES, T), lambda b: (b, 1, 0))] + [full(a) for a in cw],
        out_specs=[pl.BlockSpec((1, nrow, GROUP_LANES), lambda b: (b, 0, 0))] * 2,
        out_shape=[jax.ShapeDtypeStruct((B, nrow, GROUP_LANES), F32)] * 2,
        compiler_params=_params(("parallel",)),
        name="compress_prompt",
    )(kvw_t, kvw_t, *cw)


def _nsa_prompt_kernel(q_ref, sm_ref, kcmp_ref, vcmp_ref, ks_ref, vs_ref, kw_ref, vw_ref,
                       mmap_t_ref, exp_ref, out_ref,
                       kx_ref, vsb_ref, kwb_ref, vwb_ref, *, tq, ck, n_cmp, n_sel_blocks):
    qt = pl.program_id(1)
    n_chunks = kx_ref.shape[0]
    n_tiles = q_ref.shape[0] // tq

    @pl.when(qt == 0)
    def _():
        for c in range(n_chunks):
            cs = slice(c * ck, (c + 1) * ck)
            kx_ref[c, :GROUP_LANES, :] = ks_ref[0, :, cs].astype(BF16)
            kx_ref[c, GROUP_LANES:, :] = exp_ref[c]
            vsb_ref[c] = vs_ref[0, :, cs].astype(BF16)
            kwb_ref[c] = kw_ref[0, :, cs].astype(BF16)
            vwb_ref[c] = vw_ref[0, :, cs].astype(BF16)

    running = [_nsa_prompt_tile(j, (qt * n_tiles + j) * tq, q_ref, sm_ref, kcmp_ref, vcmp_ref, mmap_t_ref, out_ref,
                                kx_ref, vsb_ref, kwb_ref, vwb_ref, tq=tq, ck=ck, n_cmp=n_cmp,
                                n_sel_blocks=n_sel_blocks) for j in range(n_tiles)]
    while running:
        for tile in list(running):
            if next(tile, True):
                running.remove(tile)


def _nsa_prompt_tile(j, t0, q_ref, sm_ref, kcmp_ref, vcmp_ref, mmap_t_ref, out_ref,
                     kx_ref, vsb_ref, kwb_ref, vwb_ref, *, tq, ck, n_cmp, n_sel_blocks):
    n_rep = q_ref.shape[1] // GROUP_LANES
    slab = NSA_KV_HEADS * tq
    rows_j = slice(j * tq, (j + 1) * tq)
    qpos = t0 + lax.broadcasted_iota(jnp.int32, (slab, 1), 0) % tq
    qpos_row = t0 + lax.broadcasted_iota(jnp.int32, (1, slab), 1) % tq
    qs, own = _expand_q(q_ref[rows_j, :], tq)
    sg = jax.nn.sigmoid(sm_ref[rows_j, :])
    reps = range(n_rep)

    q_all = jnp.concatenate(qs, axis=0)
    kpos0 = lax.broadcasted_iota(jnp.int32, (1, ck), 1)
    c_diag = t0 // ck
    causal = jnp.where(kpos0 + c_diag * ck <= qpos, 0.0, NEG)

    kcmp, vcmp = kcmp_ref[0].astype(BF16), vcmp_ref[0].astype(BF16)
    o_cmp, p_cmp = _cmp_rows(q_all, _cmp_mask(slab, kcmp.shape[0], qpos, n_cmp), kcmp, vcmp)
    imp_rows_t = _dot_nt(mmap_t_ref[...], p_cmp)
    yield

    n_win = -(-(WINDOW - 1) // ck) + 1
    idx = [jnp.maximum(c_diag - (n_win - 1 - w), 0) for w in range(n_win)]
    gone = [jnp.where(c_diag >= n_win - 1 - w, 0.0, NEG) for w in range(n_win)]
    scores = []
    for w in range(n_win):
        s = _dot(q_all, kwb_ref[idx[w]])
        if w == n_win - 1:
            s = _add_slab_bias(s, causal)
        elif w == 0:
            s = _add_slab_bias(s, jnp.where(kpos0 + idx[0] * ck > qpos - WINDOW, 0.0, NEG) + gone[0])
        else:
            s = s + gone[w]
        scores.append(s)
    win = _softmax_values(scores, [(vwb_ref[idx[w]], True) for w in range(n_win)])
    yield

    imp_t = imp_rows_t[:, :slab]
    for r in range(1, n_rep):
        imp_t = imp_t + imp_rows_t[:, r * slab:(r + 1) * slab]
    sel_neg = _select_blocks_t(imp_t, qpos_row, n_sel_blocks)
    qx = jnp.concatenate([q_all, jnp.concatenate([sel_neg] * n_rep, axis=0)], axis=1)
    yield

    def sel_body(i, carry):
        return _attend(carry, qx, [(kx_ref[2 * i], vsb_ref[2 * i], None), (kx_ref[2 * i + 1], vsb_ref[2 * i + 1], None)])

    sel = lax.fori_loop(0, c_diag // 2, sel_body, _attend_init(n_rep * slab))
    odd = jnp.maximum(c_diag - 1, 0)
    no_odd = jnp.where(c_diag % 2 == 1, 0.0, NEG)
    sel = _attend(sel, qx, [(kx_ref[odd], vsb_ref[odd], no_odd), (kx_ref[c_diag], vsb_ref[c_diag], causal)])
    yield

    for r in reps:
        rs = slice(r * slab, (r + 1) * slab)
        piece = _combine_slab(o_cmp[rs], [a[rs] for a in sel], [a[rs] for a in win],
                              _gate_columns(sg, r, n_rep, tq), own, tq)
        out_ref[rows_j, r * GROUP_LANES:(r + 1) * GROUP_LANES] = piece.astype(out_ref.dtype)


def _cmp_to_sel(n_cmp, n_sel_blocks):
    r = SEL_BLOCK // CMP_STRIDE
    c = CMP_BLOCK // CMP_STRIDE
    m = np.zeros((LANES, LANES), np.float32)
    for j in range(n_sel_blocks):
        for a in range(r):
            for b in range(c):
                i = r * j + a - b
                if 0 <= i < n_cmp:
                    m[i, j] += 1.0
    return jnp.asarray(m, BF16)


def _block_expand(n_chunks, ck, n_sel_blocks):
    e = np.zeros((n_chunks, LANES, ck), np.float32)
    key = np.arange(n_chunks * ck).reshape(n_chunks, ck)
    for c in range(n_chunks):
        e[c, key[c] // SEL_BLOCK, np.arange(ck)] = 1.0
    e[:, n_sel_blocks:, :] = 0.0
    return jnp.asarray(e, BF16)


def _nsa_prompt(q, small, kcmp, vcmp, kv_t, win_t, *, B, T, tq, ck):
    M, QW = q.shape
    nqt = T // tq
    tiles = NSA_PROMPT_TILES_PER_STEP
    n_cmp = (T - CMP_BLOCK) // CMP_STRIDE + 1
    n_sel_blocks = -(-T // SEL_BLOCK)
    assert T % ck == 0 and ck % tq == 0 and n_sel_blocks <= LANES and kcmp.shape[1] <= LANES and nqt % tiles == 0
    assert WINDOW % ck == 0
    mmap_t = _cmp_to_sel(n_cmp, n_sel_blocks).T[:, :kcmp.shape[1]]
    expand = _block_expand(T // ck, ck, n_sel_blocks)
    const = lambda a: pl.BlockSpec(a.shape, lambda b, t: (0,) * a.ndim)
    kv_spec = lambda slot: pl.BlockSpec((1, GROUP_LANES, T), lambda b, t: (b, slot, 0))
    chunks = lambda kdim: pltpu.VMEM((T // ck, kdim, ck), BF16)
    return pl.pallas_call(
        functools.partial(_nsa_prompt_kernel, tq=tq, ck=ck, n_cmp=n_cmp, n_sel_blocks=n_sel_blocks),
        grid=(B, nqt // tiles),
        in_specs=[pl.BlockSpec((tiles * tq, QW), lambda b, t: (b * (nqt // tiles) + t, 0)),
                  pl.BlockSpec((tiles * tq, LANES), lambda b, t: (b * (nqt // tiles) + t, 0)),
                  pl.BlockSpec((1,) + kcmp.shape[1:], lambda b, t: (b, 0, 0)),
                  pl.BlockSpec((1,) + vcmp.shape[1:], lambda b, t: (b, 0, 0)),
                  kv_spec(2), kv_spec(3), kv_spec(0), kv_spec(1),
                  const(mmap_t), const(expand)],
        out_specs=pl.BlockSpec((tiles * tq, QW), lambda b, t: (b * (nqt // tiles) + t, 0)),
        out_shape=jax.ShapeDtypeStruct((M, QW), BF16),
        scratch_shapes=[chunks(GROUP_LANES + LANES), chunks(GROUP_LANES), chunks(GROUP_LANES), chunks(GROUP_LANES)],
        compiler_params=_params(("arbitrary", "arbitrary")),
        name="nsa_prompt",
    )(q, small, kcmp, vcmp, kv_t, kv_t, win_t, win_t, mmap_t, expand)


def _nsa_decode_kernel(pt_ref, *refs, nb, n_pages, **static):
    del pt_ref
    pages = refs[:nb * n_pages]
    (q_ref, sm_ref, kvn_ref, wn_ref, cw_ref, pe_ref, w1_ref, b1_ref, w2_ref, b2_ref, perm_ref,
     mmap_ref, exp_ref, *outs) = refs[nb * n_pages:]
    out_ref, nw_ref, newk_ref, neww_ref, bias_ref = outs[-5:]

    @pl.when(pl.program_id(0) == 0)
    def _():
        newk_ref[...] = jnp.zeros_like(newk_ref)
        neww_ref[...] = jnp.zeros_like(neww_ref)
        for slot in range(2):
            bias_ref[slot] = _pe_bias(slot, pe_ref, w1_ref, b1_ref)

    shared = (w1_ref, w2_ref, b2_ref, perm_ref, mmap_ref, exp_ref, bias_ref)
    running = [_nsa_decode_seq(pages[j * n_pages:(j + 1) * n_pages],
                               *[r.at[j] for r in (q_ref, sm_ref, kvn_ref, wn_ref, cw_ref)], *shared,
                               *[r.at[j] for r in (out_ref, nw_ref, newk_ref, neww_ref)], **static)
               for j in range(nb)]
    while running:
        for seq in list(running):
            if next(seq, True):
                running.remove(seq)


def _nsa_decode_seq(pages, q_ref, sm_ref, kvn_ref, wn_ref, cw_ref, w1_ref, w2_ref, b2_ref, perm_ref,
                    mmap_ref, exp_ref, bias_ref, out_ref, nw_ref, newk_ref, neww_ref,
                    *, page, tq, t_new, past_len, n_cmp, n_sel_blocks, wb):
    n_pages = len(pages)
    n_rep = q_ref.shape[1] // GROUP_LANES
    slab = NSA_KV_HEADS * tq
    reps = range(n_rep)

    newk_ref[:tq, :] = kvn_ref[...]
    neww_ref[:tq, :] = wn_ref[...]

    qpos = past_len + lax.broadcasted_iota(jnp.int32, (slab, 1), 0) % tq
    qs, own = _expand_q(q_ref[...], tq)
    sg = jax.nn.sigmoid(sm_ref[...])

    cmp_kv = _compress_slots(perm_ref[...], [[pg[0, slot] for pg in pages] for slot in range(2)],
                             [bias_ref[slot, 0:1, :] for slot in range(2)], w1_ref, w2_ref, b2_ref)
    cmp_kv = [a.astype(BF16) for a in cmp_kv]
    yield

    q_all = jnp.concatenate(qs, axis=0)
    o_cmp, p_cmp = _cmp_rows(q_all, _cmp_mask(slab, cmp_kv[0].shape[0], qpos, n_cmp), cmp_kv[0], cmp_kv[1])
    imp_rows = _dot(p_cmp, mmap_ref[...])
    imp = imp_rows[:slab]
    for r in range(1, n_rep):
        imp = imp + imp_rows[r * slab:(r + 1) * slab]
    neg_all = jnp.concatenate([_select_blocks(imp, qpos, n_sel_blocks)] * n_rep, axis=0)
    yield

    kpos0 = lax.broadcasted_iota(jnp.int32, (1, page), 1)
    own_rows = jnp.where(kpos0 + past_len <= jnp.concatenate([qpos] * n_rep, axis=0), 0.0, NEG)

    scores = [_dot(q_all, pages[p][0, 2].astype(BF16)) + _dot(neg_all, exp_ref[p]) for p in range(n_pages)]
    values = [(pages[p][0, 3].astype(BF16), True) for p in range(n_pages)]
    k_new = newk_ref[:, 2 * GROUP_LANES:3 * GROUP_LANES].astype(BF16)
    scores.append(_dot_nt(q_all, k_new) + _dot(neg_all, exp_ref[n_pages]) + own_rows)
    values.append((newk_ref[:, 3 * GROUP_LANES:].astype(BF16), False))
    sel = _softmax_values(scores, values)
    yield

    kpos = past_len - wb + lax.broadcasted_iota(jnp.int32, (1, wb), 1)
    in_band = jnp.where(kpos > jnp.concatenate([qpos] * n_rep, axis=0) - WINDOW, 0.0, NEG)
    scores = [_dot(q_all, cw_ref[0].astype(BF16)) + in_band,
              _dot_nt(q_all, neww_ref[:, :GROUP_LANES].astype(BF16)) + own_rows]
    values = [(cw_ref[1].astype(BF16), True), (neww_ref[:, GROUP_LANES:].astype(BF16), False)]
    win = _softmax_values(scores, values)
    yield

    for r in reps:
        rs = slice(r * slab, (r + 1) * slab)
        piece = _combine_slab(o_cmp[rs], [a[rs] for a in sel], [a[rs] for a in win],
                              _gate_columns(sg, r, n_rep, tq), own, tq)
        out_ref[:, r * GROUP_LANES:(r + 1) * GROUP_LANES] = piece

    new_t = neww_ref[...].T
    lane = lax.broadcasted_iota(jnp.int32, (GROUP_LANES, LANES), 1)
    for slot in range(2):
        shifted = pltpu.roll(cw_ref[slot], wb - t_new, axis=1)
        fresh = pltpu.roll(new_t[slot * GROUP_LANES:(slot + 1) * GROUP_LANES], LANES - t_new, axis=1)
        nw_ref[slot, :, :wb - LANES] = shifted[:, :wb - LANES]
        nw_ref[slot, :, wb - LANES:] = jnp.where(lane >= LANES - t_new, fresh, shifted[:, wb - LANES:])


def _nsa_decode(q8, small8, kvnew8, winnew8, cache, cache_win, page_table, cw, *, past_len, t_new, win_off,
                win_stack=None):
    B, tq, QW = q8.shape
    n_pages = page_table.shape[1]
    page = cache.shape[3]
    wb = cache_win.shape[3]
    tk = past_len + t_new
    n_cmp = (tk - CMP_BLOCK) // CMP_STRIDE + 1
    n_sel_blocks = -(-tk // SEL_BLOCK)
    nrow = past_len // CMP_STRIDE
    assert tq == SUBLANES and past_len == n_pages * page and wb % page == 0 and page == LANES
    assert n_cmp <= nrow <= LANES and (n_cmp - 1) * CMP_STRIDE + CMP_BLOCK <= past_len
    assert n_sel_blocks <= LANES and 0 < t_new < tq and wb == WINDOW
    mmap = _cmp_to_sel(n_cmp, n_sel_blocks)[:nrow]
    expand = _block_expand(n_pages + 1, page, n_sel_blocks)
    nb = min(B, NSA_DECODE_SEQS_PER_STEP)
    assert B % nb == 0 and win_off % nb == 0
    const = lambda a: pl.BlockSpec(a.shape, lambda b, pt: (0,) * a.ndim)
    page_spec = lambda j, p: pl.BlockSpec((1,) + cache.shape[1:], lambda b, pt: (pt[b * nb + j, p], 0, 0, 0))
    per_b = lambda a: pl.BlockSpec((nb,) + a.shape[1:], lambda b, pt: (b, 0, 0))
    win_block = (nb,) + cache_win.shape[1:]
    win_map = lambda b, pt: (win_off // nb + b, 0, 0, 0)
    consts = list(cw) + [mmap, expand]
    operands = [page_table] + [cache] * (nb * n_pages) + [q8, small8, kvnew8, winnew8, cache_win] + consts
    extra_specs, aliases = [], {}
    if win_stack is not None:
        extra_specs = [pl.BlockSpec(memory_space=pl.ANY)]
        aliases = {len(operands): 1}
        operands.append(win_stack)
    grid_spec = pltpu.PrefetchScalarGridSpec(
        num_scalar_prefetch=1,
        grid=(B // nb,),
        in_specs=[page_spec(j, p) for j in range(nb) for p in range(n_pages)]
        + [per_b(q8), per_b(small8), per_b(kvnew8), per_b(winnew8), pl.BlockSpec(win_block, win_map)]
        + [const(a) for a in consts] + extra_specs,
        out_specs=[per_b(q8), pl.BlockSpec(win_block, win_map)],
        scratch_shapes=[pltpu.VMEM((nb, page, kvnew8.shape[2]), F32), pltpu.VMEM((nb, page, winnew8.shape[2]), F32),
                        pltpu.VMEM((2, SUBLANES, cw[2].shape[1]), F32)],
    )
    return pl.pallas_call(
        functools.partial(_nsa_decode_kernel, nb=nb, n_pages=n_pages, page=page, tq=tq, t_new=t_new,
                          past_len=past_len, n_cmp=n_cmp, n_sel_blocks=n_sel_blocks, wb=wb),
        grid_spec=grid_spec,
        input_output_aliases=aliases,
        out_shape=[jax.ShapeDtypeStruct(q8.shape, F32), jax.ShapeDtypeStruct(cache_win.shape, F32)],
        compiler_params=_params(("arbitrary",)),
        name="nsa_decode",
    )(*operands)


def _layer_weights(i, prm, d_ml, d_nsa):
    H, G, dk = ML_HEADS, NSA_KV_HEADS, NSA_HEAD_DIM
    R = d_nsa // (G * dk)
    kvw = G * dk
    w_in = prm["w_in"][i]
    o = 0
    c_in, o = w_in[:, o:o + d_ml], o + d_ml
    v_ml, o = w_in[:, o:o + d_ml], o + d_ml
    o_ml, o = w_in[:, o:o + d_ml], o + d_ml
    i_ml, o = w_in[:, o:o + H], o + H
    f_ml, o = w_in[:, o:o + H], o + H
    q_n, o = w_in[:, o:o + d_nsa], o + d_nsa
    kv, o = w_in[:, o:o + 6 * kvw], o + 6 * kvw
    g_n = w_in[:, o:]
    K = w_in.shape[0]
    q_perm = q_n.reshape(K, G, R, dk).transpose(0, 2, 1, 3).reshape(K, d_nsa)
    w_main = jnp.concatenate([c_in, v_ml, o_ml, q_perm, kv], axis=1).astype(BF16)
    w_kvw_t = kv.T.astype(BF16)
    small = jnp.concatenate([i_ml, f_ml, g_n], axis=1)
    w_small = jnp.pad(small, ((0, 0), (0, LANES - small.shape[1]))).astype(BF16)
    w_out = prm["w_out"][i]
    w_out_nsa = w_out[d_ml:].reshape(G, R, dk, -1).transpose(1, 0, 2, 3).reshape(d_nsa, -1)
    w_out_p = jnp.concatenate([w_out[:d_ml], w_out_nsa], axis=0).astype(BF16)
    Dh = d_ml // H
    gate_bias = jnp.pad(jnp.concatenate([prm["b_i"][i], prm["b_f"][i]]), (0, LANES - 2 * H)).reshape(1, LANES)
    w2 = prm["cmp_w2"][i]
    w2_placed = jnp.stack([jnp.stack([jnp.pad(w2[s], ((0, 0), (g * dk, (G - 1 - g) * dk))) for g in range(G)])
                           for s in range(2)]).astype(BF16)
    perm = np.zeros((LANES, LANES), np.float32)
    tok = np.arange(LANES)
    perm[(tok % CMP_STRIDE) * (LANES // CMP_STRIDE) + tok // CMP_STRIDE, tok] = 1.0
    cmp_w = (prm["cmp_pe"][i].reshape(2, -1), prm["cmp_w1"][i].astype(BF16), prm["cmp_b1"][i], w2_placed,
             jnp.tile(prm["cmp_b2"][i], (1, G)), jnp.asarray(perm, BF16))
    return dict(
        w_main=w_main, w_kvw_t=w_kvw_t, w_small=w_small, w_out=w_out_p,
        w_up=prm["w_up_all"], w_down=prm["w_down_all"],
        w_pl=prm["w_pl"][i].astype(BF16), w_pl_gate=prm["w_pl_gate"][i].astype(BF16),
        conv_w=prm["conv_w"][i], conv_b=prm["conv_b"][i].reshape(1, -1),
        wq=prm["w_q_ml"][i].astype(BF16),
        wkt=(jnp.swapaxes(prm["w_k_ml"][i], 1, 2) * (Dh ** -0.5)).astype(BF16),
        gate_bias=gate_bias, g_ml=prm["g_ml"][i].reshape(1, -1), cmp=cmp_w)


def _pad_time(a, B, t, tp):
    return jnp.pad(a.reshape(B, t, -1), ((0, 0), (0, tp - t), (0, 0)))


def _layer(h, pl_e, i, prm, lw, mem, *, B, T):
    M, D = h.shape
    G, dk, H = NSA_KV_HEADS, NSA_HEAD_DIM, ML_HEADS
    d_ml = lw["conv_w"].shape[1]
    d_nsa = lw["w_out"].shape[0] - d_ml
    R = d_nsa // (G * dk)
    Dh = d_ml // H
    kvw = G * dk
    tm = min(M, 512)
    g_pre = prm["g_pre_mix"][i]
    lanes_last = lambda a, lead: a.reshape(lead + (G, dk, a.shape[-1]))

    ml_w = (lw["conv_w"], lw["conv_b"], lw["wq"], lw["wkt"], lw["gate_bias"], lw["g_ml"])
    if mem is None:
        cin, v, o, q, small = _norm_matmul(h, g_pre, lw["w_main"], (d_ml, d_ml, d_ml, d_nsa),
                                           w_small=lw["w_small"], tm=min(M, 1024), tn=512)
        kv_t, win_t = _norm_matmul_t(h, g_pre, lw["w_kvw_t"], (4 * kvw, 2 * kvw), B=B, T=T, tm=min(T, 1024), tn=512)
        zeros = lambda *s: jnp.zeros(s, F32)
        hm, c_new, n_new, m_new = _mlstm(
            cin, v, o, small, zeros(B, SUBLANES, d_ml), zeros(B, H, Dh, Dh), zeros(B, 1, d_ml), zeros(B, 1, LANES),
            *ml_w, B=B, rows=ML_CHUNK, valid=ML_CHUNK)
        kcmp, vcmp = _compress_prompt(kv_t, lw["cmp"], B=B, T=T)
        on = _nsa_prompt(q, small, kcmp, vcmp, kv_t, win_t, B=B, T=T, tq=64, ck=256)
        wlen = min(WINDOW, T)
        new_rows = lanes_last(kv_t, (B, 4)).transpose(0, 4, 1, 2, 3)
        new_win = lanes_last(win_t[:, :, T - wlen:], (B, 2)).transpose(0, 4, 1, 2, 3)
    else:
        cin, v, o, q, kv4, win2, small = _norm_matmul(
            h, g_pre, lw["w_main"], (d_ml, d_ml, d_ml, d_nsa, 4 * kvw, 2 * kvw),
            w_small=lw["w_small"], tm=tm, tn=512)
        tp = SUBLANES
        pad = lambda a: _pad_time(a, B, T, tp)
        prev = jnp.pad(mem["conv"], ((0, 0), (SUBLANES - mem["conv"].shape[1], 0), (0, 0)))
        m0 = jnp.pad(mem["m"], ((0, 0), (0, LANES - H))).reshape(B, 1, LANES)
        hm8, c_new, n_new, m_new = _mlstm(
            pad(cin).reshape(B * tp, d_ml), pad(v).reshape(B * tp, d_ml), pad(o).reshape(B * tp, d_ml),
            pad(small).reshape(B * tp, LANES), prev, mem["C"], mem["n"].reshape(B, 1, d_ml), m0,
            *ml_w, B=B, rows=tp, valid=T, c_off=i * B, c_layers=mem["C"].shape[0] // B, c_stack=mem["c_stack"])
        hm = hm8.reshape(B, tp, d_ml)[:, :T].reshape(M, d_ml)
        page = mem["kv"].shape[3]
        on8, new_win = _nsa_decode(
            pad(q), pad(small), pad(kv4), pad(win2),
            mem["kv"], mem["win"], mem["page_table"], lw["cmp"],
            past_len=mem["page_table"].shape[1] * page, t_new=T, win_off=i * B, win_stack=mem["win_stack"])
        on = on8[:, :T].reshape(M, d_nsa)
        new_rows = kv4.reshape(B, T, 4, G, dk)

    h = _matmul_norm_res([hm, on], lw["w_out"], h, prm["g_post_mix"][i], tm=tm, tk=d_ml + d_nsa)
    u, = _norm_matmul(h, prm["g_pre_mlp"][i], lw["w_up"], (lw["w_up"].shape[1],), tm=min(M, 1024), tn=1024,
                      act="relu2", out_dtype=BF16, layer=i)
    h = _matmul_norm_res([u], lw["w_down"], h, prm["g_post_mlp"][i], tm=tm, tk=2048, layer=i)
    h = _ple(h, pl_e, lw["w_pl_gate"], lw["w_pl"], prm["g_pl"][i], tm=min(M, 256), layer=i)

    state = (new_rows, new_win, c_new, n_new.reshape(B, H, Dh), m_new[:, 0, :H], cin.reshape(B, T, d_ml)[:, T - 3:])
    return h, state


def kernel(x_prompt, x_sample, cache_kv, cache_win, state_C, state_n, state_m, state_conv, page_table,
           p_prompt, p_sample, g_pre_mix, w_in, conv_w, conv_b, w_q_ml, w_k_ml, b_i, b_f, g_ml,
           cmp_pe, cmp_w1, cmp_b1, cmp_w2, cmp_b2, w_out, g_post_mix, g_pre_mlp, w_up, w_down,
           g_post_mlp, w_pl, g_pl, w_pl_gate):
    prm = dict(g_pre_mix=g_pre_mix, w_in=w_in, conv_w=conv_w, conv_b=conv_b, w_q_ml=w_q_ml, w_k_ml=w_k_ml,
               b_i=b_i, b_f=b_f, g_ml=g_ml, cmp_pe=cmp_pe, cmp_w1=cmp_w1, cmp_b1=cmp_b1, cmp_w2=cmp_w2,
               cmp_b2=cmp_b2, w_out=w_out, g_post_mix=g_post_mix, g_pre_mlp=g_pre_mlp, w_up=w_up,
               w_down=w_down, g_post_mlp=g_post_mlp, w_pl=w_pl, g_pl=g_pl, w_pl_gate=w_pl_gate)
    Bp, Tp, D = x_prompt.shape
    Bs, Ts, _ = x_sample.shape
    depth = w_in.shape[0]
    d_ml = conv_w.shape[2]
    d_nsa = w_out.shape[1] - d_ml
    hp = x_prompt.reshape(Bp * Tp, D)
    hs = x_sample.reshape(Bs * Ts, D)
    n_pool, page = cache_kv.shape[1:3]
    kv_t = jnp.transpose(cache_kv, (0, 1, 3, 4, 5, 2)).reshape(depth * n_pool, cache_kv.shape[3], -1, page)
    win_t = jnp.transpose(cache_win, (0, 1, 3, 4, 5, 2)).reshape(depth * Bs, cache_win.shape[3], -1, cache_win.shape[2])
    c_all = state_C.reshape((depth * Bs,) + state_C.shape[2:])
    prm["w_up_all"] = w_up.astype(BF16).reshape(-1, w_up.shape[2])
    prm["w_down_all"] = w_down.astype(BF16).reshape(-1, w_down.shape[2])
    sp, ss = [], []
    c_stack = win_stack = None
    for i in range(depth):
        lw = _layer_weights(i, prm, d_ml, d_nsa)
        mem = dict(kv=kv_t, page_table=page_table + i * n_pool, win=win_t, C=c_all, n=state_n[i],
                   m=state_m[i], conv=state_conv[i], c_stack=c_stack, win_stack=win_stack)
        hp, st_p = _layer(hp, p_prompt.reshape(depth * Bp * Tp, -1), i, prm, lw, None, B=Bp, T=Tp)
        hs, st_s = _layer(hs, p_sample.reshape(depth * Bs * Ts, -1), i, prm, lw, mem, B=Bs, T=Ts)
        win_stack, c_stack = st_s[1], st_s[2]
        sp.append(st_p)
        ss.append(st_s)

    stk = lambda lst, j: jnp.stack([s[j] for s in lst])
    G, dk = NSA_KV_HEADS, NSA_HEAD_DIM
    win_sample = win_stack.reshape(depth, Bs, win_stack.shape[1], G, dk, win_stack.shape[3]).transpose(0, 1, 5, 2, 3, 4)
    return (hp.reshape(Bp, Tp, D), hs.reshape(Bs, Ts, D), stk(sp, 0), stk(ss, 0), stk(sp, 1), win_sample,
            stk(sp, 2), c_stack.reshape(state_C.shape), stk(sp, 3), stk(ss, 3), stk(sp, 4), stk(ss, 4),
            stk(sp, 5), stk(ss, 5))
```
